```python
import jax, jax.numpy as jnp
from jax import lax
import numpy as np

D_MODEL = 4096
BATCH = 4
SEQ = 4096
DEPTH = 1

PLE_DIM = 256
MIX_WIDTH = D_MODEL
POOL_WIDTH = MIX_WIDTH // 2
POOL_WINDOWS = (2, 4, 8, 16)
POOL_GROUP = POOL_WIDTH // len(POOL_WINDOWS)
HEAD_DIM = 128
N_Q_HEADS = (MIX_WIDTH - POOL_WIDTH) // HEAD_DIM
N_KV_HEADS = 4
Q_PER_KV = N_Q_HEADS // N_KV_HEADS
NSA_WIDTH = N_Q_HEADS * HEAD_DIM
KV_WIDTH = N_KV_HEADS * HEAD_DIM
N_GATES = 3
IN_WIDTH = POOL_WIDTH + NSA_WIDTH + 6 * KV_WIDTH + N_GATES * N_Q_HEADS
CMP_BLOCK = 32
CMP_STRIDE = 16
SEL_BLOCK = 64
N_SEL = 16
WINDOW = 512
WIN_QBLOCK = 128
SEL_QCHUNK = 32
ROPE_THETA = 10000.0
N_EXPERTS = 32
TOP_K = 4
D_FF = D_MODEL // 4
SWIGLU_LIMIT = 7.0
SWIGLU_ALPHA = 1.702
EXPERT_ROW_BLOCK = 256
DEEPNORM_ALPHA = (2 * DEPTH) ** 0.25
DEEPNORM_BETA = (8 * DEPTH) ** -0.25
LN_EPS = 1e-5
NEG_INF = -1e30
FORCE_SCORE = 1e4

kernel_name = "hymba_pool_nsa_moe_deepnorm_block"


def layer_norm(x, g, b):
    xf = x.astype(jnp.float32)
    mu = jnp.mean(xf, axis=-1, keepdims=True)
    var = jnp.mean(jnp.square(xf - mu), axis=-1, keepdims=True)
    return ((xf - mu) * lax.rsqrt(var + LN_EPS) * g.astype(jnp.float32) + b.astype(jnp.float32)).astype(x.dtype)


def rope_angles(pos):
    inv = ROPE_THETA ** (-jnp.arange(0, HEAD_DIM, 2, dtype=jnp.float32) / HEAD_DIM)
    ang = pos.astype(jnp.float32)[:, None] * inv[None, :]
    return jnp.cos(ang), jnp.sin(ang)


def apply_rope(x, cos, sin):
    x1, x2 = jnp.split(x, 2, axis=-1)
    c = cos.astype(x.dtype)
    s = sin.astype(x.dtype)
    return jnp.concatenate([x1 * c - x2 * s, x2 * c + x1 * s], axis=-1)


def pool_mixer(u, w_pool, pool_scale):
    B, S, _ = u.shape
    ug = u.reshape(B, S, len(POOL_WINDOWS), POOL_GROUP).astype(jnp.float32)
    cs = jnp.concatenate([jnp.zeros_like(ug[:, :1]), jnp.cumsum(ug, axis=1)], axis=1)
    t = jnp.arange(S)
    outs = []
    for gi, w in enumerate(POOL_WINDOWS):
        lo = jnp.maximum(t + 1 - w, 0)
        cnt = (t + 1 - lo).astype(jnp.float32)
        outs.append((cs[:, 1:, gi] - cs[:, lo, gi]) / cnt[None, :, None] - ug[:, :, gi])
    pooled = jnp.stack(outs, axis=2).astype(u.dtype)
    mixed = jnp.einsum('bsgc,gcd->bsgd', pooled, w_pool)
    return mixed.reshape(B, S, POOL_WIDTH) * pool_scale


def compress_kv(kv, pe, w1, w2):
    S = kv.shape[2]
    n_cmp = (S - CMP_BLOCK) // CMP_STRIDE + 1
    idx = jnp.arange(n_cmp)[:, None] * CMP_STRIDE + jnp.arange(CMP_BLOCK)[None, :]
    blocks = kv[:, :, idx] + pe
    hid = jax.nn.gelu(jnp.einsum('bgnld,lde->bgne', blocks, w1))
    return jnp.einsum('bgne,ef->bgnf', hid, w2)


def selection_overlap(n_cmp, n_sel):
    cs = np.arange(n_cmp) * CMP_STRIDE
    ce = cs + CMP_BLOCK
    ss = np.arange(n_sel) * SEL_BLOCK
    se = ss + SEL_BLOCK
    ov = np.clip(np.minimum(ce[:, None], se[None, :]) - np.maximum(cs[:, None], ss[None, :]), 0, None)
    return jnp.asarray(ov / CMP_BLOCK, dtype=jnp.float32)


def selected_attend(q, k, v, sel_idx):
    B, G, Hg, S, D = q.shape
    K = sel_idx.shape[-1]
    n_sel = S // SEL_BLOCK
    n_chunk = S // SEL_QCHUNK
    kb = k.reshape(B, G, n_sel, SEL_BLOCK, D)
    vb = v.reshape(B, G, n_sel, SEL_BLOCK, D)
    bi = jnp.arange(B)[:, None, None, None]
    gi = jnp.arange(G)[None, :, None, None]
    scale = HEAD_DIM ** -0.5
    qc = q.reshape(B, G, Hg, n_chunk, SEL_QCHUNK, D).transpose(3, 0, 1, 2, 4, 5)
    ic = sel_idx.reshape(B, G, n_chunk, SEL_QCHUNK, K).transpose(2, 0, 1, 3, 4)
    starts = jnp.arange(n_chunk) * SEL_QCHUNK

    def one(args):
        qx, ix, t0 = args
        kg = kb[bi, gi, ix]
        vg = vb[bi, gi, ix]
        s = jnp.einsum('bghcd,bgcnld->bghcnl', qx, kg, preferred_element_type=jnp.float32) * scale
        kpos = ix[..., None] * SEL_BLOCK + jnp.arange(SEL_BLOCK)
        tq = t0 + jnp.arange(SEL_QCHUNK)
        mask = kpos <= tq[None, None, :, None, None]
        s = jnp.where(mask[:, :, None], s, NEG_INF)
        pr = jax.nn.softmax(s.reshape(B, G, Hg, SEL_QCHUNK, K * SEL_BLOCK), axis=-1).reshape(s.shape)
        return jnp.einsum('bghcnl,bgcnld->bghcd', pr.astype(vg.dtype), vg)

    out = lax.map(one, (qc, ic, starts))
    return out.transpose(1, 2, 3, 0, 4, 5).reshape(B, G, Hg, S, D)


def band_keys(x, nb, npv):
    B, G, S, D = x.shape
    xb = x.reshape(B, G, nb, WIN_QBLOCK, D)
    xp = jnp.concatenate([jnp.zeros((B, G, npv, WIN_QBLOCK, D), x.dtype), xb], axis=2)
    return jnp.concatenate([xp[:, :, j:j + nb] for j in range(npv + 1)], axis=3)


def window_attend(q, k, v):
    B, G, Hg, S, D = q.shape
    nb = S // WIN_QBLOCK
    npv = WINDOW // WIN_QBLOCK
    kw_len = (npv + 1) * WIN_QBLOCK
    kw = band_keys(k, nb, npv).transpose(2, 0, 1, 3, 4)
    vw = band_keys(v, nb, npv).transpose(2, 0, 1, 3, 4)
    qb = q.reshape(B, G, Hg, nb, WIN_QBLOCK, D).transpose(3, 0, 1, 2, 4, 5)
    qpos = jnp.arange(WIN_QBLOCK)[:, None]
    krel = jnp.arange(kw_len)[None, :] - npv * WIN_QBLOCK
    base = (krel <= qpos) & (krel > qpos - WINDOW)
    scale = HEAD_DIM ** -0.5

    def one(args):
        qx, kx, vx, i = args
        m = base & (krel + i * WIN_QBLOCK >= 0)
        s = jnp.einsum('bghqd,bgkd->bghqk', qx, kx, preferred_element_type=jnp.float32) * scale
        pr = jax.nn.softmax(jnp.where(m, s, NEG_INF), axis=-1)
        return jnp.einsum('bghqk,bgkd->bghqd', pr.astype(vx.dtype), vx)

    out = lax.map(one, (qb, kw, vw, jnp.arange(nb)))
    return out.transpose(1, 2, 3, 0, 4, 5).reshape(B, G, Hg, S, D)


def nsa_mixer(q, kv, gates, cmp_k_pe, cmp_k_w1, cmp_k_w2, cmp_v_pe, cmp_v_w1, cmp_v_w2):
    B, S, _ = q.shape
    G, Hg, D = N_KV_HEADS, Q_PER_KV, HEAD_DIM
    scale = HEAD_DIM ** -0.5
    t = jnp.arange(S)
    cos, sin = rope_angles(t)
    q = apply_rope(q.reshape(B, S, G, Hg, D).transpose(0, 2, 3, 1, 4), cos, sin)
    k_cmp, v_cmp, k_slc, v_slc, k_win, v_win = [
        a.reshape(B, S, G, D).transpose(0, 2, 1, 3) for a in jnp.split(kv, 6, axis=-1)]
    k_slc = apply_rope(k_slc, cos, sin)
    k_win = apply_rope(k_win, cos, sin)

    kc = compress_kv(k_cmp, cmp_k_pe, cmp_k_w1, cmp_k_w2)
    vc = compress_kv(v_cmp, cmp_v_pe, cmp_v_w1, cmp_v_w2)
    n_cmp = kc.shape[2]
    cmp_end = jnp.arange(n_cmp) * CMP_STRIDE + CMP_BLOCK - 1
    cc, sc = rope_angles(cmp_end)
    kc = apply_rope(kc, cc, sc)
    s = jnp.einsum('bghsd,bgnd->bghsn', q, kc, preferred_element_type=jnp.float32) * scale
    cvalid = cmp_end[None, :] <= t[:, None]
    pc = jnp.where(cvalid, jax.nn.softmax(jnp.where(cvalid, s, NEG_INF), axis=-1), 0.0)
    o_cmp = jnp.einsum('bghsn,bgnd->bghsd', pc.astype(vc.dtype), vc)

    n_sel = S // SEL_BLOCK
    p_sel = jnp.einsum('bghsn,nj->bgsj', pc, selection_overlap(n_cmp, n_sel))
    blk = jnp.arange(n_sel)[None, :]
    tb = (t // SEL_BLOCK)[:, None]
    valid = blk <= tb
    forced = (blk == 0) | (blk == tb) | (blk == tb - 1)
    score = jnp.where(forced, FORCE_SCORE, jnp.where(valid, p_sel, -1.0))
    _, sel_idx = lax.top_k(score, min(N_SEL, n_sel))
    o_slc = selected_attend(q, k_slc, v_slc, sel_idx)

    o_win = window_attend(q, k_win, v_win)

    g = jax.nn.sigmoid(gates.astype(jnp.float32)).reshape(B, S, G, Hg, N_GATES)
    g = g.transpose(0, 2, 3, 1, 4).astype(q.dtype)
    o = g[..., 0:1] * o_cmp + g[..., 1:2] * o_slc + g[..., 2:3] * o_win
    return o.transpose(0, 3, 1, 2, 4).reshape(B, S, NSA_WIDTH)


def moe_ffn(h, w_router, b_router, w_up, b_up, w_down, b_down):
    B, S, D = h.shape
    T = B * S
    A = T * TOP_K
    xt = h.reshape(T, D)
    logits = jnp.einsum('td,de->te', xt, w_router, preferred_element_type=jnp.float32) + b_router.astype(jnp.float32)
    top_logit, top_e = lax.top_k(logits, TOP_K)
    top_w = jax.nn.softmax(top_logit, axis=-1)
    flat_e = top_e.reshape(-1)
    order = jnp.argsort(flat_e)
    sorted_e = flat_e[order]
    counts = jnp.bincount(flat_e, length=N_EXPERTS)
    padded = (counts + EXPERT_ROW_BLOCK - 1) // EXPERT_ROW_BLOCK * EXPERT_ROW_BLOCK
    pad_end = jnp.cumsum(padded)
    pad_start = pad_end - padded
    grp_start = jnp.cumsum(counts) - counts
    dest = pad_start[sorted_e] + jnp.arange(A) - grp_start[sorted_e]
    n_blocks = -(-A // EXPERT_ROW_BLOCK) + N_EXPERTS
    n_rows = n_blocks * EXPERT_ROW_BLOCK
    row_tok = jnp.zeros((n_rows,), jnp.int32).at[dest].set((order // TOP_K).astype(jnp.int32))
    row_w = jnp.zeros((n_rows,), jnp.float32).at[dest].set(top_w.reshape(-1)[order])
    blk_e = jnp.minimum(jnp.searchsorted(pad_end, jnp.arange(n_blocks) * EXPERT_ROW_BLOCK, side='right'), N_EXPERTS - 1)

    def step(acc, args):
        tok, wgt, e = args
        xb = xt[tok]
        hcat = xb @ w_up[e] + b_up[e]
        gate = jnp.minimum(hcat[:, 0::2], SWIGLU_LIMIT)
        up = jnp.clip(hcat[:, 1::2], -SWIGLU_LIMIT, SWIGLU_LIMIT)
        act = gate * jax.nn.sigmoid(SWIGLU_ALPHA * gate) * (up + 1)
        yb = act @ w_down[e] + b_down[e]
        return acc.at[tok].add(yb.astype(jnp.float32) * wgt[:, None]), None

    acc, _ = lax.scan(step, jnp.zeros((T, D), jnp.float32),
                      (row_tok.reshape(n_blocks, EXPERT_ROW_BLOCK), row_w.reshape(n_blocks, EXPERT_ROW_BLOCK), blk_e))
    return acc.astype(h.dtype).reshape(B, S, D)


def setup_inputs(seed: int = 0) -> dict:
    key = jax.random.key(seed)
    ks = jax.random.split(key, 25)
    f32 = jnp.float32
    L = DEPTH

    def nrm(k, shape, scale):
        return jax.random.normal(k, shape, f32) * scale

    return {
        "x": nrm(ks[0], (BATCH, SEQ, D_MODEL), 1.0),
        "p": nrm(ks[1], (DEPTH, BATCH, SEQ, PLE_DIM), 1.0),
        "w_in": nrm(ks[2], (L, D_MODEL, IN_WIDTH), D_MODEL ** -0.5),
        "w_pool": nrm(ks[3], (L, len(POOL_WINDOWS), POOL_GROUP, POOL_GROUP), POOL_GROUP ** -0.5),
        "pool_scale": 1.0 + nrm(ks[4], (L, POOL_WIDTH), 0.02),
        "cmp_k_pe": nrm(ks[5], (L, CMP_BLOCK, HEAD_DIM), 0.1),
        "cmp_k_w1": nrm(ks[6], (L, CMP_BLOCK, HEAD_DIM, HEAD_DIM), (CMP_BLOCK * HEAD_DIM) ** -0.5),
        "cmp_k_w2": nrm(ks[7], (L, HEAD_DIM, HEAD_DIM), HEAD_DIM ** -0.5),
        "cmp_v_pe": nrm(ks[8], (L, CMP_BLOCK, HEAD_DIM), 0.1),
        "cmp_v_w1": nrm(ks[9], (L, CMP_BLOCK, HEAD_DIM, HEAD_DIM), (CMP_BLOCK * HEAD_DIM) ** -0.5),
        "cmp_v_w2": nrm(ks[10], (L, HEAD_DIM, HEAD_DIM), HEAD_DIM ** -0.5),
        "w_out": nrm(ks[11], (L, MIX_WIDTH, D_MODEL), MIX_WIDTH ** -0.5 * DEEPNORM_BETA),
        "ln1_g": 1.0 + nrm(ks[12], (L, D_MODEL), 0.02),
        "ln1_b": nrm(ks[13], (L, D_MODEL), 0.02),
        "w_router": nrm(ks[14], (L, D_MODEL, N_EXPERTS), D_MODEL ** -0.5),
        "b_router": nrm(ks[15], (L, N_EXPERTS), 0.01),
        "w_up": nrm(ks[16], (L, N_EXPERTS, D_MODEL, 2 * D_FF), D_MODEL ** -0.5),
        "b_up": nrm(ks[17], (L, N_EXPERTS, 2 * D_FF), 0.01),
        "w_down": nrm(ks[18], (L, N_EXPERTS, D_FF, D_MODEL), D_FF ** -0.5 * DEEPNORM_BETA),
        "b_down": nrm(ks[19], (L, N_EXPERTS, D_MODEL), 0.01),
        "w_ple_gate": nrm(ks[20], (L, D_MODEL, D_MODEL), D_MODEL ** -0.5),
        "b_ple_gate": nrm(ks[21], (L, D_MODEL), 0.01),
        "w_ple": nrm(ks[22], (L, PLE_DIM, D_MODEL), PLE_DIM ** -0.5 * DEEPNORM_BETA),
        "ln2_g": 1.0 + nrm(ks[23], (L, D_MODEL), 0.02),
        "ln2_b": nrm(ks[24], (L, D_MODEL), 0.02),
    }


def reference(x, p, w_in, w_pool, pool_scale, cmp_k_pe, cmp_k_w1, cmp_k_w2, cmp_v_pe, cmp_v_w1, cmp_v_w2,
              w_out, ln1_g, ln1_b, w_router, b_router, w_up, b_up, w_down, b_down,
              w_ple_gate, b_ple_gate, w_ple, ln2_g, ln2_b):
    splits = [POOL_WIDTH, POOL_WIDTH + NSA_WIDTH, POOL_WIDTH + NSA_WIDTH + 6 * KV_WIDTH]
    for i in range(DEPTH):
        proj = jnp.einsum('bsd,de->bse', x, w_in[i])
        u_pool, q, kv, gates = jnp.split(proj, splits, axis=-1)
        y_pool = pool_mixer(u_pool, w_pool[i], pool_scale[i])
        y_nsa = nsa_mixer(q, kv, gates, cmp_k_pe[i], cmp_k_w1[i], cmp_k_w2[i],
                          cmp_v_pe[i], cmp_v_w1[i], cmp_v_w2[i])
        mix = jnp.einsum('bse,ed->bsd', jnp.concatenate([y_pool, y_nsa], axis=-1), w_out[i])
        h = layer_norm(DEEPNORM_ALPHA * x + mix, ln1_g[i], ln1_b[i])
        ffn = moe_ffn(h, w_router[i], b_router[i], w_up[i], b_up[i], w_down[i], b_down[i])
        gate = jax.nn.sigmoid((h @ w_ple_gate[i] + b_ple_gate[i]).astype(jnp.float32)).astype(h.dtype)
        ple = gate * (p[i] @ w_ple[i])
        x = layer_norm(DEEPNORM_ALPHA * h + ffn + ple, ln2_g[i], ln2_b[i])
    return x
```

```python
import functools
import math

import jax
import jax.numpy as jnp
import numpy as np
from jax import lax
from jax.experimental import pallas as pl
from jax.experimental.pallas import tpu as pltpu

HEAD_DIM = 128
LANES = 128
N_KV_GROUPS_FIELDS = 6
N_GATES = 3
POOL_WINDOWS = (2, 4, 8, 16)
CMP_BLOCK = 32
CMP_STRIDE = 16
SEL_BLOCK = 64
N_SEL = 16
WINDOW = 512
ROPE_THETA = 10000.0
TOP_K = 4
SWIGLU_LIMIT = 7.0
SWIGLU_ALPHA = 1.702
EXPERT_ROW_BLOCK = 256
LN_EPS = 1e-5
NEG_INF = -1e30
FORCE_SCORE = 1e4
MIB = 1024 * 1024

BF16 = jnp.bfloat16
F32 = jnp.float32
NT_DIMS = (((1,), (1,)), ((), ()))


def _params(sem, vmem_mib=None):
    kw = dict(dimension_semantics=sem)
    if vmem_mib is not None:
        kw["vmem_limit_bytes"] = vmem_mib * MIB
    return pltpu.CompilerParams(**kw)


def _pick_tile(n, candidates):
    for c in candidates:
        if n % c == 0:
            return c
    raise ValueError(f"no tile for {n}")


def _mm_kernel(x_ref, w_ref, o_ref):
    @pl.when(pl.program_id(2) == 0)
    def _():
        o_ref[...] = jnp.zeros_like(o_ref)

    o_ref[...] += jnp.dot(x_ref[...], w_ref[...], preferred_element_type=F32)


def _matmul_cols(x, w, col0, ncols):
    M, K = x.shape
    tm = _pick_tile(M, (1024, 512, 256))
    tk = _pick_tile(K, (512, 256, 128))
    tn = _pick_tile(math.gcd(col0, ncols) if col0 else ncols, (1024, 512, 256, 128))
    off = col0 // tn
    return pl.pallas_call(
        _mm_kernel,
        grid=(M // tm, ncols // tn, K // tk),
        in_specs=[pl.BlockSpec((tm, tk), lambda i, j, k: (i, k)),
                  pl.BlockSpec((tk, tn), lambda i, j, k: (k, j + off))],
        out_specs=pl.BlockSpec((tm, tn), lambda i, j, k: (i, j)),
        out_shape=jax.ShapeDtypeStruct((M, ncols), F32),
        compiler_params=_params(("parallel", "parallel", "arbitrary"), 40),
    )(x, w)


def _pool_kernel(u_ref, prev_ref, w_ref, sc_ref, o_ref, ext_ref, *, ts, halo):
    g = pl.program_id(0)
    i = pl.program_id(2)
    cur = u_ref[...]
    ext_ref[pl.ds(halo, ts), :] = cur

    @pl.when(i == 0)
    def _():
        ext_ref[pl.ds(0, halo), :] = jnp.zeros((halo, cur.shape[1]), F32)

    @pl.when(i > 0)
    def _():
        ext_ref[pl.ds(0, halo), :] = prev_ref[...]

    def back(d):
        return ext_ref[pl.ds(halo - d, ts), :]

    s2 = cur + back(1)
    s4 = s2 + back(2) + back(3)
    s8 = s4 + back(4) + back(5) + back(6) + back(7)
    s16 = s8
    for d in range(8, 16):
        s16 = s16 + back(d)
    ssum = jnp.where(g == 0, s2, jnp.where(g == 1, s4, jnp.where(g == 2, s8, s16)))
    win = jnp.left_shift(2, g)
    t = i * ts + lax.broadcasted_iota(jnp.int32, (ts, 1), 0)
    cnt = jnp.minimum(t + 1, win).astype(F32)
    pooled = ssum / cnt - cur
    mixed = jnp.dot(pooled.astype(BF16), w_ref[0], preferred_element_type=F32)
    o_ref[...] = (mixed * sc_ref[...]).astype(o_ref.dtype)


def _pool_mixer(proj_a, w_pool_b, pool_scale, B, S, pool_w):
    T = B * S
    ng = len(POOL_WINDOWS)
    pg = pool_w // ng
    halo = POOL_WINDOWS[-1]
    ts = _pick_tile(S, (512, 256, 128))
    ns = S // ts
    hb = ts // halo
    kern = functools.partial(_pool_kernel, ts=ts, halo=halo)
    return pl.pallas_call(
        kern,
        grid=(ng, B, ns),
        in_specs=[
            pl.BlockSpec((ts, pg), lambda g, b, i: (b * ns + i, g)),
            pl.BlockSpec((halo, pg), lambda g, b, i: (jnp.maximum((b * ns + i) * hb - 1, 0), g)),
            pl.BlockSpec((1, pg, pg), lambda g, b, i: (g, 0, 0)),
            pl.BlockSpec((1, pg), lambda g, b, i: (0, g)),
        ],
        out_specs=pl.BlockSpec((ts, pg), lambda g, b, i: (b * ns + i, g)),
        out_shape=jax.ShapeDtypeStruct((T, pool_w), BF16),
        scratch_shapes=[pltpu.VMEM((ts + halo, pg), F32)],
        compiler_params=_params(("parallel", "parallel", "arbitrary"), 32),
    )(proj_a, proj_a, w_pool_b, pool_scale.reshape(1, pool_w))


def _rope_tables(pos):
    inv = ROPE_THETA ** (-jnp.arange(0, HEAD_DIM, 2, dtype=F32) / HEAD_DIM)
    ang = pos.astype(F32)[:, None] * inv[None, :]
    c, s = jnp.cos(ang), jnp.sin(ang)
    return jnp.concatenate([c, c], axis=-1), jnp.concatenate([-s, s], axis=-1)


def _rope(x, cos2, sin2):
    return x * cos2 + pltpu.roll(x, HEAD_DIM // 2, 1) * sin2


def _compress_kernel(x_ref, pe_ref, w1_ref, w2_ref, cos_ref, sin_ref, o_ref, a_ref, b_ref, *, rope, nl):
    l = pl.program_id(2)

    @pl.when(l == 0)
    def _():
        a_ref[...] = jnp.zeros_like(a_ref)
        b_ref[...] = jnp.zeros_like(b_ref)

    x = x_ref[0]
    xa = (x + pe_ref[pl.ds(l, 1), :]).astype(BF16)
    xb = (x + pe_ref[pl.ds(l + nl, 1), :]).astype(BF16)
    a_ref[...] += jnp.dot(xa, w1_ref[l].astype(BF16), preferred_element_type=F32)
    b_ref[...] += jnp.dot(xb, w1_ref[l + nl].astype(BF16), preferred_element_type=F32)

    @pl.when(l == nl - 1)
    def _():
        nc = a_ref.shape[0]
        hid_pre = a_ref[...] + pltpu.roll(b_ref[...], nc - 1, 0)
        hid = jax.nn.gelu(hid_pre)
        out = jnp.dot(hid.astype(BF16), w2_ref[...].astype(BF16), preferred_element_type=F32)
        if rope:
            out = _rope(out, cos_ref[...], sin_ref[...])
        o_ref[0, 0] = out.astype(o_ref.dtype)


def _compress(kvc3, field, pe, w1, w2, cos2, sin2, B, G, rope):
    nc = kvc3.shape[1]
    nl = CMP_STRIDE
    cols_per_tok = kvc3.shape[2] // nl // LANES
    kern = functools.partial(_compress_kernel, rope=rope, nl=nl)
    return pl.pallas_call(
        kern,
        grid=(B, G, nl),
        in_specs=[
            pl.BlockSpec((1, nc, LANES), lambda b, g, l: (b, 0, l * cols_per_tok + field * G + g)),
            pl.BlockSpec((CMP_BLOCK, HEAD_DIM), lambda b, g, l: (0, 0)),
            pl.BlockSpec((CMP_BLOCK, HEAD_DIM, HEAD_DIM), lambda b, g, l: (0, 0, 0)),
            pl.BlockSpec((HEAD_DIM, HEAD_DIM), lambda b, g, l: (0, 0)),
            pl.BlockSpec((nc, HEAD_DIM), lambda b, g, l: (0, 0)),
            pl.BlockSpec((nc, HEAD_DIM), lambda b, g, l: (0, 0)),
        ],
        out_specs=pl.BlockSpec((1, 1, nc, HEAD_DIM), lambda b, g, l: (b, g, 0, 0)),
        out_shape=jax.ShapeDtypeStruct((B, G, nc, HEAD_DIM), BF16),
        scratch_shapes=[pltpu.VMEM((nc, HEAD_DIM), F32), pltpu.VMEM((nc, HEAD_DIM), F32)],
        compiler_params=_params(("parallel", "parallel", "arbitrary"), 32),
    )(kvc3, pe, w1, w2, cos2, sin2)


def _rope_prep_kernel(q_ref, kv_ref, cos_ref, sin_ref, qo_ref, ks_ref, vs_ref, kw_ref, vw_ref, *, nh, G, ts):
    i = pl.program_id(1)
    c = cos_ref[...]
    s = sin_ref[...]
    kvw = G * HEAD_DIM
    for h in range(nh):
        qo_ref[0, h] = _rope(q_ref[:, h * HEAD_DIM:(h + 1) * HEAD_DIM], c, s).astype(BF16)
    blk = (i * ts + lax.broadcasted_iota(jnp.int32, (ts, LANES), 0)) // SEL_BLOCK
    onehot = (blk == lax.broadcasted_iota(jnp.int32, (ts, LANES), 1)).astype(BF16)
    for g in range(G):
        lo = g * HEAD_DIM
        ks_ref[0, g, :, 0:HEAD_DIM] = _rope(kv_ref[:, lo:lo + HEAD_DIM], c, s).astype(BF16)
        ks_ref[0, g, :, HEAD_DIM:2 * HEAD_DIM] = onehot
        vs_ref[0, g] = kv_ref[:, kvw + lo:kvw + lo + HEAD_DIM].astype(BF16)
        kw_ref[0, g] = _rope(kv_ref[:, 2 * kvw + lo:2 * kvw + lo + HEAD_DIM], c, s).astype(BF16)
        vw_ref[0, g] = kv_ref[:, 3 * kvw + lo:3 * kvw + lo + HEAD_DIM].astype(BF16)


def _rope_prep(proj_a, proj_c, cos2, sin2, B, S, nsa_w, G):
    nh = nsa_w // HEAD_DIM
    ts = _pick_tile(S, (256, 128))
    ns = S // ts
    kern = functools.partial(_rope_prep_kernel, nh=nh, G=G, ts=ts)
    hm = lambda b, i: (b, 0, i, 0)
    return pl.pallas_call(
        kern,
        grid=(B, ns),
        in_specs=[
            pl.BlockSpec((ts, nsa_w), lambda b, i: (b * ns + i, 1)),
            pl.BlockSpec((ts, 4 * G * HEAD_DIM), lambda b, i: (b * ns + i, 0)),
            pl.BlockSpec((ts, HEAD_DIM), lambda b, i: (i, 0)),
            pl.BlockSpec((ts, HEAD_DIM), lambda b, i: (i, 0)),
        ],
        out_specs=[
            pl.BlockSpec((1, nh, ts, HEAD_DIM), hm),
            pl.BlockSpec((1, G, ts, 2 * HEAD_DIM), hm),
            pl.BlockSpec((1, G, ts, HEAD_DIM), hm),
            pl.BlockSpec((1, G, ts, HEAD_DIM), hm),
            pl.BlockSpec((1, G, ts, HEAD_DIM), hm),
        ],
        out_shape=[
            jax.ShapeDtypeStruct((B, nh, S, HEAD_DIM), BF16),
            jax.ShapeDtypeStruct((B, G, S, 2 * HEAD_DIM), BF16),
            jax.ShapeDtypeStruct((B, G, S, HEAD_DIM), BF16),
            jax.ShapeDtypeStruct((B, G, S, HEAD_DIM), BF16),
            jax.ShapeDtypeStruct((B, G, S, HEAD_DIM), BF16),
        ],
        compiler_params=_params(("parallel", "parallel"), 40),
    )(proj_a, proj_c, cos2, sin2)


def _cmp_select_kernel(q_ref, kc_ref, vc_ref, ovl_ref, o_ref, bias_ref, *, hg, tq, n_sel, n_keep, scale):
    i = pl.program_id(2)
    nc = kc_ref.shape[2]
    kc = kc_ref[0, 0]
    vc = vc_ref[0, 0]
    t = i * tq + lax.broadcasted_iota(jnp.int32, (tq, nc), 0)
    n = lax.broadcasted_iota(jnp.int32, (tq, nc), 1)
    valid = (n * CMP_STRIDE + (CMP_BLOCK - 1) <= t) & (n < nc - 1)
    psum = jnp.zeros((tq, nc), F32)
    for h in range(hg):
        s = lax.dot_general(q_ref[0, h], kc, NT_DIMS, preferred_element_type=F32) * scale
        s = jnp.where(valid, s, NEG_INF)
        m = jnp.max(s, axis=-1, keepdims=True)
        e = jnp.where(valid, jnp.exp(s - m), 0.0)
        den = jnp.sum(e, axis=-1, keepdims=True)
        pc = e / jnp.where(den > 0.0, den, 1.0)
        o_ref[0, h] = jnp.dot(pc.astype(BF16), vc, preferred_element_type=F32)
        psum = psum + pc

    imp = lax.dot_general(ovl_ref[...], psum, NT_DIMS, preferred_element_type=F32,
                          precision=lax.Precision.HIGHEST)
    j = lax.broadcasted_iota(jnp.int32, (n_sel, tq), 0)
    tb = (i * tq + lax.broadcasted_iota(jnp.int32, (n_sel, tq), 1)) // SEL_BLOCK
    forced = (j == 0) | (j == tb) | (j == tb - 1)
    score = jnp.where(forced, FORCE_SCORE, jnp.where(j <= tb, imp, -1.0))
    rank = jnp.zeros((n_sel, tq), jnp.int32)
    for jp in range(n_sel):
        row = score[jp:jp + 1, :]
        ahead = (row > score) | ((row == score) & (jp < j))
        rank = rank + ahead.astype(jnp.int32)
    bias_t = jnp.where(rank < n_keep, 0.0, NEG_INF)
    if n_sel < LANES:
        bias_t = jnp.concatenate([bias_t, jnp.zeros((LANES - n_sel, tq), F32)], axis=0)
    bias_ref[0, 0] = bias_t.T.astype(BF16)


def _overlap_t(n_cmp_pad, n_sel):
    n_cmp = n_cmp_pad - 1
    cs = np.arange(n_cmp) * CMP_STRIDE
    ce = cs + CMP_BLOCK
    ss = np.arange(n_sel) * SEL_BLOCK
    se = ss + SEL_BLOCK
    ov = np.clip(np.minimum(ce[:, None], se[None, :]) - np.maximum(cs[:, None], ss[None, :]), 0, None)
    out = np.zeros((n_sel, n_cmp_pad), np.float32)
    out[:, :n_cmp] = (ov / CMP_BLOCK).T
    return jnp.asarray(out)


def _cmp_select(q_rot, kc, vc, B, S, G, hg):
    nc = kc.shape[2]
    n_sel = S // SEL_BLOCK
    assert n_sel <= LANES and n_sel % 8 == 0
    tq = _pick_tile(S, (256, 128))
    nq = S // tq
    nh = G * hg
    kern = functools.partial(_cmp_select_kernel, hg=hg, tq=tq, n_sel=n_sel,
                             n_keep=min(N_SEL, n_sel), scale=HEAD_DIM ** -0.5)
    return pl.pallas_call(
        kern,
        grid=(B, G, nq),
        in_specs=[
            pl.BlockSpec((1, hg, tq, HEAD_DIM), lambda b, g, i: (b, g, i, 0)),
            pl.BlockSpec((1, 1, nc, HEAD_DIM), lambda b, g, i: (b, g, 0, 0)),
            pl.BlockSpec((1, 1, nc, HEAD_DIM), lambda b, g, i: (b, g, 0, 0)),
            pl.BlockSpec((n_sel, nc), lambda b, g, i: (0, 0)),
        ],
        out_specs=[
            pl.BlockSpec((1, hg, tq, HEAD_DIM), lambda b, g, i: (b, g, i, 0)),
            pl.BlockSpec((1, 1, tq, LANES), lambda b, g, i: (b, g, i, 0)),
        ],
        out_shape=[
            jax.ShapeDtypeStruct((B, nh, S, HEAD_DIM), F32),
            jax.ShapeDtypeStruct((B, G, S, LANES), BF16),
        ],
        compiler_params=_params(("parallel", "parallel", "parallel"), 40),
    )(q_rot, kc, vc, _overlap_t(nc, n_sel))


def _flash_kernel(qi_ref, kj_ref, first_ref, last_ref, *refs, hg, tq, tk, window, use_bias, scale):
    if use_bias:
        q_ref, k_ref, v_ref, bias_ref, o_ref, m_ref, l_ref, acc_ref = refs
    else:
        q_ref, k_ref, v_ref, o_ref, m_ref, l_ref, acc_ref = refs
    p = pl.program_id(2)
    rows = hg * tq

    @pl.when(first_ref[p] == 1)
    def _():
        m_ref[...] = jnp.full_like(m_ref, NEG_INF)
        l_ref[...] = jnp.zeros_like(l_ref)
        acc_ref[...] = jnp.zeros_like(acc_ref)

    q = q_ref[0].reshape(rows, HEAD_DIM)
    if use_bias:
        b = bias_ref[0, 0]
        q = jnp.concatenate([q, jnp.concatenate([b] * hg, axis=0)], axis=1)
    s = lax.dot_general(q, k_ref[0, 0], NT_DIMS, preferred_element_type=F32) * scale
    tpos = qi_ref[p] * tq + (lax.broadcasted_iota(jnp.int32, (rows, tk), 0) & (tq - 1))
    kpos = kj_ref[p] * tk + lax.broadcasted_iota(jnp.int32, (rows, tk), 1)
    mask = kpos <= tpos
    if window is not None:
        mask = mask & (kpos > tpos - window)
    s = jnp.where(mask, s, NEG_INF)
    m_prev = m_ref[...]
    m_new = jnp.maximum(m_prev, jnp.max(s, axis=-1, keepdims=True))
    alpha = jnp.exp(m_prev - m_new)
    pr = jnp.exp(s - m_new)
    l_ref[...] = alpha * l_ref[...] + jnp.sum(pr, axis=-1, keepdims=True)
    acc_ref[...] = alpha * acc_ref[...] + jnp.dot(pr.astype(BF16), v_ref[0, 0], preferred_element_type=F32)
    m_ref[...] = m_new

    @pl.when(last_ref[p] == 1)
    def _():
        o_ref[0] = (acc_ref[...] / l_ref[...]).reshape(hg, tq, HEAD_DIM)


def _flash(q_rot, k, v, bias, B, S, G, hg, tq, tk, window):
    assert tq & (tq - 1) == 0
    nq = S // tq
    pairs = []
    for qi in range(nq):
        hi = (qi * tq + tq - 1) // tk
        lo = 0 if window is None else max(0, (qi * tq - window + 1) // tk)
        for kj in range(lo, hi + 1):
            pairs.append((qi, kj, int(kj == lo), int(kj == hi)))
    tab = [jnp.asarray(np.array([p[c] for p in pairs], np.int32)) for c in range(4)]
    dk = k.shape[-1]
    use_bias = bias is not None
    kern = functools.partial(_flash_kernel, hg=hg, tq=tq, tk=tk, window=window, use_bias=use_bias,
                             scale=HEAD_DIM ** -0.5)
    in_specs = [
        pl.BlockSpec((1, hg, tq, HEAD_DIM), lambda b, g, p, qi, kj, f, l: (b, g, qi[p], 0)),
        pl.BlockSpec((1, 1, tk, dk), lambda b, g, p, qi, kj, f, l: (b, g, kj[p], 0)),
        pl.BlockSpec((1, 1, tk, HEAD_DIM), lambda b, g, p, qi, kj, f, l: (b, g, kj[p], 0)),
    ]
    args = [q_rot, k, v]
    if use_bias:
        in_specs.append(pl.BlockSpec((1, 1, tq, LANES), lambda b, g, p, qi, kj, f, l: (b, g, qi[p], 0)))
        args.append(bias)
    rows = hg * tq
    return pl.pallas_call(
        kern,
        grid_spec=pltpu.PrefetchScalarGridSpec(
            num_scalar_prefetch=4,
            grid=(B, G, len(pairs)),
            in_specs=in_specs,
            out_specs=pl.BlockSpec((1, hg, tq, HEAD_DIM), lambda b, g, p, qi, kj, f, l: (b, g, qi[p], 0)),
            scratch_shapes=[pltpu.VMEM((rows, 1), F32), pltpu.VMEM((rows, 1), F32),
                            pltpu.VMEM((rows, HEAD_DIM), F32)],
        ),
        out_shape=jax.ShapeDtypeStruct((B, G * hg, S, HEAD_DIM), F32),
        compiler_params=_params(("parallel", "parallel", "arbitrary"), 40),
    )(*tab, *args)


def _gate_kernel(g_ref, oc_ref, os_ref, ow_ref, y_ref, *, nh):
    g = jax.nn.sigmoid(g_ref[...])
    for h in range(nh):
        c = N_GATES * h
        y = g[:, c:c + 1] * oc_ref[0, h] + g[:, c + 1:c + 2] * os_ref[0, h] + g[:, c + 2:c + 3] * ow_ref[0, h]
        y_ref[:, h * HEAD_DIM:(h + 1) * HEAD_DIM] = y.astype(y_ref.dtype)


def _gate_combine(gates, o_cmp, o_slc, o_win, B, S, nh):
    ts = _pick_tile(S, (256, 128))
    ns = S // ts
    hm = pl.BlockSpec((1, nh, ts, HEAD_DIM), lambda b, i: (b, 0, i, 0))
    return pl.pallas_call(
        functools.partial(_gate_kernel, nh=nh),
        grid=(B, ns),
        in_specs=[pl.BlockSpec((ts, LANES), lambda b, i: (b * ns + i, 0)), hm, hm, hm],
        out_specs=pl.BlockSpec((ts, nh * HEAD_DIM), lambda b, i: (b * ns + i, 0)),
        out_shape=jax.ShapeDtypeStruct((B * S, nh * HEAD_DIM), BF16),
        compiler_params=_params(("parallel", "parallel"), 40),
    )(gates, o_cmp, o_slc, o_win)


def _layer_norm(z, g, b):
    mu = jnp.mean(z, axis=-1, keepdims=True)
    zc = z - mu
    var = jnp.mean(zc * zc, axis=-1, keepdims=True)
    return zc * lax.rsqrt(var + LN_EPS) * g + b


def _mix_kernel(yp_ref, yn_ref, w_ref, x_ref, g_ref, b_ref, wr_ref, br_ref, h_ref, hb_ref, lg_ref, acc_ref,
                *, nkp, alpha):
    k = pl.program_id(1)

    @pl.when(k == 0)
    def _():
        acc_ref[...] = jnp.zeros_like(acc_ref)

    @pl.when(k < nkp)
    def _():
        acc_ref[...] += jnp.dot(yp_ref[...], w_ref[...], preferred_element_type=F32)

    @pl.when(k >= nkp)
    def _():
        acc_ref[...] += jnp.dot(yn_ref[...], w_ref[...], preferred_element_type=F32)

    @pl.when(k == pl.num_programs(1) - 1)
    def _():
        h = _layer_norm(alpha * x_ref[...] + acc_ref[...], g_ref[...], b_ref[...])
        h_ref[...] = h
        hb_ref[...] = h.astype(BF16)
        lg_ref[...] = jnp.dot(h, wr_ref[...], preferred_element_type=F32,
                              precision=lax.Precision.HIGHEST) + br_ref[...]


def _mix_ln_router(y_pool, y_nsa, w_out_b, x2, ln_g, ln_b, w_router_p, b_router_p, alpha):
    T, D = x2.shape
    kp = y_pool.shape[1]
    tm = _pick_tile(T, (256, 128))
    tk = _pick_tile(math.gcd(kp, y_nsa.shape[1]), (512, 256, 128))
    nkp = kp // tk
    nk = nkp + y_nsa.shape[1] // tk
    ne = w_router_p.shape[1]
    kern = functools.partial(_mix_kernel, nkp=nkp, alpha=alpha)
    row = lambda i, k: (i, 0)
    const = lambda i, k: (0, 0)
    return pl.pallas_call(
        kern,
        grid=(T // tm, nk),
        in_specs=[
            pl.BlockSpec((tm, tk), lambda i, k: (i, jnp.minimum(k, nkp - 1))),
            pl.BlockSpec((tm, tk), lambda i, k: (i, jnp.maximum(k - nkp, 0))),
            pl.BlockSpec((tk, D), lambda i, k: (k, 0)),
            pl.BlockSpec((tm, D), row),
            pl.BlockSpec((1, D), const),
            pl.BlockSpec((1, D), const),
            pl.BlockSpec((D, ne), const),
            pl.BlockSpec((1, ne), const),
        ],
        out_specs=[pl.BlockSpec((tm, D), row), pl.BlockSpec((tm, D), row), pl.BlockSpec((tm, ne), row)],
        out_shape=[jax.ShapeDtypeStruct((T, D), F32), jax.ShapeDtypeStruct((T, D), BF16),
                   jax.ShapeDtypeStruct((T, ne), F32)],
        scratch_shapes=[pltpu.VMEM((tm, D), F32)],
        compiler_params=_params(("parallel", "arbitrary"), 48),
    )(y_pool, y_nsa, w_out_b, x2, ln_g.reshape(1, D), ln_b.reshape(1, D), w_router_p, b_router_p)


def _row_copy(src_hbm, row, dst_ref, slot, sem):
    return pltpu.make_async_copy(src_hbm.at[pl.ds(row, 1)], dst_ref.at[pl.ds(slot, 1)], sem)


def _moe_up_kernel(tok_ref, be_ref, bv_ref, h_hbm, wg_ref, wu_ref, bg_ref, bu_ref, act_ref, xbuf_ref, sem, *, rb):
    i = pl.program_id(0)

    @pl.when(bv_ref[i] == 0)
    def _():
        act_ref[...] = jnp.zeros_like(act_ref)

    @pl.when(bv_ref[i] != 0)
    def _():
        def issue(r, carry):
            _row_copy(h_hbm, tok_ref[i * rb + r], xbuf_ref, r, sem).start()
            return carry

        lax.fori_loop(0, rb, issue, 0)

        def drain(r, carry):
            _row_copy(h_hbm, 0, xbuf_ref, r, sem).wait()
            return carry

        lax.fori_loop(0, rb, drain, 0)
        xb = xbuf_ref[...].astype(BF16)
        gate = jnp.dot(xb, wg_ref[0], preferred_element_type=F32) + bg_ref[0]
        up = jnp.dot(xb, wu_ref[0], preferred_element_type=F32) + bu_ref[0]
        gate = jnp.minimum(gate, SWIGLU_LIMIT)
        up = jnp.clip(up, -SWIGLU_LIMIT, SWIGLU_LIMIT)
        act = gate * jax.nn.sigmoid(SWIGLU_ALPHA * gate) * (up + 1.0)
        act_ref[...] = act.astype(act_ref.dtype)


def _moe_up(row_tok, blk_e, blk_valid, h, wg, wu, bg, bu):
    n_rows = row_tok.shape[0]
    rb = EXPERT_ROW_BLOCK
    E, D, F = wg.shape
    wmap = lambda i, tok, be, bv: (be[i], 0, 0)
    return pl.pallas_call(
        functools.partial(_moe_up_kernel, rb=rb),
        grid_spec=pltpu.PrefetchScalarGridSpec(
            num_scalar_prefetch=3,
            grid=(n_rows // rb,),
            in_specs=[
                pl.BlockSpec(memory_space=pl.ANY),
                pl.BlockSpec((1, D, F), wmap),
                pl.BlockSpec((1, D, F), wmap),
                pl.BlockSpec((1, 1, F), wmap),
                pl.BlockSpec((1, 1, F), wmap),
            ],
            out_specs=pl.BlockSpec((rb, F), lambda i, tok, be, bv: (i, 0)),
            scratch_shapes=[pltpu.VMEM((rb, D), F32), pltpu.SemaphoreType.DMA(())],
        ),
        out_shape=jax.ShapeDtypeStruct((n_rows, F), BF16),
        compiler_params=_params(("arbitrary",), 56),
    )(row_tok, blk_e, blk_valid, h, wg, wu, bg, bu)


def _moe_down_kernel(be_ref, bv_ref, act_ref, wd_ref, bd_ref, rw_ref, y_ref):
    i = pl.program_id(0)

    @pl.when(bv_ref[i] == 0)
    def _():
        y_ref[...] = jnp.zeros_like(y_ref)

    @pl.when(bv_ref[i] != 0)
    def _():
        y = jnp.dot(act_ref[...], wd_ref[0], preferred_element_type=F32) + bd_ref[0]
        y_ref[...] = y * rw_ref[...]


def _moe_down(blk_e, blk_valid, act, wd, bd, row_w):
    n_rows, F = act.shape
    rb = EXPERT_ROW_BLOCK
    E, _, D = wd.shape
    wmap = lambda i, be, bv: (be[i], 0, 0)
    return pl.pallas_call(
        _moe_down_kernel,
        grid_spec=pltpu.PrefetchScalarGridSpec(
            num_scalar_prefetch=2,
            grid=(n_rows // rb,),
            in_specs=[
                pl.BlockSpec((rb, F), lambda i, be, bv: (i, 0)),
                pl.BlockSpec((1, F, D), wmap),
                pl.BlockSpec((1, 1, D), wmap),
                pl.BlockSpec((rb, 1), lambda i, be, bv: (i, 0)),
            ],
            out_specs=pl.BlockSpec((rb, D), lambda i, be, bv: (i, 0)),
        ),
        out_shape=jax.ShapeDtypeStruct((n_rows, D), F32),
        compiler_params=_params(("arbitrary",), 48),
    )(blk_e, blk_valid, act, wd, bd, row_w.reshape(n_rows, 1))


def _ple_kernel(hb_ref, w_ref, bg_ref, p_ref, wp_ref, h_ref, z_ref, acc_ref, *, alpha):
    k = pl.program_id(1)

    @pl.when(k == 0)
    def _():
        acc_ref[...] = jnp.zeros_like(acc_ref)

    acc_ref[...] += jnp.dot(hb_ref[...], w_ref[...], preferred_element_type=F32)

    @pl.when(k == pl.num_programs(1) - 1)
    def _():
        gate = jax.nn.sigmoid(acc_ref[...] + bg_ref[...])
        emb = jnp.dot(p_ref[...].astype(BF16), wp_ref[...], preferred_element_type=F32)
        z_ref[...] = alpha * h_ref[...] + gate * emb


def _ple_residual(hb, w_gate_b, b_gate, p2, w_ple_b, h, alpha):
    T, D = h.shape
    pd = p2.shape[1]
    tm = _pick_tile(T, (256, 128))
    tk = _pick_tile(D, (512, 256, 128))
    row = lambda i, k: (i, 0)
    const = lambda i, k: (0, 0)
    return pl.pallas_call(
        functools.partial(_ple_kernel, alpha=alpha),
        grid=(T // tm, D // tk),
        in_specs=[
            pl.BlockSpec((tm, tk), lambda i, k: (i, k)),
            pl.BlockSpec((tk, D), lambda i, k: (k, 0)),
            pl.BlockSpec((1, D), const),
            pl.BlockSpec((tm, pd), row),
            pl.BlockSpec((pd, D), const),
            pl.BlockSpec((tm, D), row),
        ],
        out_specs=pl.BlockSpec((tm, D), row),
        out_shape=jax.ShapeDtypeStruct((T, D), F32),
        scratch_shapes=[pltpu.VMEM((tm, D), F32)],
        compiler_params=_params(("parallel", "arbitrary"), 48),
    )(hb, w_gate_b, b_gate.reshape(1, D), p2, w_ple_b, h)


def _combine_kernel(pos_ref, y_hbm, z_ref, g_ref, b_ref, o_ref, buf_ref, sem, *, tc, topk):
    i = pl.program_id(0)

    def issue(t, carry):
        for k in range(topk):
            _row_copy(y_hbm, pos_ref[(i * tc + t) * topk + k], buf_ref.at[k], t, sem).start()
        return carry

    lax.fori_loop(0, tc, issue, 0)

    def drain(t, carry):
        for k in range(topk):
            _row_copy(y_hbm, 0, buf_ref.at[k], t, sem).wait()
        return carry

    lax.fori_loop(0, tc, drain, 0)
    ffn = buf_ref[0]
    for k in range(1, topk):
        ffn = ffn + buf_ref[k]
    o_ref[...] = _layer_norm(z_ref[...] + ffn, g_ref[...], b_ref[...])


def _combine_ln(pos, y, z, ln_g, ln_b):
    T, D = z.shape
    tc = _pick_tile(T, (64, 32))
    row = lambda i, pos: (i, 0)
    const = lambda i, pos: (0, 0)
    return pl.pallas_call(
        functools.partial(_combine_kernel, tc=tc, topk=TOP_K),
        grid_spec=pltpu.PrefetchScalarGridSpec(
            num_scalar_prefetch=1,
            grid=(T // tc,),
            in_specs=[
                pl.BlockSpec(memory_space=pl.ANY),
                pl.BlockSpec((tc, D), row),
                pl.BlockSpec((1, D), const),
                pl.BlockSpec((1, D), const),
            ],
            out_specs=pl.BlockSpec((tc, D), row),
            scratch_shapes=[pltpu.VMEM((TOP_K, tc, D), F32), pltpu.SemaphoreType.DMA(())],
        ),
        out_shape=jax.ShapeDtypeStruct((T, D), F32),
        compiler_params=_params(("arbitrary",), 32),
    )(pos, y, z, ln_g.reshape(1, D), ln_b.reshape(1, D))


def _routing_tables(logits, n_experts):
    T = logits.shape[0]
    A = T * TOP_K
    rb = EXPERT_ROW_BLOCK
    top_logit, top_e = lax.top_k(logits[:, :n_experts], TOP_K)
    top_w = jax.nn.softmax(top_logit, axis=-1)
    flat_e = top_e.reshape(-1)
    order = jnp.argsort(flat_e)
    sorted_e = flat_e[order]
    counts = jnp.bincount(flat_e, length=n_experts)
    padded = (counts + rb - 1) // rb * rb
    pad_end = jnp.cumsum(padded)
    pad_start = pad_end - padded
    grp_start = jnp.cumsum(counts) - counts
    dest = (pad_start[sorted_e] + jnp.arange(A) - grp_start[sorted_e]).astype(jnp.int32)
    n_blocks = -(-A // rb) + n_experts
    n_rows = n_blocks * rb
    row_tok = jnp.zeros((n_rows,), jnp.int32).at[dest].set((order // TOP_K).astype(jnp.int32))
    row_w = jnp.zeros((n_rows,), F32).at[dest].set(top_w.reshape(-1)[order])
    blk_start = jnp.arange(n_blocks) * rb
    blk_e = jnp.minimum(jnp.searchsorted(pad_end, blk_start, side="right"), n_experts - 1).astype(jnp.int32)
    blk_valid = (blk_start < pad_end[-1]).astype(jnp.int32)
    pos = jnp.zeros((A,), jnp.int32).at[order].set(dest)
    return row_tok, row_w, blk_e, blk_valid, pos


def _layer(x2, p2, B, S, w_in, w_pool, pool_scale, cmp_k_pe, cmp_k_w1, cmp_k_w2, cmp_v_pe, cmp_v_w1, cmp_v_w2,
           w_out, ln1_g, ln1_b, w_router, b_router, w_up, b_up, w_down, b_down,
           w_ple_gate, b_ple_gate, w_ple, ln2_g, ln2_b, alpha):
    T, D = x2.shape
    pool_w = D // 2
    nsa_w = D - pool_w
    nh = nsa_w // HEAD_DIM
    in_w = w_in.shape[1]
    kvw = (in_w - D - N_GATES * nh) // N_KV_GROUPS_FIELDS
    G = kvw // HEAD_DIM
    hg = nh // G
    n_experts = w_router.shape[1]

    xb = x2.astype(BF16)
    w_in_b = w_in.astype(BF16)
    proj_a = _matmul_cols(xb, w_in_b, 0, D)
    proj_b = _matmul_cols(xb, w_in_b, D, 2 * kvw)
    proj_c = _matmul_cols(xb, w_in_b, D + 2 * kvw, 4 * kvw)
    w_gates = jnp.pad(w_in_b[:, D + 6 * kvw:], ((0, 0), (0, LANES - N_GATES * nh)))
    gates = _matmul_cols(xb, w_gates, 0, LANES)

    y_pool = _pool_mixer(proj_a, w_pool.astype(BF16), pool_scale, B, S, pool_w)

    cos2, sin2 = _rope_tables(jnp.arange(S))
    nc = S // CMP_STRIDE
    ccos, csin = _rope_tables(jnp.arange(nc) * CMP_STRIDE + CMP_BLOCK - 1)
    kvc3 = proj_b.reshape(B, nc, CMP_STRIDE * 2 * kvw)
    kc = _compress(kvc3, 0, cmp_k_pe, cmp_k_w1, cmp_k_w2, ccos, csin, B, G, rope=True)
    vc = _compress(kvc3, 1, cmp_v_pe, cmp_v_w1, cmp_v_w2, ccos, csin, B, G, rope=False)
    q_rot, k_slc, v_slc, k_win, v_win = _rope_prep(proj_a, proj_c, cos2, sin2, B, S, nsa_w, G)
    o_cmp, sel_bias = _cmp_select(q_rot, kc, vc, B, S, G, hg)
    tq = _pick_tile(S, (256, 128))
    o_slc = _flash(q_rot, k_slc, v_slc, sel_bias, B, S, G, hg, tq, _pick_tile(S, (512, 256)), None)
    o_win = _flash(q_rot, k_win, v_win, None, B, S, G, hg, tq, tq, WINDOW)
    y_nsa = _gate_combine(gates, o_cmp, o_slc, o_win, B, S, nh)

    ne_pad = -(-n_experts // LANES) * LANES
    w_router_p = jnp.pad(w_router, ((0, 0), (0, ne_pad - n_experts)))
    b_router_p = jnp.pad(b_router, (0, ne_pad - n_experts)).reshape(1, ne_pad)
    h, hb, logits = _mix_ln_router(y_pool, y_nsa, w_out.astype(BF16), x2, ln1_g, ln1_b,
                                   w_router_p, b_router_p, alpha)

    row_tok, row_w, blk_e, blk_valid, pos = _routing_tables(logits, n_experts)
    F = w_down.shape[1]
    wg = w_up[:, :, 0::2].astype(BF16)
    wu = w_up[:, :, 1::2].astype(BF16)
    bg = b_up[:, 0::2].reshape(n_experts, 1, F)
    bu = b_up[:, 1::2].reshape(n_experts, 1, F)
    act = _moe_up(row_tok, blk_e, blk_valid, h, wg, wu, bg, bu)
    y = _moe_down(blk_e, blk_valid, act, w_down.astype(BF16), b_down.reshape(n_experts, 1, D), row_w)

    z = _ple_residual(hb, w_ple_gate.astype(BF16), b_ple_gate, p2, w_ple.astype(BF16), h, alpha)
    return _combine_ln(pos, y, z, ln2_g, ln2_b)


def kernel(x, p, w_in, w_pool, pool_scale, cmp_k_pe, cmp_k_w1, cmp_k_w2, cmp_v_pe, cmp_v_w1, cmp_v_w2, w_out, ln1_g, ln1_b, w_router, b_router, w_up, b_up, w_down, b_down, w_ple_gate, b_ple_gate, w_ple, ln2_g, ln2_b):
    B, S, D = x.shape
    depth = w_in.shape[0]
    alpha = (2 * depth) ** 0.25
    x2 = x.reshape(B * S, D)
    for i in range(depth):
        x2 = _layer(x2, p[i].reshape(B * S, -1), B, S, w_in[i], w_pool[i], pool_scale[i],
                    cmp_k_pe[i], cmp_k_w1[i], cmp_k_w2[i], cmp_v_pe[i], cmp_v_w1[i], cmp_v_w2[i],
                    w_out[i], ln1_g[i], ln1_b[i], w_router[i], b_router[i], w_up[i], b_up[i],
                    w_down[i], b_down[i], w_ple_gate[i], b_ple_gate[i], w_ple[i], ln2_g[i], ln2_b[i], alpha)
    return x2.reshape(B, S, D)
```

```python
import functools
import math

import jax
import jax.numpy as jnp
import numpy as np
from jax import lax
from jax.experimental import pallas as pl
from jax.experimental.pallas import tpu as pltpu

HEAD_DIM = 128
LANES = 128
N_KV_GROUPS_FIELDS = 6
N_GATES = 3
POOL_WINDOWS = (2, 4, 8, 16)
CMP_BLOCK = 32
CMP_STRIDE = 16
SEL_BLOCK = 64
N_SEL = 16
WINDOW = 512
ROPE_THETA = 10000.0
TOP_K = 4
SWIGLU_LIMIT = 7.0
SWIGLU_ALPHA = 1.702
EXPERT_ROW_BLOCK = 256
LN_EPS = 1e-5
NEG_INF = -1e30
FORCE_SCORE = 1e4
LOG2E = 1.4426950408889634
EPI_ROWS = 128
MIB = 1024 * 1024

BF16 = jnp.bfloat16
F32 = jnp.float32
NT_DIMS = (((1,), (1,)), ((), ()))


def _params(sem, vmem_mib=None):
    kw = dict(dimension_semantics=sem)
    if vmem_mib is not None:
        kw["vmem_limit_bytes"] = vmem_mib * MIB
    return pltpu.CompilerParams(**kw)


def _pick_tile(n, candidates):
    for c in candidates:
        if n % c == 0:
            return c
    raise ValueError(f"no tile for {n}")


def _mm_kernel(x_ref, w_ref, o_ref):
    @pl.when(pl.program_id(2) == 0)
    def _():
        o_ref[...] = jnp.zeros_like(o_ref)

    o_ref[...] += jnp.dot(x_ref[...], w_ref[...], preferred_element_type=F32)


def _matmul_cols(x, w, col0, ncols):
    M, K = x.shape
    tm = _pick_tile(M, (1024, 512, 256))
    tk = _pick_tile(K, (2048, 1024, 512, 256, 128))
    tn = _pick_tile(math.gcd(col0, ncols) if col0 else ncols, (1024, 512, 256, 128))
    off = col0 // tn
    return pl.pallas_call(
        _mm_kernel,
        name="in_proj",
        grid=(M // tm, ncols // tn, K // tk),
        in_specs=[pl.BlockSpec((tm, tk), lambda i, j, k: (i, k)),
                  pl.BlockSpec((tk, tn), lambda i, j, k: (k, j + off))],
        out_specs=pl.BlockSpec((tm, tn), lambda i, j, k: (i, j)),
        out_shape=jax.ShapeDtypeStruct((M, ncols), F32),
        compiler_params=_params(("parallel", "parallel", "arbitrary"), 40),
    )(x, w)


def _pool_kernel(u_ref, prev_ref, w_ref, sc_ref, o_ref, ext_ref, *, ts, halo):
    g = pl.program_id(0)
    i = pl.program_id(2)
    cur = u_ref[...]
    ext_ref[pl.ds(halo, ts), :] = cur

    @pl.when(i == 0)
    def _():
        ext_ref[pl.ds(0, halo), :] = jnp.zeros((halo, cur.shape[1]), F32)

    @pl.when(i > 0)
    def _():
        ext_ref[pl.ds(0, halo), :] = prev_ref[...]

    def back(d):
        return ext_ref[pl.ds(halo - d, ts), :]

    s2 = cur + back(1)
    s4 = s2 + back(2) + back(3)
    s8 = s4 + back(4) + back(5) + back(6) + back(7)
    s16 = s8
    for d in range(8, 16):
        s16 = s16 + back(d)
    ssum = jnp.where(g == 0, s2, jnp.where(g == 1, s4, jnp.where(g == 2, s8, s16)))
    win = jnp.left_shift(2, g)
    t = i * ts + lax.broadcasted_iota(jnp.int32, (ts, 1), 0)
    cnt = jnp.minimum(t + 1, win).astype(F32)
    pooled = ssum / cnt - cur
    mixed = jnp.dot(pooled.astype(BF16), w_ref[0], preferred_element_type=F32)
    o_ref[...] = (mixed * sc_ref[...]).astype(o_ref.dtype)


def _pool_mixer(proj_a, w_pool_b, pool_scale, B, S, pool_w):
    T = B * S
    ng = len(POOL_WINDOWS)
    pg = pool_w // ng
    halo = POOL_WINDOWS[-1]
    ts = _pick_tile(S, (512, 256, 128))
    ns = S // ts
    hb = ts // halo
    kern = functools.partial(_pool_kernel, ts=ts, halo=halo)
    return pl.pallas_call(
        kern,
        name="pool_mixer",
        grid=(ng, B, ns),
        in_specs=[
            pl.BlockSpec((ts, pg), lambda g, b, i: (b * ns + i, g)),
            pl.BlockSpec((halo, pg), lambda g, b, i: (jnp.maximum((b * ns + i) * hb - 1, 0), g)),
            pl.BlockSpec((1, pg, pg), lambda g, b, i: (g, 0, 0)),
            pl.BlockSpec((1, pg), lambda g, b, i: (0, g)),
        ],
        out_specs=pl.BlockSpec((ts, pg), lambda g, b, i: (b * ns + i, g)),
        out_shape=jax.ShapeDtypeStruct((T, pool_w), BF16),
        scratch_shapes=[pltpu.VMEM((ts + halo, pg), F32)],
        compiler_params=_params(("parallel", "parallel", "arbitrary"), 32),
    )(proj_a, proj_a, w_pool_b, pool_scale.reshape(1, pool_w))


def _rope_tables(pos):
    inv = ROPE_THETA ** (-jnp.arange(0, HEAD_DIM, 2, dtype=F32) / HEAD_DIM)
    ang = pos.astype(F32)[:, None] * inv[None, :]
    c, s = jnp.cos(ang), jnp.sin(ang)
    return jnp.concatenate([c, c], axis=-1), jnp.concatenate([-s, s], axis=-1)


def _rope(x, cos2, sin2):
    return x * cos2 + pltpu.roll(x, HEAD_DIM // 2, 1) * sin2


def _compress_kernel(x_ref, pe_ref, w1_ref, w2_ref, cos_ref, sin_ref, o_ref, a_ref, b_ref, *, rope, nl):
    l = pl.program_id(2)

    @pl.when(l == 0)
    def _():
        a_ref[...] = jnp.zeros_like(a_ref)
        b_ref[...] = jnp.zeros_like(b_ref)

    x = x_ref[0]
    xa = (x + pe_ref[pl.ds(l, 1), :]).astype(BF16)
    xb = (x + pe_ref[pl.ds(l + nl, 1), :]).astype(BF16)
    a_ref[...] += jnp.dot(xa, w1_ref[l].astype(BF16), preferred_element_type=F32)
    b_ref[...] += jnp.dot(xb, w1_ref[l + nl].astype(BF16), preferred_element_type=F32)

    @pl.when(l == nl - 1)
    def _():
        nc = a_ref.shape[0]
        hid_pre = a_ref[...] + pltpu.roll(b_ref[...], nc - 1, 0)
        hid = jax.nn.gelu(hid_pre)
        out = jnp.dot(hid.astype(BF16), w2_ref[...].astype(BF16), preferred_element_type=F32)
        if rope:
            out = _rope(out, cos_ref[...], sin_ref[...])
        o_ref[0, 0] = out.astype(o_ref.dtype)


def _compress(kvc3, field, pe, w1, w2, cos2, sin2, B, G, rope):
    nc = kvc3.shape[1]
    nl = CMP_STRIDE
    cols_per_tok = kvc3.shape[2] // nl // LANES
    kern = functools.partial(_compress_kernel, rope=rope, nl=nl)
    return pl.pallas_call(
        kern,
        name="compress_k" if rope else "compress_v",
        grid=(B, G, nl),
        in_specs=[
            pl.BlockSpec((1, nc, LANES), lambda b, g, l: (b, 0, l * cols_per_tok + field * G + g)),
            pl.BlockSpec((CMP_BLOCK, HEAD_DIM), lambda b, g, l: (0, 0)),
            pl.BlockSpec((CMP_BLOCK, HEAD_DIM, HEAD_DIM), lambda b, g, l: (0, 0, 0)),
            pl.BlockSpec((HEAD_DIM, HEAD_DIM), lambda b, g, l: (0, 0)),
            pl.BlockSpec((nc, HEAD_DIM), lambda b, g, l: (0, 0)),
            pl.BlockSpec((nc, HEAD_DIM), lambda b, g, l: (0, 0)),
        ],
        out_specs=pl.BlockSpec((1, 1, nc, HEAD_DIM), lambda b, g, l: (b, g, 0, 0)),
        out_shape=jax.ShapeDtypeStruct((B, G, nc, HEAD_DIM), BF16),
        scratch_shapes=[pltpu.VMEM((nc, HEAD_DIM), F32), pltpu.VMEM((nc, HEAD_DIM), F32)],
        compiler_params=_params(("parallel", "parallel", "arbitrary"), 32),
    )(kvc3, pe, w1, w2, cos2, sin2)


def _rope_prep_kernel(q_ref, kv_ref, cos_ref, sin_ref, qo_ref, ks_ref, vs_ref, kw_ref, vw_ref, *, nh, G, ts):
    i = pl.program_id(1)
    c = cos_ref[...]
    s = sin_ref[...]
    kvw = G * HEAD_DIM
    for h in range(nh):
        qo_ref[0, h] = _rope(q_ref[:, h * HEAD_DIM:(h + 1) * HEAD_DIM], c, s).astype(BF16)
    blk = (i * ts + lax.broadcasted_iota(jnp.int32, (ts, LANES), 0)) // SEL_BLOCK
    onehot = (blk == lax.broadcasted_iota(jnp.int32, (ts, LANES), 1)).astype(BF16)
    for g in range(G):
        lo = g * HEAD_DIM
        ks_ref[0, g, :, 0:HEAD_DIM] = _rope(kv_ref[:, lo:lo + HEAD_DIM], c, s).astype(BF16)
        ks_ref[0, g, :, HEAD_DIM:2 * HEAD_DIM] = onehot
        vs_ref[0, g] = kv_ref[:, kvw + lo:kvw + lo + HEAD_DIM].astype(BF16)
        kw_ref[0, g] = _rope(kv_ref[:, 2 * kvw + lo:2 * kvw + lo + HEAD_DIM], c, s).astype(BF16)
        vw_ref[0, g] = kv_ref[:, 3 * kvw + lo:3 * kvw + lo + HEAD_DIM].astype(BF16)


def _rope_prep(proj_a, proj_c, cos2, sin2, B, S, nsa_w, G):
    nh = nsa_w // HEAD_DIM
    ts = _pick_tile(S, (256, 128))
    ns = S // ts
    kern = functools.partial(_rope_prep_kernel, nh=nh, G=G, ts=ts)
    hm = lambda b, i: (b, 0, i, 0)
    return pl.pallas_call(
        kern,
        name="rope_prep",
        grid=(B, ns),
        in_specs=[
            pl.BlockSpec((ts, nsa_w), lambda b, i: (b * ns + i, 1)),
            pl.BlockSpec((ts, 4 * G * HEAD_DIM), lambda b, i: (b * ns + i, 0)),
            pl.BlockSpec((ts, HEAD_DIM), lambda b, i: (i, 0)),
            pl.BlockSpec((ts, HEAD_DIM), lambda b, i: (i, 0)),
        ],
        out_specs=[
            pl.BlockSpec((1, nh, ts, HEAD_DIM), hm),
            pl.BlockSpec((1, G, ts, 2 * HEAD_DIM), hm),
            pl.BlockSpec((1, G, ts, HEAD_DIM), hm),
            pl.BlockSpec((1, G, ts, HEAD_DIM), hm),
            pl.BlockSpec((1, G, ts, HEAD_DIM), hm),
        ],
        out_shape=[
            jax.ShapeDtypeStruct((B, nh, S, HEAD_DIM), BF16),
            jax.ShapeDtypeStruct((B, G, S, 2 * HEAD_DIM), BF16),
            jax.ShapeDtypeStruct((B, G, S, HEAD_DIM), BF16),
            jax.ShapeDtypeStruct((B, G, S, HEAD_DIM), BF16),
            jax.ShapeDtypeStruct((B, G, S, HEAD_DIM), BF16),
        ],
        compiler_params=_params(("parallel", "parallel"), 40),
    )(proj_a, proj_c, cos2, sin2)


def _cmp_select_kernel(q_ref, kc_ref, vc_ref, ovl_ref, o_ref, bias_ref, *, hg, tq, n_sel, n_keep, scale):
    i = pl.program_id(2)
    nc = kc_ref.shape[2]
    kc = kc_ref[0, 0]
    vc = vc_ref[0, 0]
    t = i * tq + lax.broadcasted_iota(jnp.int32, (tq, nc), 0)
    n = lax.broadcasted_iota(jnp.int32, (tq, nc), 1)
    valid = (n * CMP_STRIDE + (CMP_BLOCK - 1) <= t) & (n < nc - 1)
    ovl = ovl_ref[...]
    imp = jnp.zeros((n_sel, tq), F32)
    for h in range(hg):
        s = lax.dot_general(q_ref[0, h], kc, NT_DIMS, preferred_element_type=F32) * scale
        s = jnp.where(valid, s, NEG_INF)
        m = jnp.max(s, axis=-1, keepdims=True)
        e = jnp.where(valid, jnp.exp(s - m), 0.0)
        den = jnp.sum(e, axis=-1, keepdims=True)
        pc = (e / jnp.where(den > 0.0, den, 1.0)).astype(BF16)
        o_ref[0, h] = jnp.dot(pc, vc, preferred_element_type=F32)
        imp = imp + lax.dot_general(ovl, pc, NT_DIMS, preferred_element_type=F32)
    j = lax.broadcasted_iota(jnp.int32, (n_sel, tq), 0)
    tb = (i * tq + lax.broadcasted_iota(jnp.int32, (n_sel, tq), 1)) // SEL_BLOCK
    forced = (j == 0) | (j == tb) | (j == tb - 1)
    score = jnp.where(forced, FORCE_SCORE, jnp.where(j <= tb, imp, -1.0))
    rank = jnp.zeros((n_sel, tq), jnp.int32)
    for jp in range(n_sel):
        row = score[jp:jp + 1, :]
        ahead = (row > score) | ((row == score) & (jp < j))
        rank = rank + ahead.astype(jnp.int32)
    bias_t = jnp.where(rank < n_keep, 0.0, NEG_INF)
    if n_sel < LANES:
        bias_t = jnp.concatenate([bias_t, jnp.zeros((LANES - n_sel, tq), F32)], axis=0)
    bias_ref[0, 0] = bias_t.T.astype(BF16)


def _overlap_t(n_cmp_pad, n_sel):
    n_cmp = n_cmp_pad - 1
    cs = np.arange(n_cmp) * CMP_STRIDE
    ce = cs + CMP_BLOCK
    ss = np.arange(n_sel) * SEL_BLOCK
    se = ss + SEL_BLOCK
    ov = np.clip(np.minimum(ce[:, None], se[None, :]) - np.maximum(cs[:, None], ss[None, :]), 0, None)
    out = np.zeros((n_sel, n_cmp_pad), np.float32)
    out[:, :n_cmp] = (ov / CMP_BLOCK).T
    return jnp.asarray(out, dtype=BF16)


def _cmp_select(q_rot, kc, vc, B, S, G, hg):
    nc = kc.shape[2]
    n_sel = S // SEL_BLOCK
    assert n_sel <= LANES and n_sel % 8 == 0
    tq = _pick_tile(S, (256, 128))
    nq = S // tq
    nh = G * hg
    kern = functools.partial(_cmp_select_kernel, hg=hg, tq=tq, n_sel=n_sel,
                             n_keep=min(N_SEL, n_sel), scale=HEAD_DIM ** -0.5)
    return pl.pallas_call(
        kern,
        name="cmp_attn_select",
        grid=(B, G, nq),
        in_specs=[
            pl.BlockSpec((1, hg, tq, HEAD_DIM), lambda b, g, i: (b, g, i, 0)),
            pl.BlockSpec((1, 1, nc, HEAD_DIM), lambda b, g, i: (b, g, 0, 0)),
            pl.BlockSpec((1, 1, nc, HEAD_DIM), lambda b, g, i: (b, g, 0, 0)),
            pl.BlockSpec((n_sel, nc), lambda b, g, i: (0, 0)),
        ],
        out_specs=[
            pl.BlockSpec((1, hg, tq, HEAD_DIM), lambda b, g, i: (b, g, i, 0)),
            pl.BlockSpec((1, 1, tq, LANES), lambda b, g, i: (b, g, i, 0)),
        ],
        out_shape=[
            jax.ShapeDtypeStruct((B, nh, S, HEAD_DIM), F32),
            jax.ShapeDtypeStruct((B, G, S, LANES), BF16),
        ],
        compiler_params=_params(("parallel", "parallel", "parallel"), 40),
    )(q_rot, kc, vc, _overlap_t(nc, n_sel))


def _lane_fold(x, op):
    r = x[:, :LANES]
    for a in range(1, x.shape[1] // LANES):
        r = op(r, x[:, a * LANES:(a + 1) * LANES])
    return r


def _slc_kernel(q_ref, bias_ref, k_ref, v_ref, o_ref, s_ref, m_ref, l_ref, acc_ref, *, hg, tq, scale):
    i = pl.program_id(2)
    rows = hg * tq
    c = scale * LOG2E
    q = q_ref[0].reshape(rows, HEAD_DIM)
    b = bias_ref[0, 0]
    qa = jnp.concatenate([q, jnp.concatenate([b] * hg, axis=0)], axis=1)

    def k_tile(j):
        return k_ref[0, 0, pl.ds(pl.multiple_of(j * tq, tq), tq), :]

    def v_tile(j):
        return v_ref[0, 0, pl.ds(pl.multiple_of(j * tq, tq), tq), :]

    m_ref[...] = jnp.full_like(m_ref, NEG_INF)

    def scores(j, carry):
        s = lax.dot_general(qa, k_tile(j), NT_DIMS, preferred_element_type=F32)
        s_ref[j] = s
        m_ref[...] = jnp.maximum(m_ref[...], _lane_fold(s, jnp.maximum))
        return carry

    lax.fori_loop(0, i, scores, 0)
    sd = lax.dot_general(qa, k_tile(i), NT_DIMS, preferred_element_type=F32)
    tpos = lax.broadcasted_iota(jnp.int32, (rows, tq), 0) & (tq - 1)
    kpos = lax.broadcasted_iota(jnp.int32, (rows, tq), 1)
    sd = jnp.where(kpos <= tpos, sd, NEG_INF)
    mrow = jnp.max(jnp.maximum(m_ref[...], _lane_fold(sd, jnp.maximum)), axis=-1, keepdims=True) * c
    m_ref[...] = jnp.broadcast_to(mrow, (rows, LANES))
    l_ref[...] = jnp.zeros_like(l_ref)
    acc_ref[...] = jnp.zeros_like(acc_ref)

    def probs(s):
        mfull = jnp.concatenate([m_ref[...]] * (tq // LANES), axis=1)
        return jnp.exp2(s * c - mfull)

    def accumulate(j, carry):
        p = probs(s_ref[j])
        l_ref[...] += _lane_fold(p, jnp.add)
        acc_ref[...] += jnp.dot(p.astype(BF16), v_tile(j), preferred_element_type=F32)
        return carry

    lax.fori_loop(0, i, accumulate, 0)
    p = probs(sd)
    den = jnp.sum(l_ref[...] + _lane_fold(p, jnp.add), axis=-1, keepdims=True)
    acc = acc_ref[...] + jnp.dot(p.astype(BF16), v_tile(i), preferred_element_type=F32)
    o_ref[0] = (acc / den).reshape(hg, tq, HEAD_DIM)


def _selected_attention(q_rot, k_aug, v, bias, B, S, G, hg, tq):
    assert tq & (tq - 1) == 0 and tq % LANES == 0
    nq = S // tq
    rows = hg * tq
    kern = functools.partial(_slc_kernel, hg=hg, tq=tq, scale=HEAD_DIM ** -0.5)
    return pl.pallas_call(
        kern,
        name="selected_attn",
        grid=(B, G, nq),
        in_specs=[
            pl.BlockSpec((1, hg, tq, HEAD_DIM), lambda b, g, i: (b, g, i, 0)),
            pl.BlockSpec((1, 1, tq, LANES), lambda b, g, i: (b, g, i, 0)),
            pl.BlockSpec((1, 1, S, 2 * HEAD_DIM), lambda b, g, i: (b, g, 0, 0)),
            pl.BlockSpec((1, 1, S, HEAD_DIM), lambda b, g, i: (b, g, 0, 0)),
        ],
        out_specs=pl.BlockSpec((1, hg, tq, HEAD_DIM), lambda b, g, i: (b, g, i, 0)),
        out_shape=jax.ShapeDtypeStruct((B, G * hg, S, HEAD_DIM), F32),
        scratch_shapes=[pltpu.VMEM((nq, rows, tq), F32), pltpu.VMEM((rows, LANES), F32),
                        pltpu.VMEM((rows, LANES), F32), pltpu.VMEM((rows, HEAD_DIM), F32)],
        compiler_params=_params(("parallel", "parallel", "arbitrary"), 48),
    )(q_rot, bias, k_aug, v)


def _win_kernel(q_ref, *refs, hg, tq, nprev, window, scale):
    nt = nprev + 1
    k_refs, v_refs, o_ref = refs[:nt], refs[nt:2 * nt], refs[2 * nt]
    i = pl.program_id(2)
    c = scale * LOG2E
    nk = nt * tq
    kcat = jnp.concatenate([r[0, 0] for r in k_refs], axis=0)
    vcat = jnp.concatenate([r[0, 0] for r in v_refs], axis=0)
    tpos = i * tq + lax.broadcasted_iota(jnp.int32, (tq, nk), 0)
    kpos = (i - nprev) * tq + lax.broadcasted_iota(jnp.int32, (tq, nk), 1)
    mask = (kpos <= tpos) & (kpos > tpos - window) & (kpos >= 0)
    for h in range(hg):
        s = lax.dot_general(q_ref[0, h], kcat, NT_DIMS, preferred_element_type=F32)
        s = jnp.where(mask, s, NEG_INF)
        m = jnp.max(s, axis=-1, keepdims=True)
        p = jnp.exp2((s - m) * c)
        den = jnp.sum(p, axis=-1, keepdims=True)
        o_ref[0, h] = jnp.dot(p.astype(BF16), vcat, preferred_element_type=F32) / den


def _window_attention(q_rot, k, v, B, S, G, hg, tq, window):
    assert window % tq == 0
    nprev = window // tq
    nq = S // tq
    kern = functools.partial(_win_kernel, hg=hg, tq=tq, nprev=nprev, window=window, scale=HEAD_DIM ** -0.5)

    def kv_spec(a):
        return pl.BlockSpec((1, 1, tq, HEAD_DIM), lambda b, g, i: (b, g, jnp.maximum(i - nprev + a, 0), 0))

    kv_specs = [kv_spec(a) for a in range(nprev + 1)]
    return pl.pallas_call(
        kern,
        name="window_attn",
        grid=(B, G, nq),
        in_specs=[pl.BlockSpec((1, hg, tq, HEAD_DIM), lambda b, g, i: (b, g, i, 0))] + kv_specs + kv_specs,
        out_specs=pl.BlockSpec((1, hg, tq, HEAD_DIM), lambda b, g, i: (b, g, i, 0)),
        out_shape=jax.ShapeDtypeStruct((B, G * hg, S, HEAD_DIM), F32),
        compiler_params=_params(("parallel", "parallel", "arbitrary"), 40),
    )(q_rot, *([k] * (nprev + 1)), *([v] * (nprev + 1)))


def _gate_kernel(g_ref, oc_ref, os_ref, ow_ref, y_ref, *, nh):
    g = jax.nn.sigmoid(g_ref[...])
    for h in range(nh):
        c = N_GATES * h
        y = g[:, c:c + 1] * oc_ref[0, h] + g[:, c + 1:c + 2] * os_ref[0, h] + g[:, c + 2:c + 3] * ow_ref[0, h]
        y_ref[:, h * HEAD_DIM:(h + 1) * HEAD_DIM] = y.astype(y_ref.dtype)


def _gate_combine(gates, o_cmp, o_slc, o_win, B, S, nh):
    ts = _pick_tile(S, (256, 128))
    ns = S // ts
    hm = pl.BlockSpec((1, nh, ts, HEAD_DIM), lambda b, i: (b, 0, i, 0))
    return pl.pallas_call(
        functools.partial(_gate_kernel, nh=nh),
        name="gate_combine",
        grid=(B, ns),
        in_specs=[pl.BlockSpec((ts, LANES), lambda b, i: (b * ns + i, 0)), hm, hm, hm],
        out_specs=pl.BlockSpec((ts, nh * HEAD_DIM), lambda b, i: (b * ns + i, 0)),
        out_shape=jax.ShapeDtypeStruct((B * S, nh * HEAD_DIM), BF16),
        compiler_params=_params(("parallel", "parallel"), 40),
    )(gates, o_cmp, o_slc, o_win)


def _layer_norm(z, g, b):
    mu = jnp.mean(z, axis=-1, keepdims=True)
    zc = z - mu
    var = jnp.mean(zc * zc, axis=-1, keepdims=True)
    return zc * lax.rsqrt(var + LN_EPS) * g + b


def _mix_kernel(yp_ref, yn_ref, w_ref, x_ref, g_ref, b_ref, wr_ref, br_ref, h_ref, hb_ref, lg_ref, acc_ref,
                *, nkp, alpha):
    k = pl.program_id(1)

    @pl.when(k == 0)
    def _():
        acc_ref[...] = jnp.zeros_like(acc_ref)

    @pl.when(k < nkp)
    def _():
        acc_ref[...] += jnp.dot(yp_ref[...], w_ref[...], preferred_element_type=F32)

    @pl.when(k >= nkp)
    def _():
        acc_ref[...] += jnp.dot(yn_ref[...], w_ref[...], preferred_element_type=F32)

    @pl.when(k == pl.num_programs(1) - 1)
    def _():
        def epilogue(r, carry):
            rows = pl.ds(pl.multiple_of(r * EPI_ROWS, EPI_ROWS), EPI_ROWS)
            h = _layer_norm(alpha * x_ref[rows, :] + acc_ref[rows, :], g_ref[...], b_ref[...])
            hb = h.astype(BF16)
            h_ref[rows, :] = h
            hb_ref[rows, :] = hb
            lg_ref[rows, :] = jnp.dot(hb, wr_ref[...], preferred_element_type=F32) + br_ref[...]
            return carry

        lax.fori_loop(0, acc_ref.shape[0] // EPI_ROWS, epilogue, 0)


def _mix_ln_router(y_pool, y_nsa, w_out_b, x2, ln_g, ln_b, w_router_p, b_router_p, alpha):
    T, D = x2.shape
    kp = y_pool.shape[1]
    tm = _pick_tile(T, (512, 256, 128))
    tk = _pick_tile(math.gcd(kp, y_nsa.shape[1]), (512, 256, 128))
    nkp = kp // tk
    nk = nkp + y_nsa.shape[1] // tk
    ne = w_router_p.shape[1]
    kern = functools.partial(_mix_kernel, nkp=nkp, alpha=alpha)
    row = lambda i, k: (i, 0)
    const = lambda i, k: (0, 0)
    return pl.pallas_call(
        kern,
        name="out_proj_ln1_router",
        grid=(T // tm, nk),
        in_specs=[
            pl.BlockSpec((tm, tk), lambda i, k: (i, jnp.minimum(k, nkp - 1))),
            pl.BlockSpec((tm, tk), lambda i, k: (i, jnp.maximum(k - nkp, 0))),
            pl.BlockSpec((tk, D), lambda i, k: (k, 0)),
            pl.BlockSpec((tm, D), row, pipeline_mode=pl.Buffered(1)),
            pl.BlockSpec((1, D), const),
            pl.BlockSpec((1, D), const),
            pl.BlockSpec((D, ne), const),
            pl.BlockSpec((1, ne), const),
        ],
        out_specs=[pl.BlockSpec((tm, D), row), pl.BlockSpec((tm, D), row), pl.BlockSpec((tm, ne), row)],
        out_shape=[jax.ShapeDtypeStruct((T, D), F32), jax.ShapeDtypeStruct((T, D), BF16),
                   jax.ShapeDtypeStruct((T, ne), F32)],
        scratch_shapes=[pltpu.VMEM((tm, D), F32)],
        compiler_params=_params(("parallel", "arbitrary"), 56),
    )(y_pool, y_nsa, w_out_b, x2, ln_g.reshape(1, D), ln_b.reshape(1, D), w_router_p, b_router_p)


def _row_copy(src_hbm, row, dst_ref, slot, sem):
    return pltpu.make_async_copy(src_hbm.at[pl.ds(row, 1)], dst_ref.at[pl.ds(slot, 1)], sem)


def _rows_copy(src_hbm, dst_ref, sem, n):
    return pltpu.make_async_copy(src_hbm.at[pl.ds(0, n)], dst_ref, sem)


PAIR = 2 * LANES


def _pair_split_kernel(w_ref, perm_ref, o_ref):
    perm = perm_ref[...]
    for c in range(w_ref.shape[2] // PAIR):
        sl = slice(c * PAIR, (c + 1) * PAIR)
        o_ref[0, :, sl] = jnp.dot(w_ref[0, :, sl].astype(BF16), perm, preferred_element_type=F32).astype(BF16)


def _pair_split_cast(w_up):
    E, D, F2 = w_up.shape
    assert F2 % PAIR == 0
    td = _pick_tile(D, (512, 256, 128))
    perm = np.zeros((PAIR, PAIR), np.float32)
    perm[2 * np.arange(LANES), np.arange(LANES)] = 1.0
    perm[2 * np.arange(LANES) + 1, LANES + np.arange(LANES)] = 1.0
    return pl.pallas_call(
        _pair_split_kernel,
        name="w_up_pair_split",
        grid=(E, D // td),
        in_specs=[pl.BlockSpec((1, td, F2), lambda e, i: (e, i, 0)),
                  pl.BlockSpec((PAIR, PAIR), lambda e, i: (0, 0))],
        out_specs=pl.BlockSpec((1, td, F2), lambda e, i: (e, i, 0)),
        out_shape=jax.ShapeDtypeStruct((E, D, F2), BF16),
        compiler_params=_params(("parallel", "parallel"), 32),
    )(w_up, jnp.asarray(perm, dtype=BF16))


def _moe_up_kernel(tok_ref, be_ref, bv_ref, h_hbm, w_ref, b_ref, act_ref, xbuf_ref, sem, *, rb):
    i = pl.program_id(0)
    nb = pl.num_programs(0)

    def gather(blk, slot):
        def issue(r, carry):
            _row_copy(h_hbm, tok_ref[blk * rb + r], xbuf_ref.at[slot], r, sem.at[slot]).start()
            return carry

        lax.fori_loop(0, rb, issue, 0)

    @pl.when((i == 0) & (bv_ref[0] != 0))
    def _():
        gather(0, 0)

    nxt = jnp.minimum(i + 1, nb - 1)

    @pl.when((i + 1 < nb) & (bv_ref[nxt] != 0))
    def _():
        gather(nxt, nxt % 2)

    @pl.when(bv_ref[i] == 0)
    def _():
        act_ref[...] = jnp.zeros_like(act_ref)

    @pl.when(bv_ref[i] != 0)
    def _():
        slot = i % 2
        _rows_copy(h_hbm, xbuf_ref.at[slot], sem.at[slot], rb).wait()
        xb = xbuf_ref[slot].astype(BF16)
        hcat = jnp.dot(xb, w_ref[0], preferred_element_type=F32) + b_ref[0]
        for c in range(hcat.shape[1] // PAIR):
            gate = jnp.minimum(hcat[:, c * PAIR:c * PAIR + LANES], SWIGLU_LIMIT)
            up = jnp.clip(hcat[:, c * PAIR + LANES:(c + 1) * PAIR], -SWIGLU_LIMIT, SWIGLU_LIMIT)
            act = gate * jax.nn.sigmoid(SWIGLU_ALPHA * gate) * (up + 1.0)
            act_ref[:, c * LANES:(c + 1) * LANES] = act.astype(act_ref.dtype)


def _moe_up(row_tok, blk_e, blk_valid, h, w_pairs, b_pairs):
    n_rows = row_tok.shape[0]
    rb = EXPERT_ROW_BLOCK
    E, D, F2 = w_pairs.shape
    wmap = lambda i, tok, be, bv: (be[i], 0, 0)
    return pl.pallas_call(
        functools.partial(_moe_up_kernel, rb=rb),
        name="moe_gather_up_swiglu",
        grid_spec=pltpu.PrefetchScalarGridSpec(
            num_scalar_prefetch=3,
            grid=(n_rows // rb,),
            in_specs=[
                pl.BlockSpec(memory_space=pl.ANY),
                pl.BlockSpec((1, D, F2), wmap),
                pl.BlockSpec((1, 1, F2), wmap),
            ],
            out_specs=pl.BlockSpec((rb, F2 // 2), lambda i, tok, be, bv: (i, 0)),
            scratch_shapes=[pltpu.VMEM((2, rb, D), F32), pltpu.SemaphoreType.DMA((2,))],
        ),
        out_shape=jax.ShapeDtypeStruct((n_rows, F2 // 2), BF16),
        compiler_params=_params(("arbitrary",), 56),
    )(row_tok, blk_e, blk_valid, h, w_pairs, b_pairs)


def _moe_down_kernel(be_ref, bv_ref, act_ref, wd_ref, bd_ref, rw_ref, y_ref):
    i = pl.program_id(0)

    @pl.when(bv_ref[i] == 0)
    def _():
        y_ref[...] = jnp.zeros_like(y_ref)

    @pl.when(bv_ref[i] != 0)
    def _():
        y = jnp.dot(act_ref[...], wd_ref[0], preferred_element_type=F32) + bd_ref[0]
        y_ref[...] = y * rw_ref[...]


def _moe_down(blk_e, blk_valid, act, wd, bd, row_w):
    n_rows, F = act.shape
    rb = EXPERT_ROW_BLOCK
    E, _, D = wd.shape
    wmap = lambda i, be, bv: (be[i], 0, 0)
    return pl.pallas_call(
        _moe_down_kernel,
        name="moe_down",
        grid_spec=pltpu.PrefetchScalarGridSpec(
            num_scalar_prefetch=2,
            grid=(n_rows // rb,),
            in_specs=[
                pl.BlockSpec((rb, F), lambda i, be, bv: (i, 0)),
                pl.BlockSpec((1, F, D), wmap),
                pl.BlockSpec((1, 1, D), wmap),
                pl.BlockSpec((rb, 1), lambda i, be, bv: (i, 0)),
            ],
            out_specs=pl.BlockSpec((rb, D), lambda i, be, bv: (i, 0)),
        ),
        out_shape=jax.ShapeDtypeStruct((n_rows, D), F32),
        compiler_params=_params(("arbitrary",), 48),
    )(blk_e, blk_valid, act, wd, bd, row_w.reshape(n_rows, 1))


def _ple_kernel(hb_ref, w_ref, bg_ref, p_ref, wp_ref, h_ref, z_ref, acc_ref, *, alpha):
    k = pl.program_id(1)

    @pl.when(k == 0)
    def _():
        acc_ref[...] = jnp.zeros_like(acc_ref)

    acc_ref[...] += jnp.dot(hb_ref[...], w_ref[...], preferred_element_type=F32)

    @pl.when(k == pl.num_programs(1) - 1)
    def _():
        def epilogue(r, carry):
            rows = pl.ds(pl.multiple_of(r * EPI_ROWS, EPI_ROWS), EPI_ROWS)
            gate = jax.nn.sigmoid(acc_ref[rows, :] + bg_ref[...])
            emb = jnp.dot(p_ref[rows, :].astype(BF16), wp_ref[...], preferred_element_type=F32)
            z_ref[rows, :] = alpha * h_ref[rows, :] + gate * emb
            return carry

        lax.fori_loop(0, acc_ref.shape[0] // EPI_ROWS, epilogue, 0)


def _ple_residual(hb, w_gate_b, b_gate, p2, w_ple_b, h, alpha):
    T, D = h.shape
    pd = p2.shape[1]
    tm = _pick_tile(T, (512, 256, 128))
    tk = _pick_tile(D, (512, 256, 128))
    row = lambda i, k: (i, 0)
    const = lambda i, k: (0, 0)
    return pl.pallas_call(
        functools.partial(_ple_kernel, alpha=alpha),
        name="ple_gate_residual",
        grid=(T // tm, D // tk),
        in_specs=[
            pl.BlockSpec((tm, tk), lambda i, k: (i, k)),
            pl.BlockSpec((tk, D), lambda i, k: (k, 0)),
            pl.BlockSpec((1, D), const),
            pl.BlockSpec((tm, pd), row),
            pl.BlockSpec((pd, D), const),
            pl.BlockSpec((tm, D), row, pipeline_mode=pl.Buffered(1)),
        ],
        out_specs=pl.BlockSpec((tm, D), row),
        out_shape=jax.ShapeDtypeStruct((T, D), F32),
        scratch_shapes=[pltpu.VMEM((tm, D), F32)],
        compiler_params=_params(("parallel", "arbitrary"), 56),
    )(hb, w_gate_b, b_gate.reshape(1, D), p2, w_ple_b, h)


def _combine_kernel(pos_ref, y_hbm, z_ref, g_ref, b_ref, o_ref, buf_ref, sem, *, tc, topk):
    i = pl.program_id(0)
    nb = pl.num_programs(0)

    def gather(blk, slot):
        def issue(t, carry):
            for k in range(topk):
                _row_copy(y_hbm, pos_ref[(blk * tc + t) * topk + k], buf_ref.at[slot, k], t, sem.at[slot]).start()
            return carry

        lax.fori_loop(0, tc, issue, 0)

    @pl.when(i == 0)
    def _():
        gather(0, 0)

    @pl.when(i + 1 < nb)
    def _():
        gather(i + 1, (i + 1) % 2)

    slot = i % 2
    for k in range(topk):
        _rows_copy(y_hbm, buf_ref.at[slot, k], sem.at[slot], tc).wait()
    ffn = buf_ref[slot, 0]
    for k in range(1, topk):
        ffn = ffn + buf_ref[slot, k]
    o_ref[...] = _layer_norm(z_ref[...] + ffn, g_ref[...], b_ref[...])


def _combine_ln(pos, y, z, ln_g, ln_b):
    T, D = z.shape
    tc = _pick_tile(T, (64, 32))
    row = lambda i, pos: (i, 0)
    const = lambda i, pos: (0, 0)
    return pl.pallas_call(
        functools.partial(_combine_kernel, tc=tc, topk=TOP_K),
        name="moe_combine_ln2",
        grid_spec=pltpu.PrefetchScalarGridSpec(
            num_scalar_prefetch=1,
            grid=(T // tc,),
            in_specs=[
                pl.BlockSpec(memory_space=pl.ANY),
                pl.BlockSpec((tc, D), row),
                pl.BlockSpec((1, D), const),
                pl.BlockSpec((1, D), const),
            ],
            out_specs=pl.BlockSpec((tc, D), row),
            scratch_shapes=[pltpu.VMEM((2, TOP_K, tc, D), F32), pltpu.SemaphoreType.DMA((2,))],
        ),
        out_shape=jax.ShapeDtypeStruct((T, D), F32),
        compiler_params=_params(("arbitrary",), 32),
    )(pos, y, z, ln_g.reshape(1, D), ln_b.reshape(1, D))


def _routing_tables(logits, n_experts):
    T = logits.shape[0]
    A = T * TOP_K
    rb = EXPERT_ROW_BLOCK
    top_logit, top_e = lax.top_k(logits[:, :n_experts], TOP_K)
    top_w = jax.nn.softmax(top_logit, axis=-1)
    flat_e = top_e.reshape(-1)
    order = jnp.argsort(flat_e)
    sorted_e = flat_e[order]
    counts = jnp.bincount(flat_e, length=n_experts)
    padded = (counts + rb - 1) // rb * rb
    pad_end = jnp.cumsum(padded)
    pad_start = pad_end - padded
    grp_start = jnp.cumsum(counts) - counts
    dest = (pad_start[sorted_e] + jnp.arange(A) - grp_start[sorted_e]).astype(jnp.int32)
    n_blocks = -(-A // rb) + n_experts
    n_rows = n_blocks * rb
    row_tok = jnp.zeros((n_rows,), jnp.int32).at[dest].set((order // TOP_K).astype(jnp.int32))
    row_w = jnp.zeros((n_rows,), F32).at[dest].set(top_w.reshape(-1)[order])
    blk_start = jnp.arange(n_blocks) * rb
    blk_e = jnp.minimum(jnp.searchsorted(pad_end, blk_start, side="right"), n_experts - 1).astype(jnp.int32)
    blk_valid = (blk_start < pad_end[-1]).astype(jnp.int32)
    pos = jnp.zeros((A,), jnp.int32).at[order].set(dest)
    return row_tok, row_w, blk_e, blk_valid, pos


def _layer(x2, p2, B, S, w_in, w_pool, pool_scale, cmp_k_pe, cmp_k_w1, cmp_k_w2, cmp_v_pe, cmp_v_w1, cmp_v_w2,
           w_out, ln1_g, ln1_b, w_router, b_router, w_up, b_up, w_down, b_down,
           w_ple_gate, b_ple_gate, w_ple, ln2_g, ln2_b, alpha):
    T, D = x2.shape
    pool_w = D // 2
    nsa_w = D - pool_w
    nh = nsa_w // HEAD_DIM
    in_w = w_in.shape[1]
    kvw = (in_w - D - N_GATES * nh) // N_KV_GROUPS_FIELDS
    G = kvw // HEAD_DIM
    hg = nh // G
    n_experts = w_router.shape[1]

    xb = x2.astype(BF16)
    w_in_b = w_in.astype(BF16)
    proj_a = _matmul_cols(xb, w_in_b, 0, D)
    proj_b = _matmul_cols(xb, w_in_b, D, 2 * kvw)
    proj_c = _matmul_cols(xb, w_in_b, D + 2 * kvw, 4 * kvw)
    w_gates = jnp.pad(w_in_b[:, D + 6 * kvw:], ((0, 0), (0, LANES - N_GATES * nh)))
    gates = _matmul_cols(xb, w_gates, 0, LANES)

    y_pool = _pool_mixer(proj_a, w_pool.astype(BF16), pool_scale, B, S, pool_w)

    cos2, sin2 = _rope_tables(jnp.arange(S))
    nc = S // CMP_STRIDE
    ccos, csin = _rope_tables(jnp.arange(nc) * CMP_STRIDE + CMP_BLOCK - 1)
    kvc3 = proj_b.reshape(B, nc, CMP_STRIDE * 2 * kvw)
    kc = _compress(kvc3, 0, cmp_k_pe, cmp_k_w1, cmp_k_w2, ccos, csin, B, G, rope=True)
    vc = _compress(kvc3, 1, cmp_v_pe, cmp_v_w1, cmp_v_w2, ccos, csin, B, G, rope=False)
    q_rot, k_slc, v_slc, k_win, v_win = _rope_prep(proj_a, proj_c, cos2, sin2, B, S, nsa_w, G)
    o_cmp, sel_bias = _cmp_select(q_rot, kc, vc, B, S, G, hg)
    tq = _pick_tile(S, (256, 128))
    o_slc = _selected_attention(q_rot, k_slc, v_slc, sel_bias, B, S, G, hg, tq)
    o_win = _window_attention(q_rot, k_win, v_win, B, S, G, hg, tq, WINDOW)
    y_nsa = _gate_combine(gates, o_cmp, o_slc, o_win, B, S, nh)

    ne_pad = -(-n_experts // LANES) * LANES
    w_router_p = jnp.pad(w_router, ((0, 0), (0, ne_pad - n_experts))).astype(BF16)
    b_router_p = jnp.pad(b_router, (0, ne_pad - n_experts)).reshape(1, ne_pad)
    h, hb, logits = _mix_ln_router(y_pool, y_nsa, w_out.astype(BF16), x2, ln1_g, ln1_b,
                                   w_router_p, b_router_p, alpha)

    row_tok, row_w, blk_e, blk_valid, pos = _routing_tables(logits, n_experts)
    F = w_down.shape[1]
    w_pairs = _pair_split_cast(w_up)
    b_pairs = b_up.reshape(n_experts, F // LANES, LANES, 2).transpose(0, 1, 3, 2).reshape(n_experts, 1, 2 * F)
    act = _moe_up(row_tok, blk_e, blk_valid, h, w_pairs, b_pairs)
    y = _moe_down(blk_e, blk_valid, act, w_down.astype(BF16), b_down.reshape(n_experts, 1, D), row_w)

    z = _ple_residual(hb, w_ple_gate.astype(BF16), b_ple_gate, p2, w_ple.astype(BF16), h, alpha)
    return _combine_ln(pos, y, z, ln2_g, ln2_b)


def kernel(x, p, w_in, w_pool, pool_scale, cmp_k_pe, cmp_k_w1, cmp_k_w2, cmp_v_pe, cmp_v_w1, cmp_v_w2, w_out, ln1_g, ln1_b, w_router, b_router, w_up, b_up, w_down, b_down, w_ple_gate, b_ple_gate, w_ple, ln2_g, ln2_b):
    B, S, D = x.shape
    depth = w_in.shape[0]
    alpha = (2 * depth) ** 0.25
    x2 = x.reshape(B * S, D)
    for i in range(depth):
        x2 = _layer(x2, p[i].reshape(B * S, -1), B, S, w_in[i], w_pool[i], pool_scale[i],
                    cmp_k_pe[i], cmp_k_w1[i], cmp_k_w2[i], cmp_v_pe[i], cmp_v_w1[i], cmp_v_w2[i],
                    w_out[i], ln1_g[i], ln1_b[i], w_router[i], b_router[i], w_up[i], b_up[i],
                    w_down[i], b_down[i], w_ple_gate[i], b_ple_gate[i], w_ple[i], ln2_g[i], ln2_b[i], alpha)
    return x2.reshape(B, S, D)
```

```python
import functools
import math

import jax
import jax.numpy as jnp
import numpy as np
from jax import lax
from jax.experimental import pallas as pl
from jax.experimental.pallas import tpu as pltpu

HEAD_DIM = 128
LANES = 128
N_KV_GROUPS_FIELDS = 6
N_GATES = 3
POOL_WINDOWS = (2, 4, 8, 16)
CMP_BLOCK = 32
CMP_STRIDE = 16
SEL_BLOCK = 64
N_SEL = 16
WINDOW = 512
ROPE_THETA = 10000.0
TOP_K = 4
SWIGLU_LIMIT = 7.0
SWIGLU_ALPHA = 1.702
EXPERT_ROW_BLOCK = 256
LN_EPS = 1e-5
NEG_INF = -1e30
FORCE_SCORE = 1e4
LOG2E = 1.4426950408889634
EPI_ROWS = 128
ISSUE_UNROLL = 8
MIB = 1024 * 1024

BF16 = jnp.bfloat16
F32 = jnp.float32
NT_DIMS = (((1,), (1,)), ((), ()))


def _params(sem, vmem_mib=None):
    kw = dict(dimension_semantics=sem)
    if vmem_mib is not None:
        kw["vmem_limit_bytes"] = vmem_mib * MIB
    return pltpu.CompilerParams(**kw)


def _pick_tile(n, candidates):
    for c in candidates:
        if n % c == 0:
            return c
    raise ValueError(f"no tile for {n}")


def _mm_kernel(x_ref, w_ref, o_ref):
    @pl.when(pl.program_id(2) == 0)
    def _():
        o_ref[...] = jnp.zeros_like(o_ref)

    o_ref[...] += jnp.dot(x_ref[...], w_ref[...], preferred_element_type=F32)


def _matmul_cols(x, w, col0, ncols):
    M, K = x.shape
    tm = _pick_tile(M, (1024, 512, 256))
    tk = _pick_tile(K, (2048, 1024, 512, 256, 128))
    tn = _pick_tile(math.gcd(col0, ncols) if col0 else ncols, (1024, 512, 256, 128))
    off = col0 // tn
    return pl.pallas_call(
        _mm_kernel,
        name="in_proj",
        grid=(M // tm, ncols // tn, K // tk),
        in_specs=[pl.BlockSpec((tm, tk), lambda i, j, k: (i, k)),
                  pl.BlockSpec((tk, tn), lambda i, j, k: (k, j + off))],
        out_specs=pl.BlockSpec((tm, tn), lambda i, j, k: (i, j)),
        out_shape=jax.ShapeDtypeStruct((M, ncols), F32),
        compiler_params=_params(("parallel", "parallel", "arbitrary"), 40),
    )(x, w)


def _pool_kernel(u_ref, prev_ref, w_ref, sc_ref, o_ref, ext_ref, *, ts, halo):
    g = pl.program_id(0)
    i = pl.program_id(2)
    cur = u_ref[...]
    ext_ref[pl.ds(halo, ts), :] = cur

    @pl.when(i == 0)
    def _():
        ext_ref[pl.ds(0, halo), :] = jnp.zeros((halo, cur.shape[1]), F32)

    @pl.when(i > 0)
    def _():
        ext_ref[pl.ds(0, halo), :] = prev_ref[...]

    def back(d):
        return ext_ref[pl.ds(halo - d, ts), :]

    s2 = cur + back(1)
    s4 = s2 + back(2) + back(3)
    s8 = s4 + back(4) + back(5) + back(6) + back(7)
    s16 = s8
    for d in range(8, 16):
        s16 = s16 + back(d)
    ssum = jnp.where(g == 0, s2, jnp.where(g == 1, s4, jnp.where(g == 2, s8, s16)))
    win = jnp.left_shift(2, g)
    t = i * ts + lax.broadcasted_iota(jnp.int32, (ts, 1), 0)
    cnt = jnp.minimum(t + 1, win).astype(F32)
    pooled = ssum / cnt - cur
    mixed = jnp.dot(pooled.astype(BF16), w_ref[0], preferred_element_type=F32)
    o_ref[...] = (mixed * sc_ref[...]).astype(o_ref.dtype)


def _pool_mixer(proj_a, w_pool_b, pool_scale, B, S, pool_w):
    T = B * S
    ng = len(POOL_WINDOWS)
    pg = pool_w // ng
    halo = POOL_WINDOWS[-1]
    ts = _pick_tile(S, (512, 256, 128))
    ns = S // ts
    hb = ts // halo
    kern = functools.partial(_pool_kernel, ts=ts, halo=halo)
    return pl.pallas_call(
        kern,
        name="pool_mixer",
        grid=(ng, B, ns),
        in_specs=[
            pl.BlockSpec((ts, pg), lambda g, b, i: (b * ns + i, g)),
            pl.BlockSpec((halo, pg), lambda g, b, i: (jnp.maximum((b * ns + i) * hb - 1, 0), g)),
            pl.BlockSpec((1, pg, pg), lambda g, b, i: (g, 0, 0)),
            pl.BlockSpec((1, pg), lambda g, b, i: (0, g)),
        ],
        out_specs=pl.BlockSpec((ts, pg), lambda g, b, i: (b * ns + i, g)),
        out_shape=jax.ShapeDtypeStruct((T, pool_w), BF16),
        scratch_shapes=[pltpu.VMEM((ts + halo, pg), F32)],
        compiler_params=_params(("parallel", "parallel", "arbitrary"), 32),
    )(proj_a, proj_a, w_pool_b, pool_scale.reshape(1, pool_w))


def _rope_tables(pos):
    inv = ROPE_THETA ** (-jnp.arange(0, HEAD_DIM, 2, dtype=F32) / HEAD_DIM)
    ang = pos.astype(F32)[:, None] * inv[None, :]
    c, s = jnp.cos(ang), jnp.sin(ang)
    return jnp.concatenate([c, c], axis=-1), jnp.concatenate([-s, s], axis=-1)


def _rope(x, cos2, sin2):
    return x * cos2 + pltpu.roll(x, HEAD_DIM // 2, 1) * sin2


def _compress_kernel(*refs, rope, nl):
    x_refs = refs[:nl]
    pe_ref, w1_ref, w2_ref, cos_ref, sin_ref, o_ref = refs[nl:]
    lo = jnp.concatenate([(x_refs[l][0] + pe_ref[l:l + 1, :]).astype(BF16) for l in range(nl)], axis=1)
    hi = jnp.concatenate([(x_refs[l][0] + pe_ref[nl + l:nl + l + 1, :]).astype(BF16) for l in range(nl)], axis=1)
    half = nl * HEAD_DIM
    a = jnp.dot(lo, w1_ref[0:half, :], preferred_element_type=F32)
    b = jnp.dot(hi, w1_ref[half:2 * half, :], preferred_element_type=F32)
    nc = a.shape[0]
    hid = jax.nn.gelu(a + pltpu.roll(b, nc - 1, 0))
    out = jnp.dot(hid.astype(BF16), w2_ref[...], preferred_element_type=F32)
    if rope:
        out = _rope(out, cos_ref[...], sin_ref[...])
    o_ref[0, 0] = out.astype(o_ref.dtype)


def _compress(kvc3, field, pe, w1, w2, cos2, sin2, B, G, rope):
    nc = kvc3.shape[1]
    nl = CMP_STRIDE
    cols_per_tok = kvc3.shape[2] // nl // LANES
    kern = functools.partial(_compress_kernel, rope=rope, nl=nl)

    def x_spec(l):
        return pl.BlockSpec((1, nc, LANES), lambda b, g: (b, 0, l * cols_per_tok + field * G + g))

    const = lambda b, g: (0, 0)
    return pl.pallas_call(
        kern,
        name="compress_k" if rope else "compress_v",
        grid=(B, G),
        in_specs=[x_spec(l) for l in range(nl)] + [
            pl.BlockSpec((CMP_BLOCK, HEAD_DIM), const),
            pl.BlockSpec((CMP_BLOCK * HEAD_DIM, HEAD_DIM), const),
            pl.BlockSpec((HEAD_DIM, HEAD_DIM), const),
            pl.BlockSpec((nc, HEAD_DIM), const),
            pl.BlockSpec((nc, HEAD_DIM), const),
        ],
        out_specs=pl.BlockSpec((1, 1, nc, HEAD_DIM), lambda b, g: (b, g, 0, 0)),
        out_shape=jax.ShapeDtypeStruct((B, G, nc, HEAD_DIM), BF16),
        compiler_params=_params(("parallel", "parallel"), 32),
    )(*([kvc3] * nl), pe, w1.reshape(CMP_BLOCK * HEAD_DIM, HEAD_DIM).astype(BF16), w2.astype(BF16), cos2, sin2)


def _rope_prep_kernel(q_ref, kv_ref, cos_ref, sin_ref, qo_ref, ks_ref, vs_ref, kw_ref, vw_ref, *, nh, G, ts):
    i = pl.program_id(1)
    c = cos_ref[...]
    s = sin_ref[...]
    kvw = G * HEAD_DIM
    for h in range(nh):
        qo_ref[0, h] = _rope(q_ref[:, h * HEAD_DIM:(h + 1) * HEAD_DIM], c, s).astype(BF16)
    blk = (i * ts + lax.broadcasted_iota(jnp.int32, (ts, LANES), 0)) // SEL_BLOCK
    onehot = (blk == lax.broadcasted_iota(jnp.int32, (ts, LANES), 1)).astype(BF16)
    for g in range(G):
        lo = g * HEAD_DIM
        ks_ref[0, g, :, 0:HEAD_DIM] = _rope(kv_ref[:, lo:lo + HEAD_DIM], c, s).astype(BF16)
        ks_ref[0, g, :, HEAD_DIM:2 * HEAD_DIM] = onehot
        vs_ref[0, g] = kv_ref[:, kvw + lo:kvw + lo + HEAD_DIM].astype(BF16)
        kw_ref[0, g] = _rope(kv_ref[:, 2 * kvw + lo:2 * kvw + lo + HEAD_DIM], c, s).astype(BF16)
        vw_ref[0, g] = kv_ref[:, 3 * kvw + lo:3 * kvw + lo + HEAD_DIM].astype(BF16)


def _rope_prep(proj_a, proj_c, cos2, sin2, B, S, nsa_w, G):
    nh = nsa_w // HEAD_DIM
    ts = _pick_tile(S, (256, 128))
    ns = S // ts
    kern = functools.partial(_rope_prep_kernel, nh=nh, G=G, ts=ts)
    hm = lambda b, i: (b, 0, i, 0)
    return pl.pallas_call(
        kern,
        name="rope_prep",
        grid=(B, ns),
        in_specs=[
            pl.BlockSpec((ts, nsa_w), lambda b, i: (b * ns + i, 1)),
            pl.BlockSpec((ts, 4 * G * HEAD_DIM), lambda b, i: (b * ns + i, 0)),
            pl.BlockSpec((ts, HEAD_DIM), lambda b, i: (i, 0)),
            pl.BlockSpec((ts, HEAD_DIM), lambda b, i: (i, 0)),
        ],
        out_specs=[
            pl.BlockSpec((1, nh, ts, HEAD_DIM), hm),
            pl.BlockSpec((1, G, ts, 2 * HEAD_DIM), hm),
            pl.BlockSpec((1, G, ts, HEAD_DIM), hm),
            pl.BlockSpec((1, G, ts, HEAD_DIM), hm),
            pl.BlockSpec((1, G, ts, HEAD_DIM), hm),
        ],
        out_shape=[
            jax.ShapeDtypeStruct((B, nh, S, HEAD_DIM), BF16),
            jax.ShapeDtypeStruct((B, G, S, 2 * HEAD_DIM), BF16),
            jax.ShapeDtypeStruct((B, G, S, HEAD_DIM), BF16),
            jax.ShapeDtypeStruct((B, G, S, HEAD_DIM), BF16),
            jax.ShapeDtypeStruct((B, G, S, HEAD_DIM), BF16),
        ],
        compiler_params=_params(("parallel", "parallel"), 40),
    )(proj_a, proj_c, cos2, sin2)


def _cmp_select_kernel(q_ref, kc_ref, vc_ref, ovl_ref, o_ref, bias_ref, *, hg, tq, n_sel, n_keep, scale):
    i = pl.program_id(2)
    nc = kc_ref.shape[2]
    kc = kc_ref[0, 0]
    vc = vc_ref[0, 0]
    t = i * tq + lax.broadcasted_iota(jnp.int32, (tq, nc), 0)
    n = lax.broadcasted_iota(jnp.int32, (tq, nc), 1)
    valid = (n * CMP_STRIDE + (CMP_BLOCK - 1) <= t) & (n < nc - 1)
    ovl = ovl_ref[...]
    imp = jnp.zeros((n_sel, tq), F32)
    for h in range(hg):
        s = lax.dot_general(q_ref[0, h], kc, NT_DIMS, preferred_element_type=F32) * scale
        s = jnp.where(valid, s, NEG_INF)
        m = jnp.max(s, axis=-1, keepdims=True)
        e = jnp.where(valid, jnp.exp(s - m), 0.0)
        den = jnp.sum(e, axis=-1, keepdims=True)
        pc = (e / jnp.where(den > 0.0, den, 1.0)).astype(BF16)
        o_ref[0, h] = jnp.dot(pc, vc, preferred_element_type=F32)
        imp = imp + lax.dot_general(ovl, pc, NT_DIMS, preferred_element_type=F32)
    j = lax.broadcasted_iota(jnp.int32, (n_sel, tq), 0)
    tb = (i * tq + lax.broadcasted_iota(jnp.int32, (n_sel, tq), 1)) // SEL_BLOCK
    forced = (j == 0) | (j == tb) | (j == tb - 1)
    score = jnp.where(forced, FORCE_SCORE, jnp.where(j <= tb, imp, -1.0))
    rank = jnp.zeros((n_sel, tq), jnp.int32)
    for jp in range(n_sel):
        row = score[jp:jp + 1, :]
        ahead = (row > score) | ((row == score) & (jp < j))
        rank = rank + ahead.astype(jnp.int32)
    bias_t = jnp.where(rank < n_keep, 0.0, NEG_INF)
    if n_sel < LANES:
        bias_t = jnp.concatenate([bias_t, jnp.zeros((LANES - n_sel, tq), F32)], axis=0)
    bias_ref[0, 0] = bias_t.T.astype(BF16)


def _overlap_t(n_cmp_pad, n_sel):
    n_cmp = n_cmp_pad - 1
    cs = np.arange(n_cmp) * CMP_STRIDE
    ce = cs + CMP_BLOCK
    ss = np.arange(n_sel) * SEL_BLOCK
    se = ss + SEL_BLOCK
    ov = np.clip(np.minimum(ce[:, None], se[None, :]) - np.maximum(cs[:, None], ss[None, :]), 0, None)
    out = np.zeros((n_sel, n_cmp_pad), np.float32)
    out[:, :n_cmp] = (ov / CMP_BLOCK).T
    return jnp.asarray(out, dtype=BF16)


def _cmp_select(q_rot, kc, vc, B, S, G, hg):
    nc = kc.shape[2]
    n_sel = S // SEL_BLOCK
    assert n_sel <= LANES and n_sel % 8 == 0
    tq = _pick_tile(S, (256, 128))
    nq = S // tq
    nh = G * hg
    kern = functools.partial(_cmp_select_kernel, hg=hg, tq=tq, n_sel=n_sel,
                             n_keep=min(N_SEL, n_sel), scale=HEAD_DIM ** -0.5)
    return pl.pallas_call(
        kern,
        name="cmp_attn_select",
        grid=(B, G, nq),
        in_specs=[
            pl.BlockSpec((1, hg, tq, HEAD_DIM), lambda b, g, i: (b, g, i, 0)),
            pl.BlockSpec((1, 1, nc, HEAD_DIM), lambda b, g, i: (b, g, 0, 0)),
            pl.BlockSpec((1, 1, nc, HEAD_DIM), lambda b, g, i: (b, g, 0, 0)),
            pl.BlockSpec((n_sel, nc), lambda b, g, i: (0, 0)),
        ],
        out_specs=[
            pl.BlockSpec((1, hg, tq, HEAD_DIM), lambda b, g, i: (b, g, i, 0)),
            pl.BlockSpec((1, 1, tq, LANES), lambda b, g, i: (b, g, i, 0)),
        ],
        out_shape=[
            jax.ShapeDtypeStruct((B, nh, S, HEAD_DIM), F32),
            jax.ShapeDtypeStruct((B, G, S, LANES), BF16),
        ],
        compiler_params=_params(("parallel", "parallel", "parallel"), 40),
    )(q_rot, kc, vc, _overlap_t(nc, n_sel))


def _lane_fold(x, op):
    r = x[:, :LANES]
    for a in range(1, x.shape[1] // LANES):
        r = op(r, x[:, a * LANES:(a + 1) * LANES])
    return r


def _slc_kernel(q_ref, bias_ref, k_ref, v_ref, o_ref, s_ref, m_ref, l_ref, acc_ref, *, hg, tq, scale):
    i = pl.program_id(2)
    rows = hg * tq
    c = scale * LOG2E
    q = q_ref[0].reshape(rows, HEAD_DIM)
    b = bias_ref[0, 0]
    qa = jnp.concatenate([q, jnp.concatenate([b] * hg, axis=0)], axis=1)

    def k_tile(j):
        return k_ref[0, 0, pl.ds(pl.multiple_of(j * tq, tq), tq), :]

    def v_tile(j):
        return v_ref[0, 0, pl.ds(pl.multiple_of(j * tq, tq), tq), :]

    m_ref[...] = jnp.full_like(m_ref, NEG_INF)

    def scores(j, carry):
        s = lax.dot_general(qa, k_tile(j), NT_DIMS, preferred_element_type=F32)
        s_ref[j] = s
        m_ref[...] = jnp.maximum(m_ref[...], _lane_fold(s, jnp.maximum))
        return carry

    lax.fori_loop(0, i, scores, 0)
    sd = lax.dot_general(qa, k_tile(i), NT_DIMS, preferred_element_type=F32)
    tpos = lax.broadcasted_iota(jnp.int32, (rows, tq), 0) & (tq - 1)
    kpos = lax.broadcasted_iota(jnp.int32, (rows, tq), 1)
    sd = jnp.where(kpos <= tpos, sd, NEG_INF)
    mrow = jnp.max(jnp.maximum(m_ref[...], _lane_fold(sd, jnp.maximum)), axis=-1, keepdims=True) * c
    m_ref[...] = jnp.broadcast_to(mrow, (rows, LANES))
    l_ref[...] = jnp.zeros_like(l_ref)
    acc_ref[...] = jnp.zeros_like(acc_ref)

    def probs(s):
        mfull = jnp.concatenate([m_ref[...]] * (tq // LANES), axis=1)
        return jnp.exp2(s * c - mfull)

    def accumulate(j, carry):
        p = probs(s_ref[j])
        l_ref[...] += _lane_fold(p, jnp.add)
        acc_ref[...] += jnp.dot(p.astype(BF16), v_tile(j), preferred_element_type=F32)
        return carry

    lax.fori_loop(0, i, accumulate, 0)
    p = probs(sd)
    den = jnp.sum(l_ref[...] + _lane_fold(p, jnp.add), axis=-1, keepdims=True)
    acc = acc_ref[...] + jnp.dot(p.astype(BF16), v_tile(i), preferred_element_type=F32)
    o_ref[0] = (acc / den).reshape(hg, tq, HEAD_DIM)


def _selected_attention(q_rot, k_aug, v, bias, B, S, G, hg, tq):
    assert tq & (tq - 1) == 0 and tq % LANES == 0
    nq = S // tq
    rows = hg * tq
    kern = functools.partial(_slc_kernel, hg=hg, tq=tq, scale=HEAD_DIM ** -0.5)
    return pl.pallas_call(
        kern,
        name="selected_attn",
        grid=(B, G, nq),
        in_specs=[
            pl.BlockSpec((1, hg, tq, HEAD_DIM), lambda b, g, i: (b, g, i, 0)),
            pl.BlockSpec((1, 1, tq, LANES), lambda b, g, i: (b, g, i, 0)),
            pl.BlockSpec((1, 1, S, 2 * HEAD_DIM), lambda b, g, i: (b, g, 0, 0)),
            pl.BlockSpec((1, 1, S, HEAD_DIM), lambda b, g, i: (b, g, 0, 0)),
        ],
        out_specs=pl.BlockSpec((1, hg, tq, HEAD_DIM), lambda b, g, i: (b, g, i, 0)),
        out_shape=jax.ShapeDtypeStruct((B, G * hg, S, HEAD_DIM), F32),
        scratch_shapes=[pltpu.VMEM((nq, rows, tq), F32), pltpu.VMEM((rows, LANES), F32),
                        pltpu.VMEM((rows, LANES), F32), pltpu.VMEM((rows, HEAD_DIM), F32)],
        compiler_params=_params(("parallel", "parallel", "arbitrary"), 48),
    )(q_rot, bias, k_aug, v)


def _win_kernel(q_ref, g_ref, oc_ref, os_ref, *refs, hg, tq, nprev, window, scale):
    nt = nprev + 1
    k_refs, v_refs, y_ref = refs[:nt], refs[nt:2 * nt], refs[2 * nt]
    i = pl.program_id(2)
    gate = jax.nn.sigmoid(g_ref[0])
    c = scale * LOG2E
    nk = nt * tq
    kcat = jnp.concatenate([r[0, 0] for r in k_refs], axis=0)
    vcat = jnp.concatenate([r[0, 0] for r in v_refs], axis=0)
    tpos = i * tq + lax.broadcasted_iota(jnp.int32, (tq, nk), 0)
    kpos = (i - nprev) * tq + lax.broadcasted_iota(jnp.int32, (tq, nk), 1)
    mask = (kpos <= tpos) & (kpos > tpos - window) & (kpos >= 0)
    for h in range(hg):
        s = lax.dot_general(q_ref[0, h], kcat, NT_DIMS, preferred_element_type=F32)
        s = jnp.where(mask, s, NEG_INF)
        m = jnp.max(s, axis=-1, keepdims=True)
        p = jnp.exp2((s - m) * c)
        den = jnp.sum(p, axis=-1, keepdims=True)
        o_win = jnp.dot(p.astype(BF16), vcat, preferred_element_type=F32) / den
        gc = N_GATES * h
        y = (gate[:, gc:gc + 1] * oc_ref[0, h] + gate[:, gc + 1:gc + 2] * os_ref[0, h]
             + gate[:, gc + 2:gc + 3] * o_win)
        y_ref[:, h * HEAD_DIM:(h + 1) * HEAD_DIM] = y.astype(y_ref.dtype)


def _window_attention_gated(q_rot, k, v, gates_g, o_cmp, o_slc, B, S, G, hg, tq, window):
    assert window % tq == 0
    nprev = window // tq
    nq = S // tq
    kern = functools.partial(_win_kernel, hg=hg, tq=tq, nprev=nprev, window=window, scale=HEAD_DIM ** -0.5)

    def kv_spec(a):
        return pl.BlockSpec((1, 1, tq, HEAD_DIM), lambda b, g, i: (b, g, jnp.maximum(i - nprev + a, 0), 0))

    kv_specs = [kv_spec(a) for a in range(nprev + 1)]
    heads = pl.BlockSpec((1, hg, tq, HEAD_DIM), lambda b, g, i: (b, g, i, 0))
    return pl.pallas_call(
        kern,
        name="window_attn_gated_sum",
        grid=(B, G, nq),
        in_specs=[heads, pl.BlockSpec((1, tq, LANES), lambda b, g, i: (g, b * nq + i, 0)), heads, heads]
        + kv_specs + kv_specs,
        out_specs=pl.BlockSpec((tq, hg * HEAD_DIM), lambda b, g, i: (b * nq + i, g)),
        out_shape=jax.ShapeDtypeStruct((B * S, G * hg * HEAD_DIM), BF16),
        compiler_params=_params(("parallel", "parallel", "arbitrary"), 40),
    )(q_rot, gates_g, o_cmp, o_slc, *([k] * (nprev + 1)), *([v] * (nprev + 1)))


def _layer_norm(z, g, b):
    mu = jnp.mean(z, axis=-1, keepdims=True)
    zc = z - mu
    var = jnp.mean(zc * zc, axis=-1, keepdims=True)
    return zc * lax.rsqrt(var + LN_EPS) * g + b


def _mix_kernel(yp_ref, yn_ref, w_ref, x_ref, g_ref, b_ref, wr_ref, br_ref, h_ref, hb_ref, lg_ref, acc_ref,
                *, nkp, alpha):
    k = pl.program_id(1)

    @pl.when(k == 0)
    def _():
        acc_ref[...] = jnp.zeros_like(acc_ref)

    @pl.when(k < nkp)
    def _():
        acc_ref[...] += jnp.dot(yp_ref[...], w_ref[...], preferred_element_type=F32)

    @pl.when(k >= nkp)
    def _():
        acc_ref[...] += jnp.dot(yn_ref[...], w_ref[...], preferred_element_type=F32)

    @pl.when(k == pl.num_programs(1) - 1)
    def _():
        def epilogue(r, carry):
            rows = pl.ds(pl.multiple_of(r * EPI_ROWS, EPI_ROWS), EPI_ROWS)
            h = _layer_norm(alpha * x_ref[rows, :] + acc_ref[rows, :], g_ref[...], b_ref[...])
            hb = h.astype(BF16)
            h_ref[rows, :] = h
            hb_ref[rows, :] = hb
            lg_ref[rows, :] = jnp.dot(hb, wr_ref[...], preferred_element_type=F32) + br_ref[...]
            return carry

        lax.fori_loop(0, acc_ref.shape[0] // EPI_ROWS, epilogue, 0)


def _mix_ln_router(y_pool, y_nsa, w_out_b, x2, ln_g, ln_b, w_router_p, b_router_p, alpha):
    T, D = x2.shape
    kp = y_pool.shape[1]
    tm = _pick_tile(T, (512, 256, 128))
    tk = _pick_tile(math.gcd(kp, y_nsa.shape[1]), (512, 256, 128))
    nkp = kp // tk
    nk = nkp + y_nsa.shape[1] // tk
    ne = w_router_p.shape[1]
    kern = functools.partial(_mix_kernel, nkp=nkp, alpha=alpha)
    row = lambda i, k: (i, 0)
    const = lambda i, k: (0, 0)
    return pl.pallas_call(
        kern,
        name="out_proj_ln1_router",
        grid=(T // tm, nk),
        in_specs=[
            pl.BlockSpec((tm, tk), lambda i, k: (i, jnp.minimum(k, nkp - 1))),
            pl.BlockSpec((tm, tk), lambda i, k: (i, jnp.maximum(k - nkp, 0))),
            pl.BlockSpec((tk, D), lambda i, k: (k, 0)),
            pl.BlockSpec((tm, D), row, pipeline_mode=pl.Buffered(1)),
            pl.BlockSpec((1, D), const),
            pl.BlockSpec((1, D), const),
            pl.BlockSpec((D, ne), const),
            pl.BlockSpec((1, ne), const),
        ],
        out_specs=[pl.BlockSpec((tm, D), row), pl.BlockSpec((tm, D), row), pl.BlockSpec((tm, ne), row)],
        out_shape=[jax.ShapeDtypeStruct((T, D), F32), jax.ShapeDtypeStruct((T, D), BF16),
                   jax.ShapeDtypeStruct((T, ne), F32)],
        scratch_shapes=[pltpu.VMEM((tm, D), F32)],
        compiler_params=_params(("parallel", "arbitrary"), 56),
    )(y_pool, y_nsa, w_out_b, x2, ln_g.reshape(1, D), ln_b.reshape(1, D), w_router_p, b_router_p)


def _row_copy(src_hbm, row, dst_ref, slot, sem):
    return pltpu.make_async_copy(src_hbm.at[pl.ds(row, 1)], dst_ref.at[pl.ds(slot, 1)], sem)


def _rows_copy(src_hbm, dst_ref, sem, n):
    return pltpu.make_async_copy(src_hbm.at[pl.ds(0, n)], dst_ref, sem)


PAIR = 2 * LANES


def _pair_split_kernel(w_ref, perm_ref, o_ref):
    perm = perm_ref[...]
    for c in range(w_ref.shape[2] // PAIR):
        sl = slice(c * PAIR, (c + 1) * PAIR)
        o_ref[0, :, sl] = jnp.dot(w_ref[0, :, sl].astype(BF16), perm, preferred_element_type=F32).astype(BF16)


def _pair_split_cast(w_up):
    E, D, F2 = w_up.shape
    assert F2 % PAIR == 0
    td = _pick_tile(D, (512, 256, 128))
    perm = np.zeros((PAIR, PAIR), np.float32)
    perm[2 * np.arange(LANES), np.arange(LANES)] = 1.0
    perm[2 * np.arange(LANES) + 1, LANES + np.arange(LANES)] = 1.0
    return pl.pallas_call(
        _pair_split_kernel,
        name="w_up_pair_split",
        grid=(E, D // td),
        in_specs=[pl.BlockSpec((1, td, F2), lambda e, i: (e, i, 0)),
                  pl.BlockSpec((PAIR, PAIR), lambda e, i: (0, 0))],
        out_specs=pl.BlockSpec((1, td, F2), lambda e, i: (e, i, 0)),
        out_shape=jax.ShapeDtypeStruct((E, D, F2), BF16),
        compiler_params=_params(("parallel", "parallel"), 32),
    )(w_up, jnp.asarray(perm, dtype=BF16))


def _moe_up_kernel(tok_ref, be_ref, bv_ref, h_hbm, w_ref, b_ref, act_ref, xbuf_ref, sem, *, rb):
    i = pl.program_id(0)
    nb = pl.num_programs(0)

    def gather(blk, slot):
        def issue(r, carry):
            _row_copy(h_hbm, tok_ref[blk * rb + r], xbuf_ref.at[slot], r, sem.at[slot]).start()
            return carry

        lax.fori_loop(0, rb, issue, 0, unroll=ISSUE_UNROLL)

    @pl.when((i == 0) & (bv_ref[0] != 0))
    def _():
        gather(0, 0)

    nxt = jnp.minimum(i + 1, nb - 1)

    @pl.when((i + 1 < nb) & (bv_ref[nxt] != 0))
    def _():
        gather(nxt, nxt % 2)

    @pl.when(bv_ref[i] == 0)
    def _():
        act_ref[...] = jnp.zeros_like(act_ref)

    @pl.when(bv_ref[i] != 0)
    def _():
        slot = i % 2
        _rows_copy(h_hbm, xbuf_ref.at[slot], sem.at[slot], rb).wait()
        xb = xbuf_ref[slot].astype(BF16)
        hcat = jnp.dot(xb, w_ref[0], preferred_element_type=F32) + b_ref[0]
        for c in range(hcat.shape[1] // PAIR):
            gate = jnp.minimum(hcat[:, c * PAIR:c * PAIR + LANES], SWIGLU_LIMIT)
            up = jnp.clip(hcat[:, c * PAIR + LANES:(c + 1) * PAIR], -SWIGLU_LIMIT, SWIGLU_LIMIT)
            act = gate * jax.nn.sigmoid(SWIGLU_ALPHA * gate) * (up + 1.0)
            act_ref[:, c * LANES:(c + 1) * LANES] = act.astype(act_ref.dtype)


def _moe_up(row_tok, blk_e, blk_valid, h, w_pairs, b_pairs):
    n_rows = row_tok.shape[0]
    rb = EXPERT_ROW_BLOCK
    E, D, F2 = w_pairs.shape
    wmap = lambda i, tok, be, bv: (be[i], 0, 0)
    return pl.pallas_call(
        functools.partial(_moe_up_kernel, rb=rb),
        name="moe_gather_up_swiglu",
        grid_spec=pltpu.PrefetchScalarGridSpec(
            num_scalar_prefetch=3,
            grid=(n_rows // rb,),
            in_specs=[
                pl.BlockSpec(memory_space=pl.ANY),
                pl.BlockSpec((1, D, F2), wmap),
                pl.BlockSpec((1, 1, F2), wmap),
            ],
            out_specs=pl.BlockSpec((rb, F2 // 2), lambda i, tok, be, bv: (i, 0)),
            scratch_shapes=[pltpu.VMEM((2, rb, D), F32), pltpu.SemaphoreType.DMA((2,))],
        ),
        out_shape=jax.ShapeDtypeStruct((n_rows, F2 // 2), BF16),
        compiler_params=_params(("arbitrary",), 56),
    )(row_tok, blk_e, blk_valid, h, w_pairs, b_pairs)


def _moe_down_kernel(be_ref, bv_ref, act_ref, wd_ref, bd_ref, rw_ref, y_ref, wb_ref):
    i = pl.program_id(0)

    @pl.when(bv_ref[i] == 0)
    def _():
        y_ref[...] = jnp.zeros_like(y_ref)

    @pl.when((bv_ref[i] != 0) & ((i == 0) | (be_ref[i] != be_ref[jnp.maximum(i - 1, 0)])))
    def _():
        wb_ref[...] = wd_ref[0].astype(BF16)

    @pl.when(bv_ref[i] != 0)
    def _():
        y = jnp.dot(act_ref[...], wb_ref[...], preferred_element_type=F32) + bd_ref[0]
        y_ref[...] = y * rw_ref[...]


def _moe_down(blk_e, blk_valid, act, wd, bd, row_w):
    n_rows, F = act.shape
    rb = EXPERT_ROW_BLOCK
    E, _, D = wd.shape
    wmap = lambda i, be, bv: (be[i], 0, 0)
    return pl.pallas_call(
        _moe_down_kernel,
        name="moe_down",
        grid_spec=pltpu.PrefetchScalarGridSpec(
            num_scalar_prefetch=2,
            grid=(n_rows // rb,),
            in_specs=[
                pl.BlockSpec((rb, F), lambda i, be, bv: (i, 0)),
                pl.BlockSpec((1, F, D), wmap),
                pl.BlockSpec((1, 1, D), wmap),
                pl.BlockSpec((rb, 1), lambda i, be, bv: (i, 0)),
            ],
            out_specs=pl.BlockSpec((rb, D), lambda i, be, bv: (i, 0)),
            scratch_shapes=[pltpu.VMEM((F, D), BF16)],
        ),
        out_shape=jax.ShapeDtypeStruct((n_rows, D), F32),
        compiler_params=_params(("arbitrary",), 56),
    )(blk_e, blk_valid, act, wd, bd, row_w.reshape(n_rows, 1))


def _ple_kernel(hb_ref, w_ref, bg_ref, p_ref, wp_ref, h_ref, z_ref, *, alpha):
    k = pl.program_id(2)

    @pl.when(k == 0)
    def _():
        z_ref[...] = jnp.zeros_like(z_ref)

    z_ref[...] += jnp.dot(hb_ref[...], w_ref[...], preferred_element_type=F32)

    @pl.when(k == pl.num_programs(2) - 1)
    def _():
        gate = jax.nn.sigmoid(z_ref[...] + bg_ref[...])
        emb = jnp.dot(p_ref[...].astype(BF16), wp_ref[...], preferred_element_type=F32)
        z_ref[...] = alpha * h_ref[...] + gate * emb


def _ple_residual(hb, w_gate_b, b_gate, p2, w_ple_b, h, alpha):
    T, D = h.shape
    pd = p2.shape[1]
    tm = _pick_tile(T, (1024, 512, 256, 128))
    tn = _pick_tile(D, (1024, 512, 256, 128))
    tk = _pick_tile(D, (2048, 1024, 512, 256, 128))
    return pl.pallas_call(
        functools.partial(_ple_kernel, alpha=alpha),
        name="ple_gate_residual",
        grid=(T // tm, D // tn, D // tk),
        in_specs=[
            pl.BlockSpec((tm, tk), lambda i, j, k: (i, k)),
            pl.BlockSpec((tk, tn), lambda i, j, k: (k, j)),
            pl.BlockSpec((1, tn), lambda i, j, k: (0, j)),
            pl.BlockSpec((tm, pd), lambda i, j, k: (i, 0)),
            pl.BlockSpec((pd, tn), lambda i, j, k: (0, j)),
            pl.BlockSpec((tm, tn), lambda i, j, k: (i, j)),
        ],
        out_specs=pl.BlockSpec((tm, tn), lambda i, j, k: (i, j)),
        out_shape=jax.ShapeDtypeStruct((T, D), F32),
        compiler_params=_params(("parallel", "parallel", "arbitrary"), 48),
    )(hb, w_gate_b, b_gate.reshape(1, D), p2, w_ple_b, h)


def _combine_kernel(pos_ref, y_hbm, z_ref, g_ref, b_ref, o_ref, buf_ref, sem, *, tc, topk):
    i = pl.program_id(0)
    nb = pl.num_programs(0)

    def gather(blk, slot):
        def issue(t, carry):
            for k in range(topk):
                _row_copy(y_hbm, pos_ref[(blk * tc + t) * topk + k], buf_ref.at[slot, k], t, sem.at[slot]).start()
            return carry

        lax.fori_loop(0, tc, issue, 0, unroll=ISSUE_UNROLL // topk)

    @pl.when(i == 0)
    def _():
        gather(0, 0)

    @pl.when(i + 1 < nb)
    def _():
        gather(i + 1, (i + 1) % 2)

    slot = i % 2
    for k in range(topk):
        _rows_copy(y_hbm, buf_ref.at[slot, k], sem.at[slot], tc).wait()
    ffn = buf_ref[slot, 0]
    for k in range(1, topk):
        ffn = ffn + buf_ref[slot, k]
    o_ref[...] = _layer_norm(z_ref[...] + ffn, g_ref[...], b_ref[...])


def _combine_ln(pos, y, z, ln_g, ln_b):
    T, D = z.shape
    tc = _pick_tile(T, (64, 32))
    row = lambda i, pos: (i, 0)
    const = lambda i, pos: (0, 0)
    return pl.pallas_call(
        functools.partial(_combine_kernel, tc=tc, topk=TOP_K),
        name="moe_combine_ln2",
        grid_spec=pltpu.PrefetchScalarGridSpec(
            num_scalar_prefetch=1,
            grid=(T // tc,),
            in_specs=[
                pl.BlockSpec(memory_space=pl.ANY),
                pl.BlockSpec((tc, D), row),
                pl.BlockSpec((1, D), const),
                pl.BlockSpec((1, D), const),
            ],
            out_specs=pl.BlockSpec((tc, D), row),
            scratch_shapes=[pltpu.VMEM((2, TOP_K, tc, D), F32), pltpu.SemaphoreType.DMA((2,))],
        ),
        out_shape=jax.ShapeDtypeStruct((T, D), F32),
        compiler_params=_params(("arbitrary",), 32),
    )(pos, y, z, ln_g.reshape(1, D), ln_b.reshape(1, D))


def _routing_tables(logits, n_experts):
    T = logits.shape[0]
    A = T * TOP_K
    rb = EXPERT_ROW_BLOCK
    i32 = jnp.int32
    top_logit, top_e = lax.top_k(logits[:, :n_experts], TOP_K)
    top_w = jax.nn.softmax(top_logit, axis=-1)
    flat_e = top_e.reshape(-1).astype(i32)
    order = jnp.argsort(flat_e).astype(i32)
    sorted_pos = jnp.argsort(order).astype(i32)
    onehot = (flat_e[:, None] == jnp.arange(n_experts, dtype=i32)[None, :]).astype(i32)
    counts = jnp.sum(onehot, axis=0)
    padded = (counts + rb - 1) // rb * rb
    pad_end = jnp.cumsum(padded)
    pad_start = pad_end - padded
    grp_start = jnp.cumsum(counts) - counts
    pos = (jnp.sum(onehot * (pad_start - grp_start)[None, :], axis=1) + sorted_pos).astype(i32)
    n_blocks = -(-A // rb) + n_experts
    blk_start = jnp.arange(n_blocks, dtype=i32) * rb
    blk_e = jnp.minimum(jnp.sum((pad_end[None, :] <= blk_start[:, None]).astype(i32), axis=1), n_experts - 1)
    blk_valid = (blk_start < pad_end[-1]).astype(i32)
    off = (blk_start - pad_start[blk_e])[:, None] + jnp.arange(rb, dtype=i32)[None, :]
    valid = off < counts[blk_e][:, None]
    src = order[jnp.clip(grp_start[blk_e][:, None] + off, 0, A - 1)]
    row_tok = jnp.where(valid, src // TOP_K, 0).astype(i32).reshape(-1)
    row_w = jnp.where(valid, top_w.reshape(-1)[src], 0.0).reshape(-1)
    return row_tok, row_w, blk_e.astype(i32), blk_valid, pos


def _layer(x2, p2, B, S, w_in, w_pool, pool_scale, cmp_k_pe, cmp_k_w1, cmp_k_w2, cmp_v_pe, cmp_v_w1, cmp_v_w2,
           w_out, ln1_g, ln1_b, w_router, b_router, w_up, b_up, w_down, b_down,
           w_ple_gate, b_ple_gate, w_ple, ln2_g, ln2_b, alpha):
    T, D = x2.shape
    pool_w = D // 2
    nsa_w = D - pool_w
    nh = nsa_w // HEAD_DIM
    in_w = w_in.shape[1]
    kvw = (in_w - D - N_GATES * nh) // N_KV_GROUPS_FIELDS
    G = kvw // HEAD_DIM
    hg = nh // G
    n_experts = w_router.shape[1]

    xb = x2.astype(BF16)
    w_in_b = w_in.astype(BF16)
    proj_a = _matmul_cols(xb, w_in_b, 0, D)
    proj_b = _matmul_cols(xb, w_in_b, D, 2 * kvw)
    proj_c = _matmul_cols(xb, w_in_b, D + 2 * kvw, 4 * kvw)
    w_gates = jnp.pad(w_in_b[:, D + 6 * kvw:], ((0, 0), (0, LANES - N_GATES * nh)))
    gates = _matmul_cols(xb, w_gates, 0, LANES)

    y_pool = _pool_mixer(proj_a, w_pool.astype(BF16), pool_scale, B, S, pool_w)

    cos2, sin2 = _rope_tables(jnp.arange(S))
    nc = S // CMP_STRIDE
    ccos, csin = _rope_tables(jnp.arange(nc) * CMP_STRIDE + CMP_BLOCK - 1)
    kvc3 = proj_b.reshape(B, nc, CMP_STRIDE * 2 * kvw)
    kc = _compress(kvc3, 0, cmp_k_pe, cmp_k_w1, cmp_k_w2, ccos, csin, B, G, rope=True)
    vc = _compress(kvc3, 1, cmp_v_pe, cmp_v_w1, cmp_v_w2, ccos, csin, B, G, rope=False)
    q_rot, k_slc, v_slc, k_win, v_win = _rope_prep(proj_a, proj_c, cos2, sin2, B, S, nsa_w, G)
    o_cmp, sel_bias = _cmp_select(q_rot, kc, vc, B, S, G, hg)
    tq = _pick_tile(S, (256, 128))
    o_slc = _selected_attention(q_rot, k_slc, v_slc, sel_bias, B, S, G, hg, tq)
    gates_g = gates[:, :N_GATES * nh].reshape(T, G, N_GATES * hg).transpose(1, 0, 2)
    gates_g = jnp.pad(gates_g, ((0, 0), (0, 0), (0, LANES - N_GATES * hg)))
    y_nsa = _window_attention_gated(q_rot, k_win, v_win, gates_g, o_cmp, o_slc, B, S, G, hg, tq, WINDOW)

    ne_pad = -(-n_experts // LANES) * LANES
    w_router_p = jnp.pad(w_router, ((0, 0), (0, ne_pad - n_experts))).astype(BF16)
    b_router_p = jnp.pad(b_router, (0, ne_pad - n_experts)).reshape(1, ne_pad)
    h, hb, logits = _mix_ln_router(y_pool, y_nsa, w_out.astype(BF16), x2, ln1_g, ln1_b,
                                   w_router_p, b_router_p, alpha)

    row_tok, row_w, blk_e, blk_valid, pos = _routing_tables(logits, n_experts)
    F = w_down.shape[1]
    w_pairs = _pair_split_cast(w_up)
    b_pairs = b_up.reshape(n_experts, F // LANES, LANES, 2).transpose(0, 1, 3, 2).reshape(n_experts, 1, 2 * F)
    act = _moe_up(row_tok, blk_e, blk_valid, h, w_pairs, b_pairs)
    y = _moe_down(blk_e, blk_valid, act, w_down, b_down.reshape(n_experts, 1, D), row_w)

    z = _ple_residual(hb, w_ple_gate.astype(BF16), b_ple_gate, p2, w_ple.astype(BF16), h, alpha)
    return _combine_ln(pos, y, z, ln2_g, ln2_b)


def kernel(x, p, w_in, w_pool, pool_scale, cmp_k_pe, cmp_k_w1, cmp_k_w2, cmp_v_pe, cmp_v_w1, cmp_v_w2, w_out, ln1_g, ln1_b, w_router, b_router, w_up, b_up, w_down, b_down, w_ple_gate, b_ple_gate, w_ple, ln2_g, ln2_b):
    B, S, D = x.shape
    depth = w_in.shape[0]
    alpha = (2 * depth) ** 0.25
    x2 = x.reshape(B * S, D)
    for i in range(depth):
        x2 = _layer(x2, p[i].reshape(B * S, -1), B, S, w_in[i], w_pool[i], pool_scale[i],
                    cmp_k_pe[i], cmp_k_w1[i], cmp_k_w2[i], cmp_v_pe[i], cmp_v_w1[i], cmp_v_w2[i],
                    w_out[i], ln1_g[i], ln1_b[i], w_router[i], b_router[i], w_up[i], b_up[i],
                    w_down[i], b_down[i], w_ple_gate[i], b_ple_gate[i], w_ple[i], ln2_g[i], ln2_b[i], alpha)
    return x2.reshape(B, S, D)
```

```python
import functools
import math

import jax
import jax.numpy as jnp
import numpy as np
from jax import lax
from jax.experimental import pallas as pl
from jax.experimental.pallas import tpu as pltpu

HEAD_DIM = 128
LANES = 128
N_KV_GROUPS_FIELDS = 6
N_GATES = 3
POOL_WINDOWS = (2, 4, 8, 16)
CMP_BLOCK = 32
CMP_STRIDE = 16
SEL_BLOCK = 64
N_SEL = 16
WINDOW = 512
ROPE_THETA = 10000.0
TOP_K = 4
SWIGLU_LIMIT = 7.0
SWIGLU_ALPHA = 1.702
EXPERT_ROW_BLOCK = 256
LN_EPS = 1e-5
NEG_INF = -1e30
FORCE_SCORE = 1e4
LOG2E = 1.4426950408889634
EPI_ROWS = 128
ISSUE_UNROLL = 8
MIB = 1024 * 1024

BF16 = jnp.bfloat16
F32 = jnp.float32
NT_DIMS = (((1,), (1,)), ((), ()))


def _params(sem, vmem_mib=None):
    kw = dict(dimension_semantics=sem)
    if vmem_mib is not None:
        kw["vmem_limit_bytes"] = vmem_mib * MIB
    return pltpu.CompilerParams(**kw)


def _pick_tile(n, candidates):
    for c in candidates:
        if n % c == 0:
            return c
    raise ValueError(f"no tile for {n}")


def _mm_kernel(x_ref, w_ref, o_ref):
    @pl.when(pl.program_id(2) == 0)
    def _():
        o_ref[...] = jnp.zeros_like(o_ref)

    o_ref[...] += jnp.dot(x_ref[...], w_ref[...], preferred_element_type=F32)


def _matmul_cols(x, w, col0, ncols):
    M, K = x.shape
    tm = _pick_tile(M, (1024, 512, 256))
    tk = _pick_tile(K, (2048, 1024, 512, 256, 128))
    tn = _pick_tile(math.gcd(col0, ncols) if col0 else ncols, (1024, 512, 256, 128))
    off = col0 // tn
    return pl.pallas_call(
        _mm_kernel,
        name="in_proj",
        grid=(M // tm, ncols // tn, K // tk),
        in_specs=[pl.BlockSpec((tm, tk), lambda i, j, k: (i, k)),
                  pl.BlockSpec((tk, tn), lambda i, j, k: (k, j + off))],
        out_specs=pl.BlockSpec((tm, tn), lambda i, j, k: (i, j)),
        out_shape=jax.ShapeDtypeStruct((M, ncols), F32),
        compiler_params=_params(("parallel", "parallel", "arbitrary"), 40),
    )(x, w)


def _pool_kernel(u_ref, prev_ref, w_ref, sc_ref, o_ref, ext_ref, *, ts, halo):
    g = pl.program_id(0)
    i = pl.program_id(2)
    cur = u_ref[...]
    ext_ref[pl.ds(halo, ts), :] = cur

    @pl.when(i == 0)
    def _():
        ext_ref[pl.ds(0, halo), :] = jnp.zeros((halo, cur.shape[1]), F32)

    @pl.when(i > 0)
    def _():
        ext_ref[pl.ds(0, halo), :] = prev_ref[...]

    def back(d):
        return ext_ref[pl.ds(halo - d, ts), :]

    s2 = cur + back(1)
    s4 = s2 + back(2) + back(3)
    s8 = s4 + back(4) + back(5) + back(6) + back(7)
    s16 = s8
    for d in range(8, 16):
        s16 = s16 + back(d)
    ssum = jnp.where(g == 0, s2, jnp.where(g == 1, s4, jnp.where(g == 2, s8, s16)))
    win = jnp.left_shift(2, g)
    t = i * ts + lax.broadcasted_iota(jnp.int32, (ts, 1), 0)
    cnt = jnp.minimum(t + 1, win).astype(F32)
    pooled = ssum / cnt - cur
    mixed = jnp.dot(pooled.astype(BF16), w_ref[0], preferred_element_type=F32)
    o_ref[...] = (mixed * sc_ref[...]).astype(o_ref.dtype)


def _pool_mixer(proj_a, w_pool_b, pool_scale, B, S, pool_w):
    T = B * S
    ng = len(POOL_WINDOWS)
    pg = pool_w // ng
    halo = POOL_WINDOWS[-1]
    ts = _pick_tile(S, (512, 256, 128))
    ns = S // ts
    hb = ts // halo
    kern = functools.partial(_pool_kernel, ts=ts, halo=halo)
    return pl.pallas_call(
        kern,
        name="pool_mixer",
        grid=(ng, B, ns),
        in_specs=[
            pl.BlockSpec((ts, pg), lambda g, b, i: (b * ns + i, g)),
            pl.BlockSpec((halo, pg), lambda g, b, i: (jnp.maximum((b * ns + i) * hb - 1, 0), g)),
            pl.BlockSpec((1, pg, pg), lambda g, b, i: (g, 0, 0)),
            pl.BlockSpec((1, pg), lambda g, b, i: (0, g)),
        ],
        out_specs=pl.BlockSpec((ts, pg), lambda g, b, i: (b * ns + i, g)),
        out_shape=jax.ShapeDtypeStruct((T, pool_w), BF16),
        scratch_shapes=[pltpu.VMEM((ts + halo, pg), F32)],
        compiler_params=_params(("parallel", "parallel", "arbitrary"), 32),
    )(proj_a, proj_a, w_pool_b, pool_scale.reshape(1, pool_w))


def _rope_tables(pos):
    inv = ROPE_THETA ** (-jnp.arange(0, HEAD_DIM, 2, dtype=F32) / HEAD_DIM)
    ang = pos.astype(F32)[:, None] * inv[None, :]
    c, s = jnp.cos(ang), jnp.sin(ang)
    return jnp.concatenate([c, c], axis=-1), jnp.concatenate([-s, s], axis=-1)


def _rope(x, cos2, sin2):
    return x * cos2 + pltpu.roll(x, HEAD_DIM // 2, 1) * sin2


def _compress_kernel(x_ref, pe_ref, w1_ref, w2_ref, cos_ref, sin_ref, o_ref, *, rope, nl):
    nc = x_ref.shape[0] // nl
    xs = [x_ref[pl.ds(l, nc, stride=nl), :] for l in range(nl)]
    lo = jnp.concatenate([(xs[l] + pe_ref[l:l + 1, :]).astype(BF16) for l in range(nl)], axis=1)
    hi = jnp.concatenate([(xs[l] + pe_ref[nl + l:nl + l + 1, :]).astype(BF16) for l in range(nl)], axis=1)
    half = nl * HEAD_DIM
    a = jnp.dot(lo, w1_ref[0:half, :], preferred_element_type=F32)
    b = jnp.dot(hi, w1_ref[half:2 * half, :], preferred_element_type=F32)
    hid = jax.nn.gelu(a + pltpu.roll(b, nc - 1, 0))
    out = jnp.dot(hid.astype(BF16), w2_ref[...], preferred_element_type=F32)
    if rope:
        out = _rope(out, cos_ref[...], sin_ref[...])
    o_ref[0, 0] = out.astype(o_ref.dtype)


def _compress(proj_b, field, pe, w1, w2, cos2, sin2, B, S, G, rope):
    nl = CMP_STRIDE
    nc = S // nl
    kern = functools.partial(_compress_kernel, rope=rope, nl=nl)
    const = lambda b, g: (0, 0)
    return pl.pallas_call(
        kern,
        name="compress_k" if rope else "compress_v",
        grid=(B, G),
        in_specs=[
            pl.BlockSpec((S, HEAD_DIM), lambda b, g: (b, field * G + g)),
            pl.BlockSpec((CMP_BLOCK, HEAD_DIM), const),
            pl.BlockSpec((CMP_BLOCK * HEAD_DIM, HEAD_DIM), const),
            pl.BlockSpec((HEAD_DIM, HEAD_DIM), const),
            pl.BlockSpec((nc, HEAD_DIM), const),
            pl.BlockSpec((nc, HEAD_DIM), const),
        ],
        out_specs=pl.BlockSpec((1, 1, nc, HEAD_DIM), lambda b, g: (b, g, 0, 0)),
        out_shape=jax.ShapeDtypeStruct((B, G, nc, HEAD_DIM), BF16),
        compiler_params=_params(("parallel", "parallel"), 32),
    )(proj_b, pe, w1.reshape(CMP_BLOCK * HEAD_DIM, HEAD_DIM).astype(BF16), w2.astype(BF16), cos2, sin2)


def _rope_prep_kernel(q_ref, kv_ref, cos_ref, sin_ref, qo_ref, ks_ref, vs_ref, kw_ref, vw_ref, *, nh, G, ts):
    i = pl.program_id(1)
    c = cos_ref[...]
    s = sin_ref[...]
    kvw = G * HEAD_DIM
    for h in range(nh):
        qo_ref[0, h] = _rope(q_ref[:, h * HEAD_DIM:(h + 1) * HEAD_DIM], c, s).astype(BF16)
    blk = (i * ts + lax.broadcasted_iota(jnp.int32, (ts, LANES), 0)) // SEL_BLOCK
    onehot = (blk == lax.broadcasted_iota(jnp.int32, (ts, LANES), 1)).astype(BF16)
    for g in range(G):
        lo = g * HEAD_DIM
        ks_ref[0, g, :, 0:HEAD_DIM] = _rope(kv_ref[:, lo:lo + HEAD_DIM], c, s).astype(BF16)
        ks_ref[0, g, :, HEAD_DIM:2 * HEAD_DIM] = onehot
        vs_ref[0, g] = kv_ref[:, kvw + lo:kvw + lo + HEAD_DIM].astype(BF16)
        kw_ref[0, g] = _rope(kv_ref[:, 2 * kvw + lo:2 * kvw + lo + HEAD_DIM], c, s).astype(BF16)
        vw_ref[0, g] = kv_ref[:, 3 * kvw + lo:3 * kvw + lo + HEAD_DIM].astype(BF16)


def _rope_prep(proj_a, proj_c, cos2, sin2, B, S, nsa_w, G):
    nh = nsa_w // HEAD_DIM
    ts = _pick_tile(S, (256, 128))
    ns = S // ts
    kern = functools.partial(_rope_prep_kernel, nh=nh, G=G, ts=ts)
    hm = lambda b, i: (b, 0, i, 0)
    return pl.pallas_call(
        kern,
        name="rope_prep",
        grid=(B, ns),
        in_specs=[
            pl.BlockSpec((ts, nsa_w), lambda b, i: (b * ns + i, 1)),
            pl.BlockSpec((ts, 4 * G * HEAD_DIM), lambda b, i: (b * ns + i, 0)),
            pl.BlockSpec((ts, HEAD_DIM), lambda b, i: (i, 0)),
            pl.BlockSpec((ts, HEAD_DIM), lambda b, i: (i, 0)),
        ],
        out_specs=[
            pl.BlockSpec((1, nh, ts, HEAD_DIM), hm),
            pl.BlockSpec((1, G, ts, 2 * HEAD_DIM), hm),
            pl.BlockSpec((1, G, ts, HEAD_DIM), hm),
            pl.BlockSpec((1, G, ts, HEAD_DIM), hm),
            pl.BlockSpec((1, G, ts, HEAD_DIM), hm),
        ],
        out_shape=[
            jax.ShapeDtypeStruct((B, nh, S, HEAD_DIM), BF16),
            jax.ShapeDtypeStruct((B, G, S, 2 * HEAD_DIM), BF16),
            jax.ShapeDtypeStruct((B, G, S, HEAD_DIM), BF16),
            jax.ShapeDtypeStruct((B, G, S, HEAD_DIM), BF16),
            jax.ShapeDtypeStruct((B, G, S, HEAD_DIM), BF16),
        ],
        compiler_params=_params(("parallel", "parallel"), 40),
    )(proj_a, proj_c, cos2, sin2)


def _cmp_select_kernel(q_ref, kc_ref, vc_ref, ovl_ref, o_ref, bias_ref, *, hg, tq, n_sel, n_keep, scale):
    i = pl.program_id(2)
    nc = kc_ref.shape[2]
    kc = kc_ref[0, 0]
    vc = vc_ref[0, 0]
    t = i * tq + lax.broadcasted_iota(jnp.int32, (tq, nc), 0)
    n = lax.broadcasted_iota(jnp.int32, (tq, nc), 1)
    valid = (n * CMP_STRIDE + (CMP_BLOCK - 1) <= t) & (n < nc - 1)
    ovl = ovl_ref[...]
    imp = jnp.zeros((n_sel, tq), F32)
    for h in range(hg):
        s = lax.dot_general(q_ref[0, h], kc, NT_DIMS, preferred_element_type=F32) * scale
        s = jnp.where(valid, s, NEG_INF)
        m = jnp.max(s, axis=-1, keepdims=True)
        e = jnp.where(valid, jnp.exp(s - m), 0.0)
        den = jnp.sum(e, axis=-1, keepdims=True)
        pc = (e / jnp.where(den > 0.0, den, 1.0)).astype(BF16)
        o_ref[0, h] = jnp.dot(pc, vc, preferred_element_type=F32)
        imp = imp + lax.dot_general(ovl, pc, NT_DIMS, preferred_element_type=F32)
    j = lax.broadcasted_iota(jnp.int32, (n_sel, tq), 0)
    tb = (i * tq + lax.broadcasted_iota(jnp.int32, (n_sel, tq), 1)) // SEL_BLOCK
    forced = (j == 0) | (j == tb) | (j == tb - 1)
    score = jnp.where(forced, FORCE_SCORE, jnp.where(j <= tb, imp, -1.0))
    rank = jnp.zeros((n_sel, tq), jnp.int32)
    for jp in range(n_sel):
        row = score[jp:jp + 1, :]
        ahead = (row > score) | ((row == score) & (jp < j))
        rank = rank + ahead.astype(jnp.int32)
    bias_t = jnp.where(rank < n_keep, 0.0, NEG_INF)
    if n_sel < LANES:
        bias_t = jnp.concatenate([bias_t, jnp.zeros((LANES - n_sel, tq), F32)], axis=0)
    bias_ref[0, 0] = bias_t.T.astype(BF16)


def _overlap_t(n_cmp_pad, n_sel):
    n_cmp = n_cmp_pad - 1
    cs = np.arange(n_cmp) * CMP_STRIDE
    ce = cs + CMP_BLOCK
    ss = np.arange(n_sel) * SEL_BLOCK
    se = ss + SEL_BLOCK
    ov = np.clip(np.minimum(ce[:, None], se[None, :]) - np.maximum(cs[:, None], ss[None, :]), 0, None)
    out = np.zeros((n_sel, n_cmp_pad), np.float32)
    out[:, :n_cmp] = (ov / CMP_BLOCK).T
    return jnp.asarray(out, dtype=BF16)


def _cmp_select(q_rot, kc, vc, B, S, G, hg):
    nc = kc.shape[2]
    n_sel = S // SEL_BLOCK
    assert n_sel <= LANES and n_sel % 8 == 0
    tq = _pick_tile(S, (256, 128))
    nq = S // tq
    nh = G * hg
    kern = functools.partial(_cmp_select_kernel, hg=hg, tq=tq, n_sel=n_sel,
                             n_keep=min(N_SEL, n_sel), scale=HEAD_DIM ** -0.5)
    return pl.pallas_call(
        kern,
        name="cmp_attn_select",
        grid=(B, G, nq),
        in_specs=[
            pl.BlockSpec((1, hg, tq, HEAD_DIM), lambda b, g, i: (b, g, i, 0)),
            pl.BlockSpec((1, 1, nc, HEAD_DIM), lambda b, g, i: (b, g, 0, 0)),
            pl.BlockSpec((1, 1, nc, HEAD_DIM), lambda b, g, i: (b, g, 0, 0)),
            pl.BlockSpec((n_sel, nc), lambda b, g, i: (0, 0)),
        ],
        out_specs=[
            pl.BlockSpec((1, hg, tq, HEAD_DIM), lambda b, g, i: (b, g, i, 0)),
            pl.BlockSpec((1, 1, tq, LANES), lambda b, g, i: (b, g, i, 0)),
        ],
        out_shape=[
            jax.ShapeDtypeStruct((B, nh, S, HEAD_DIM), F32),
            jax.ShapeDtypeStruct((B, G, S, LANES), BF16),
        ],
        compiler_params=_params(("parallel", "parallel", "parallel"), 40),
    )(q_rot, kc, vc, _overlap_t(nc, n_sel))


def _lane_fold(x, op):
    r = x[:, :LANES]
    for a in range(1, x.shape[1] // LANES):
        r = op(r, x[:, a * LANES:(a + 1) * LANES])
    return r


def _slc_kernel(q_ref, bias_ref, k_ref, v_ref, o_ref, s_ref, m_ref, l_ref, acc_ref, *, hg, tq, scale):
    i = pl.program_id(2)
    rows = hg * tq
    c = scale * LOG2E
    q = q_ref[0].reshape(rows, HEAD_DIM)
    b = bias_ref[0, 0]
    qa = jnp.concatenate([q, jnp.concatenate([b] * hg, axis=0)], axis=1)

    def k_tile(j):
        return k_ref[0, 0, pl.ds(pl.multiple_of(j * tq, tq), tq), :]

    def v_tile(j):
        return v_ref[0, 0, pl.ds(pl.multiple_of(j * tq, tq), tq), :]

    m_ref[...] = jnp.full_like(m_ref, NEG_INF)

    def scores(j, carry):
        s = lax.dot_general(qa, k_tile(j), NT_DIMS, preferred_element_type=F32)
        s_ref[j] = s
        m_ref[...] = jnp.maximum(m_ref[...], _lane_fold(s, jnp.maximum))
        return carry

    lax.fori_loop(0, i, scores, 0)
    sd = lax.dot_general(qa, k_tile(i), NT_DIMS, preferred_element_type=F32)
    tpos = lax.broadcasted_iota(jnp.int32, (rows, tq), 0) & (tq - 1)
    kpos = lax.broadcasted_iota(jnp.int32, (rows, tq), 1)
    sd = jnp.where(kpos <= tpos, sd, NEG_INF)
    mrow = jnp.max(jnp.maximum(m_ref[...], _lane_fold(sd, jnp.maximum)), axis=-1, keepdims=True) * c
    m_ref[...] = jnp.broadcast_to(mrow, (rows, LANES))
    l_ref[...] = jnp.zeros_like(l_ref)
    acc_ref[...] = jnp.zeros_like(acc_ref)

    def probs(s):
        mfull = jnp.concatenate([m_ref[...]] * (tq // LANES), axis=1)
        return jnp.exp2(s * c - mfull)

    def accumulate(j, carry):
        p = probs(s_ref[j])
        l_ref[...] += _lane_fold(p, jnp.add)
        acc_ref[...] += jnp.dot(p.astype(BF16), v_tile(j), preferred_element_type=F32)
        return carry

    lax.fori_loop(0, i, accumulate, 0)
    p = probs(sd)
    den = jnp.sum(l_ref[...] + _lane_fold(p, jnp.add), axis=-1, keepdims=True)
    acc = acc_ref[...] + jnp.dot(p.astype(BF16), v_tile(i), preferred_element_type=F32)
    o_ref[0] = (acc / den).reshape(hg, tq, HEAD_DIM)


def _selected_attention(q_rot, k_aug, v, bias, B, S, G, hg, tq):
    assert tq & (tq - 1) == 0 and tq % LANES == 0
    nq = S // tq
    rows = hg * tq
    kern = functools.partial(_slc_kernel, hg=hg, tq=tq, scale=HEAD_DIM ** -0.5)
    return pl.pallas_call(
        kern,
        name="selected_attn",
        grid=(B, G, nq),
        in_specs=[
            pl.BlockSpec((1, hg, tq, HEAD_DIM), lambda b, g, i: (b, g, i, 0)),
            pl.BlockSpec((1, 1, tq, LANES), lambda b, g, i: (b, g, i, 0)),
            pl.BlockSpec((1, 1, S, 2 * HEAD_DIM), lambda b, g, i: (b, g, 0, 0)),
            pl.BlockSpec((1, 1, S, HEAD_DIM), lambda b, g, i: (b, g, 0, 0)),
        ],
        out_specs=pl.BlockSpec((1, hg, tq, HEAD_DIM), lambda b, g, i: (b, g, i, 0)),
        out_shape=jax.ShapeDtypeStruct((B, G * hg, S, HEAD_DIM), F32),
        scratch_shapes=[pltpu.VMEM((nq, rows, tq), F32), pltpu.VMEM((rows, LANES), F32),
                        pltpu.VMEM((rows, LANES), F32), pltpu.VMEM((rows, HEAD_DIM), F32)],
        compiler_params=_params(("parallel", "parallel", "arbitrary"), 48),
    )(q_rot, bias, k_aug, v)


def _win_kernel(q_ref, g_ref, oc_ref, os_ref, *refs, hg, tq, nprev, window, scale):
    nt = nprev + 1
    k_refs, v_refs, y_ref = refs[:nt], refs[nt:2 * nt], refs[2 * nt]
    i = pl.program_id(2)
    gate = jax.nn.sigmoid(g_ref[0])
    c = scale * LOG2E
    nk = nt * tq
    kcat = jnp.concatenate([r[0, 0] for r in k_refs], axis=0)
    vcat = jnp.concatenate([r[0, 0] for r in v_refs], axis=0)
    tpos = i * tq + lax.broadcasted_iota(jnp.int32, (tq, nk), 0)
    kpos = (i - nprev) * tq + lax.broadcasted_iota(jnp.int32, (tq, nk), 1)
    mask = (kpos <= tpos) & (kpos > tpos - window) & (kpos >= 0)
    for h in range(hg):
        s = lax.dot_general(q_ref[0, h], kcat, NT_DIMS, preferred_element_type=F32)
        s = jnp.where(mask, s, NEG_INF)
        m = jnp.max(s, axis=-1, keepdims=True)
        p = jnp.exp2((s - m) * c)
        den = jnp.sum(p, axis=-1, keepdims=True)
        o_win = jnp.dot(p.astype(BF16), vcat, preferred_element_type=F32) / den
        gc = N_GATES * h
        y = (gate[:, gc:gc + 1] * oc_ref[0, h] + gate[:, gc + 1:gc + 2] * os_ref[0, h]
             + gate[:, gc + 2:gc + 3] * o_win)
        y_ref[:, h * HEAD_DIM:(h + 1) * HEAD_DIM] = y.astype(y_ref.dtype)


def _window_attention_gated(q_rot, k, v, gates_g, o_cmp, o_slc, B, S, G, hg, tq, window):
    assert window % tq == 0
    nprev = window // tq
    nq = S // tq
    kern = functools.partial(_win_kernel, hg=hg, tq=tq, nprev=nprev, window=window, scale=HEAD_DIM ** -0.5)

    def kv_spec(a):
        return pl.BlockSpec((1, 1, tq, HEAD_DIM), lambda b, g, i: (b, g, jnp.maximum(i - nprev + a, 0), 0))

    kv_specs = [kv_spec(a) for a in range(nprev + 1)]
    heads = pl.BlockSpec((1, hg, tq, HEAD_DIM), lambda b, g, i: (b, g, i, 0))
    return pl.pallas_call(
        kern,
        name="window_attn_gated_sum",
        grid=(B, G, nq),
        in_specs=[heads, pl.BlockSpec((1, tq, LANES), lambda b, g, i: (g, b * nq + i, 0)), heads, heads]
        + kv_specs + kv_specs,
        out_specs=pl.BlockSpec((tq, hg * HEAD_DIM), lambda b, g, i: (b * nq + i, g)),
        out_shape=jax.ShapeDtypeStruct((B * S, G * hg * HEAD_DIM), BF16),
        compiler_params=_params(("parallel", "parallel", "arbitrary"), 40),
    )(q_rot, gates_g, o_cmp, o_slc, *([k] * (nprev + 1)), *([v] * (nprev + 1)))


def _layer_norm(z, g, b):
    mu = jnp.mean(z, axis=-1, keepdims=True)
    zc = z - mu
    var = jnp.mean(zc * zc, axis=-1, keepdims=True)
    return zc * lax.rsqrt(var + LN_EPS) * g + b


def _mix_kernel(yp_ref, yn_ref, w_ref, x_ref, g_ref, b_ref, wr_ref, br_ref, h_ref, hb_ref, lg_ref, acc_ref,
                *, nkp, alpha):
    k = pl.program_id(1)

    @pl.when(k == 0)
    def _():
        acc_ref[...] = jnp.zeros_like(acc_ref)

    @pl.when(k < nkp)
    def _():
        acc_ref[...] += jnp.dot(yp_ref[...], w_ref[...], preferred_element_type=F32)

    @pl.when(k >= nkp)
    def _():
        acc_ref[...] += jnp.dot(yn_ref[...], w_ref[...], preferred_element_type=F32)

    @pl.when(k == pl.num_programs(1) - 1)
    def _():
        def epilogue(r, carry):
            rows = pl.ds(pl.multiple_of(r * EPI_ROWS, EPI_ROWS), EPI_ROWS)
            h = _layer_norm(alpha * x_ref[rows, :] + acc_ref[rows, :], g_ref[...], b_ref[...])
            hb = h.astype(BF16)
            h_ref[rows, :] = h
            hb_ref[rows, :] = hb
            lg_ref[rows, :] = jnp.dot(hb, wr_ref[...], preferred_element_type=F32) + br_ref[...]
            return carry

        lax.fori_loop(0, acc_ref.shape[0] // EPI_ROWS, epilogue, 0)


def _mix_ln_router(y_pool, y_nsa, w_out_b, x2, ln_g, ln_b, w_router_p, b_router_p, alpha):
    T, D = x2.shape
    kp = y_pool.shape[1]
    tm = _pick_tile(T, (512, 256, 128))
    tk = _pick_tile(math.gcd(kp, y_nsa.shape[1]), (512, 256, 128))
    nkp = kp // tk
    nk = nkp + y_nsa.shape[1] // tk
    ne = w_router_p.shape[1]
    kern = functools.partial(_mix_kernel, nkp=nkp, alpha=alpha)
    row = lambda i, k: (i, 0)
    const = lambda i, k: (0, 0)
    return pl.pallas_call(
        kern,
        name="out_proj_ln1_router",
        grid=(T // tm, nk),
        in_specs=[
            pl.BlockSpec((tm, tk), lambda i, k: (i, jnp.minimum(k, nkp - 1))),
            pl.BlockSpec((tm, tk), lambda i, k: (i, jnp.maximum(k - nkp, 0))),
            pl.BlockSpec((tk, D), lambda i, k: (k, 0)),
            pl.BlockSpec((tm, D), row, pipeline_mode=pl.Buffered(1)),
            pl.BlockSpec((1, D), const),
            pl.BlockSpec((1, D), const),
            pl.BlockSpec((D, ne), const),
            pl.BlockSpec((1, ne), const),
        ],
        out_specs=[pl.BlockSpec((tm, D), row), pl.BlockSpec((tm, D), row), pl.BlockSpec((tm, ne), row)],
        out_shape=[jax.ShapeDtypeStruct((T, D), F32), jax.ShapeDtypeStruct((T, D), BF16),
                   jax.ShapeDtypeStruct((T, ne), F32)],
        scratch_shapes=[pltpu.VMEM((tm, D), F32)],
        compiler_params=_params(("parallel", "arbitrary"), 56),
    )(y_pool, y_nsa, w_out_b, x2, ln_g.reshape(1, D), ln_b.reshape(1, D), w_router_p, b_router_p)


def _row_copy(src_hbm, row, dst_ref, slot, sem):
    return pltpu.make_async_copy(src_hbm.at[pl.ds(row, 1)], dst_ref.at[pl.ds(slot, 1)], sem)


def _rows_copy(src_hbm, dst_ref, sem, n):
    return pltpu.make_async_copy(src_hbm.at[pl.ds(0, n)], dst_ref, sem)


PAIR = 2 * LANES


def _pair_split_kernel(w_ref, perm_ref, o_ref):
    perm = perm_ref[...]
    for c in range(w_ref.shape[2] // PAIR):
        sl = slice(c * PAIR, (c + 1) * PAIR)
        o_ref[0, :, sl] = jnp.dot(w_ref[0, :, sl].astype(BF16), perm, preferred_element_type=F32).astype(BF16)


def _pair_split_cast(w_up):
    E, D, F2 = w_up.shape
    assert F2 % PAIR == 0
    td = _pick_tile(D, (512, 256, 128))
    perm = np.zeros((PAIR, PAIR), np.float32)
    perm[2 * np.arange(LANES), np.arange(LANES)] = 1.0
    perm[2 * np.arange(LANES) + 1, LANES + np.arange(LANES)] = 1.0
    return pl.pallas_call(
        _pair_split_kernel,
        name="w_up_pair_split",
        grid=(E, D // td),
        in_specs=[pl.BlockSpec((1, td, F2), lambda e, i: (e, i, 0)),
                  pl.BlockSpec((PAIR, PAIR), lambda e, i: (0, 0))],
        out_specs=pl.BlockSpec((1, td, F2), lambda e, i: (e, i, 0)),
        out_shape=jax.ShapeDtypeStruct((E, D, F2), BF16),
        compiler_params=_params(("parallel", "parallel"), 32),
    )(w_up, jnp.asarray(perm, dtype=BF16))


def _moe_up_kernel(tok_ref, be_ref, bv_ref, h_hbm, w_ref, b_ref, act_ref, xbuf_ref, sem, *, rb):
    i = pl.program_id(0)
    nb = pl.num_programs(0)

    def gather(blk, slot):
        def issue(r, carry):
            _row_copy(h_hbm, tok_ref[blk * rb + r], xbuf_ref.at[slot], r, sem.at[slot]).start()
            return carry

        lax.fori_loop(0, rb, issue, 0, unroll=ISSUE_UNROLL)

    @pl.when((i == 0) & (bv_ref[0] != 0))
    def _():
        gather(0, 0)

    slot = i % 2
    valid = bv_ref[i] != 0

    @pl.when(jnp.logical_not(valid))
    def _():
        act_ref[...] = jnp.zeros_like(act_ref)

    @pl.when(jnp.logical_not(valid) & (i > 0) & (bv_ref[jnp.maximum(i - 1, 0)] != 0))
    def _():
        _rows_copy(h_hbm, xbuf_ref.at[slot], sem.at[slot], rb).wait()

    @pl.when(valid)
    def _():
        _rows_copy(h_hbm, xbuf_ref.at[slot], sem.at[slot], rb).wait()
        xb = xbuf_ref[slot].astype(BF16)
        nxt = jnp.minimum(i + 1, nb - 1)
        for r in range(rb):
            _row_copy(h_hbm, tok_ref[nxt * rb + r], xbuf_ref.at[1 - slot], r, sem.at[1 - slot]).start()
        hcat = jnp.dot(xb, w_ref[0], preferred_element_type=F32) + b_ref[0]
        for c in range(hcat.shape[1] // PAIR):
            gate = jnp.minimum(hcat[:, c * PAIR:c * PAIR + LANES], SWIGLU_LIMIT)
            up = jnp.clip(hcat[:, c * PAIR + LANES:(c + 1) * PAIR], -SWIGLU_LIMIT, SWIGLU_LIMIT)
            act = gate * jax.nn.sigmoid(SWIGLU_ALPHA * gate) * (up + 1.0)
            act_ref[:, c * LANES:(c + 1) * LANES] = act.astype(act_ref.dtype)

    @pl.when(valid & (i == nb - 1))
    def _():
        _rows_copy(h_hbm, xbuf_ref.at[1 - slot], sem.at[1 - slot], rb).wait()


def _moe_up(row_tok, blk_e, blk_valid, h, w_pairs, b_pairs):
    n_rows = row_tok.shape[0]
    rb = EXPERT_ROW_BLOCK
    E, D, F2 = w_pairs.shape
    wmap = lambda i, tok, be, bv: (be[i], 0, 0)
    return pl.pallas_call(
        functools.partial(_moe_up_kernel, rb=rb),
        name="moe_gather_up_swiglu",
        grid_spec=pltpu.PrefetchScalarGridSpec(
            num_scalar_prefetch=3,
            grid=(n_rows // rb,),
            in_specs=[
                pl.BlockSpec(memory_space=pl.ANY),
                pl.BlockSpec((1, D, F2), wmap),
                pl.BlockSpec((1, 1, F2), wmap),
            ],
            out_specs=pl.BlockSpec((rb, F2 // 2), lambda i, tok, be, bv: (i, 0)),
            scratch_shapes=[pltpu.VMEM((2, rb, D), F32), pltpu.SemaphoreType.DMA((2,))],
        ),
        out_shape=jax.ShapeDtypeStruct((n_rows, F2 // 2), BF16),
        compiler_params=_params(("arbitrary",), 56),
    )(row_tok, blk_e, blk_valid, h, w_pairs, b_pairs)


def _moe_down_kernel(be_ref, bv_ref, act_ref, wd_ref, bd_ref, rw_ref, y_ref, wb_ref):
    i = pl.program_id(0)

    @pl.when(bv_ref[i] == 0)
    def _():
        y_ref[...] = jnp.zeros_like(y_ref)

    @pl.when((bv_ref[i] != 0) & ((i == 0) | (be_ref[i] != be_ref[jnp.maximum(i - 1, 0)])))
    def _():
        wb_ref[...] = wd_ref[0].astype(BF16)

    @pl.when(bv_ref[i] != 0)
    def _():
        y = (jnp.dot(act_ref[...], wb_ref[...], preferred_element_type=F32) + bd_ref[0]) * rw_ref[...]
        bits = pltpu.bitcast(y.astype(BF16).astype(F32), jnp.uint32)
        half = bits.shape[1] // 2
        y_ref[...] = (bits[:, :half] >> 16) | (bits[:, half:] & jnp.uint32(0xFFFF0000))


def _moe_down(blk_e, blk_valid, act, wd, bd, row_w):
    n_rows, F = act.shape
    rb = EXPERT_ROW_BLOCK
    E, _, D = wd.shape
    wmap = lambda i, be, bv: (be[i], 0, 0)
    return pl.pallas_call(
        _moe_down_kernel,
        name="moe_down",
        grid_spec=pltpu.PrefetchScalarGridSpec(
            num_scalar_prefetch=2,
            grid=(n_rows // rb,),
            in_specs=[
                pl.BlockSpec((rb, F), lambda i, be, bv: (i, 0)),
                pl.BlockSpec((1, F, D), wmap),
                pl.BlockSpec((1, 1, D), wmap),
                pl.BlockSpec((rb, 1), lambda i, be, bv: (i, 0)),
            ],
            out_specs=pl.BlockSpec((rb, D // 2), lambda i, be, bv: (i, 0)),
            scratch_shapes=[pltpu.VMEM((F, D), BF16)],
        ),
        out_shape=jax.ShapeDtypeStruct((n_rows, D // 2), jnp.uint32),
        compiler_params=_params(("arbitrary",), 56),
    )(blk_e, blk_valid, act, wd, bd, row_w.reshape(n_rows, 1))


def _ple_kernel(hb_ref, w_ref, bg_ref, p_ref, wp_ref, h_ref, z_ref, *, alpha):
    k = pl.program_id(2)

    @pl.when(k == 0)
    def _():
        z_ref[...] = jnp.zeros_like(z_ref)

    z_ref[...] += jnp.dot(hb_ref[...], w_ref[...], preferred_element_type=F32)

    @pl.when(k == pl.num_programs(2) - 1)
    def _():
        gate = jax.nn.sigmoid(z_ref[...] + bg_ref[...])
        emb = jnp.dot(p_ref[...].astype(BF16), wp_ref[...], preferred_element_type=F32)
        z_ref[...] = alpha * h_ref[...] + gate * emb


def _ple_residual(hb, w_gate_b, b_gate, p2, w_ple_b, h, alpha):
    T, D = h.shape
    pd = p2.shape[1]
    tm = _pick_tile(T, (1024, 512, 256, 128))
    tn = _pick_tile(D, (1024, 512, 256, 128))
    tk = _pick_tile(D, (2048, 1024, 512, 256, 128))
    return pl.pallas_call(
        functools.partial(_ple_kernel, alpha=alpha),
        name="ple_gate_residual",
        grid=(T // tm, D // tn, D // tk),
        in_specs=[
            pl.BlockSpec((tm, tk), lambda i, j, k: (i, k)),
            pl.BlockSpec((tk, tn), lambda i, j, k: (k, j)),
            pl.BlockSpec((1, tn), lambda i, j, k: (0, j)),
            pl.BlockSpec((tm, pd), lambda i, j, k: (i, 0)),
            pl.BlockSpec((pd, tn), lambda i, j, k: (0, j)),
            pl.BlockSpec((tm, tn), lambda i, j, k: (i, j)),
        ],
        out_specs=pl.BlockSpec((tm, tn), lambda i, j, k: (i, j)),
        out_shape=jax.ShapeDtypeStruct((T, D), F32),
        compiler_params=_params(("parallel", "parallel", "arbitrary"), 48),
    )(hb, w_gate_b, b_gate.reshape(1, D), p2, w_ple_b, h)


def _combine_kernel(pos_ref, y_hbm, z_ref, g_ref, b_ref, o_ref, buf_ref, sem, *, tc, topk):
    i = pl.program_id(0)
    nb = pl.num_programs(0)

    def gather(blk, slot):
        def issue(t, carry):
            for k in range(topk):
                _row_copy(y_hbm, pos_ref[(blk * tc + t) * topk + k], buf_ref.at[slot, k], t, sem.at[slot]).start()
            return carry

        lax.fori_loop(0, tc, issue, 0, unroll=ISSUE_UNROLL // topk)

    @pl.when(i == 0)
    def _():
        gather(0, 0)

    slot = i % 2
    for k in range(topk):
        _rows_copy(y_hbm, buf_ref.at[slot, k], sem.at[slot], tc).wait()
    nxt = jnp.minimum(i + 1, nb - 1)
    for t in range(tc):
        for k in range(topk):
            _row_copy(y_hbm, pos_ref[(nxt * tc + t) * topk + k], buf_ref.at[1 - slot, k], t, sem.at[1 - slot]).start()
    lo = hi = None
    for k in range(topk):
        word = buf_ref[slot, k]
        lo_k = pltpu.bitcast(word << 16, F32)
        hi_k = pltpu.bitcast(word & jnp.uint32(0xFFFF0000), F32)
        lo = lo_k if lo is None else lo + lo_k
        hi = hi_k if hi is None else hi + hi_k
    ffn = jnp.concatenate([lo, hi], axis=1)
    o_ref[...] = _layer_norm(z_ref[...] + ffn, g_ref[...], b_ref[...])

    @pl.when(i == nb - 1)
    def _():
        for k in range(topk):
            _rows_copy(y_hbm, buf_ref.at[1 - slot, k], sem.at[1 - slot], tc).wait()


def _combine_ln(pos, y, z, ln_g, ln_b):
    T, D = z.shape
    tc = _pick_tile(T, (128, 64, 32))
    row = lambda i, pos: (i, 0)
    const = lambda i, pos: (0, 0)
    return pl.pallas_call(
        functools.partial(_combine_kernel, tc=tc, topk=TOP_K),
        name="moe_combine_ln2",
        grid_spec=pltpu.PrefetchScalarGridSpec(
            num_scalar_prefetch=1,
            grid=(T // tc,),
            in_specs=[
                pl.BlockSpec(memory_space=pl.ANY),
                pl.BlockSpec((tc, D), row),
                pl.BlockSpec((1, D), const),
                pl.BlockSpec((1, D), const),
            ],
            out_specs=pl.BlockSpec((tc, D), row),
            scratch_shapes=[pltpu.VMEM((2, TOP_K, tc, D // 2), jnp.uint32), pltpu.SemaphoreType.DMA((2,))],
        ),
        out_shape=jax.ShapeDtypeStruct((T, D), F32),
        compiler_params=_params(("arbitrary",), 32),
    )(pos, y, z, ln_g.reshape(1, D), ln_b.reshape(1, D))


def _routing_tables(logits, n_experts):
    T = logits.shape[0]
    A = T * TOP_K
    rb = EXPERT_ROW_BLOCK
    i32 = jnp.int32
    top_logit, top_e = lax.top_k(logits[:, :n_experts], TOP_K)
    top_w = jax.nn.softmax(top_logit, axis=-1)
    flat_e = top_e.reshape(-1).astype(i32)
    order = jnp.argsort(flat_e).astype(i32)
    sorted_pos = jnp.argsort(order).astype(i32)
    onehot = (flat_e[:, None] == jnp.arange(n_experts, dtype=i32)[None, :]).astype(i32)
    counts = jnp.sum(onehot, axis=0)
    padded = (counts + rb - 1) // rb * rb
    pad_end = jnp.cumsum(padded)
    pad_start = pad_end - padded
    grp_start = jnp.cumsum(counts) - counts
    pos = (jnp.sum(onehot * (pad_start - grp_start)[None, :], axis=1) + sorted_pos).astype(i32)
    n_blocks = -(-A // rb) + n_experts
    blk_start = jnp.arange(n_blocks, dtype=i32) * rb
    blk_e = jnp.minimum(jnp.sum((pad_end[None, :] <= blk_start[:, None]).astype(i32), axis=1), n_experts - 1)
    blk_valid = (blk_start < pad_end[-1]).astype(i32)
    off = (blk_start - pad_start[blk_e])[:, None] + jnp.arange(rb, dtype=i32)[None, :]
    valid = off < counts[blk_e][:, None]
    src = order[jnp.clip(grp_start[blk_e][:, None] + off, 0, A - 1)]
    row_tok = jnp.where(valid, src // TOP_K, 0).astype(i32).reshape(-1)
    row_w = jnp.where(valid, top_w.reshape(-1)[src], 0.0).reshape(-1)
    return row_tok, row_w, blk_e.astype(i32), blk_valid, pos


def _layer(x2, p2, B, S, w_in, w_pool, pool_scale, cmp_k_pe, cmp_k_w1, cmp_k_w2, cmp_v_pe, cmp_v_w1, cmp_v_w2,
           w_out, ln1_g, ln1_b, w_router, b_router, w_up, b_up, w_down, b_down,
           w_ple_gate, b_ple_gate, w_ple, ln2_g, ln2_b, alpha):
    T, D = x2.shape
    pool_w = D // 2
    nsa_w = D - pool_w
    nh = nsa_w // HEAD_DIM
    in_w = w_in.shape[1]
    kvw = (in_w - D - N_GATES * nh) // N_KV_GROUPS_FIELDS
    G = kvw // HEAD_DIM
    hg = nh // G
    n_experts = w_router.shape[1]

    xb = x2.astype(BF16)
    w_in_b = w_in.astype(BF16)
    proj_a = _matmul_cols(xb, w_in_b, 0, D)
    proj_b = _matmul_cols(xb, w_in_b, D, 2 * kvw)
    proj_c = _matmul_cols(xb, w_in_b, D + 2 * kvw, 4 * kvw)
    w_gates = jnp.pad(w_in_b[:, D + 6 * kvw:], ((0, 0), (0, LANES - N_GATES * nh)))
    gates = _matmul_cols(xb, w_gates, 0, LANES)

    y_pool = _pool_mixer(proj_a, w_pool.astype(BF16), pool_scale, B, S, pool_w)

    cos2, sin2 = _rope_tables(jnp.arange(S))
    nc = S // CMP_STRIDE
    ccos, csin = _rope_tables(jnp.arange(nc) * CMP_STRIDE + CMP_BLOCK - 1)
    kc = _compress(proj_b, 0, cmp_k_pe, cmp_k_w1, cmp_k_w2, ccos, csin, B, S, G, rope=True)
    vc = _compress(proj_b, 1, cmp_v_pe, cmp_v_w1, cmp_v_w2, ccos, csin, B, S, G, rope=False)
    q_rot, k_slc, v_slc, k_win, v_win = _rope_prep(proj_a, proj_c, cos2, sin2, B, S, nsa_w, G)
    o_cmp, sel_bias = _cmp_select(q_rot, kc, vc, B, S, G, hg)
    tq = _pick_tile(S, (256, 128))
    o_slc = _selected_attention(q_rot, k_slc, v_slc, sel_bias, B, S, G, hg, tq)
    gates_g = gates[:, :N_GATES * nh].reshape(T, G, N_GATES * hg).transpose(1, 0, 2)
    gates_g = jnp.pad(gates_g, ((0, 0), (0, 0), (0, LANES - N_GATES * hg)))
    y_nsa = _window_attention_gated(q_rot, k_win, v_win, gates_g, o_cmp, o_slc, B, S, G, hg, tq, WINDOW)

    ne_pad = -(-n_experts // LANES) * LANES
    w_router_p = jnp.pad(w_router, ((0, 0), (0, ne_pad - n_experts))).astype(BF16)
    b_router_p = jnp.pad(b_router, (0, ne_pad - n_experts)).reshape(1, ne_pad)
    h, hb, logits = _mix_ln_router(y_pool, y_nsa, w_out.astype(BF16), x2, ln1_g, ln1_b,
                                   w_router_p, b_router_p, alpha)

    row_tok, row_w, blk_e, blk_valid, pos = _routing_tables(logits, n_experts)
    F = w_down.shape[1]
    w_pairs = _pair_split_cast(w_up)
    b_pairs = b_up.reshape(n_experts, F // LANES, LANES, 2).transpose(0, 1, 3, 2).reshape(n_experts, 1, 2 * F)
    act = _moe_up(row_tok, blk_e, blk_valid, h, w_pairs, b_pairs)
    y = _moe_down(blk_e, blk_valid, act, w_down, b_down.reshape(n_experts, 1, D), row_w)

    z = _ple_residual(hb, w_ple_gate.astype(BF16), b_ple_gate, p2, w_ple.astype(BF16), h, alpha)
    return _combine_ln(pos, y, z, ln2_g, ln2_b)


def kernel(x, p, w_in, w_pool, pool_scale, cmp_k_pe, cmp_k_w1, cmp_k_w2, cmp_v_pe, cmp_v_w1, cmp_v_w2, w_out, ln1_g, ln1_b, w_router, b_router, w_up, b_up, w_down, b_down, w_ple_gate, b_ple_gate, w_ple, ln2_g, ln2_b):
    B, S, D = x.shape
    depth = w_in.shape[0]
    alpha = (2 * depth) ** 0.25
    x2 = x.reshape(B * S, D)
    for i in range(depth):
        x2 = _layer(x2, p[i].reshape(B * S, -1), B, S, w_in[i], w_pool[i], pool_scale[i],
                    cmp_k_pe[i], cmp_k_w1[i], cmp_k_w2[i], cmp_v_pe[i], cmp_v_w1[i], cmp_v_w2[i],
                    w_out[i], ln1_g[i], ln1_b[i], w_router[i], b_router[i], w_up[i], b_up[i],
                    w_down[i], b_down[i], w_ple_gate[i], b_ple_gate[i], w_ple[i], ln2_g[i], ln2_b[i], alpha)
    return x2.reshape(B, S, D)
```

```python
import functools
import math

import jax
import jax.numpy as jnp
import numpy as np
from jax import lax
from jax.experimental import pallas as pl
from jax.experimental.pallas import tpu as pltpu

HEAD_DIM = 128
LANES = 128
N_KV_GROUPS_FIELDS = 6
N_GATES = 3
POOL_WINDOWS = (2, 4, 8, 16)
CMP_BLOCK = 32
CMP_STRIDE = 16
SEL_BLOCK = 64
N_SEL = 16
WINDOW = 512
ROPE_THETA = 10000.0
TOP_K = 4
SWIGLU_LIMIT = 7.0
SWIGLU_ALPHA = 1.702
EXPERT_ROW_BLOCK = 256
LN_EPS = 1e-5
NEG_INF = -1e30
FORCE_SCORE = 1e4
LOG2E = 1.4426950408889634
EPI_ROWS = 128
ISSUE_UNROLL = 8
GATHER_AHEAD = 2
NBUF = GATHER_AHEAD + 1
MIB = 1024 * 1024

BF16 = jnp.bfloat16
F32 = jnp.float32
NT_DIMS = (((1,), (1,)), ((), ()))


def _params(sem, vmem_mib=None):
    kw = dict(dimension_semantics=sem)
    if vmem_mib is not None:
        kw["vmem_limit_bytes"] = vmem_mib * MIB
    return pltpu.CompilerParams(**kw)


def _pick_tile(n, candidates):
    for c in candidates:
        if n % c == 0:
            return c
    raise ValueError(f"no tile for {n}")


def _mm_kernel(x_ref, w_ref, o_ref):
    @pl.when(pl.program_id(2) == 0)
    def _():
        o_ref[...] = jnp.zeros_like(o_ref)

    o_ref[...] += jnp.dot(x_ref[...], w_ref[...], preferred_element_type=F32)


def _matmul_cols(x, w, col0, ncols):
    M, K = x.shape
    tm = _pick_tile(M, (1024, 512, 256))
    tk = _pick_tile(K, (2048, 1024, 512, 256, 128))
    tn = _pick_tile(math.gcd(col0, ncols) if col0 else ncols, (1024, 512, 256, 128))
    off = col0 // tn
    return pl.pallas_call(
        _mm_kernel,
        name="in_proj",
        grid=(M // tm, ncols // tn, K // tk),
        in_specs=[pl.BlockSpec((tm, tk), lambda i, j, k: (i, k)),
                  pl.BlockSpec((tk, tn), lambda i, j, k: (k, j + off))],
        out_specs=pl.BlockSpec((tm, tn), lambda i, j, k: (i, j)),
        out_shape=jax.ShapeDtypeStruct((M, ncols), F32),
        compiler_params=_params(("parallel", "parallel", "arbitrary"), 40),
    )(x, w)


def _pool_kernel(u_ref, prev_ref, w_ref, sc_ref, o_ref, ext_ref, *, ts, halo):
    g = pl.program_id(0)
    i = pl.program_id(2)
    cur = u_ref[...]
    ext_ref[pl.ds(halo, ts), :] = cur

    @pl.when(i == 0)
    def _():
        ext_ref[pl.ds(0, halo), :] = jnp.zeros((halo, cur.shape[1]), F32)

    @pl.when(i > 0)
    def _():
        ext_ref[pl.ds(0, halo), :] = prev_ref[...]

    def back(d):
        return ext_ref[pl.ds(halo - d, ts), :]

    s2 = cur + back(1)
    s4 = s2 + back(2) + back(3)
    s8 = s4 + back(4) + back(5) + back(6) + back(7)
    s16 = s8
    for d in range(8, 16):
        s16 = s16 + back(d)
    ssum = jnp.where(g == 0, s2, jnp.where(g == 1, s4, jnp.where(g == 2, s8, s16)))
    win = jnp.left_shift(2, g)
    t = i * ts + lax.broadcasted_iota(jnp.int32, (ts, 1), 0)
    cnt = jnp.minimum(t + 1, win).astype(F32)
    pooled = ssum / cnt - cur
    mixed = jnp.dot(pooled.astype(BF16), w_ref[0], preferred_element_type=F32)
    o_ref[...] = (mixed * sc_ref[...]).astype(o_ref.dtype)


def _pool_mixer(proj_a, w_pool_b, pool_scale, B, S, pool_w):
    T = B * S
    ng = len(POOL_WINDOWS)
    pg = pool_w // ng
    halo = POOL_WINDOWS[-1]
    ts = _pick_tile(S, (512, 256, 128))
    ns = S // ts
    hb = ts // halo
    kern = functools.partial(_pool_kernel, ts=ts, halo=halo)
    return pl.pallas_call(
        kern,
        name="pool_mixer",
        grid=(ng, B, ns),
        in_specs=[
            pl.BlockSpec((ts, pg), lambda g, b, i: (b * ns + i, g)),
            pl.BlockSpec((halo, pg), lambda g, b, i: (jnp.maximum((b * ns + i) * hb - 1, 0), g)),
            pl.BlockSpec((1, pg, pg), lambda g, b, i: (g, 0, 0)),
            pl.BlockSpec((1, pg), lambda g, b, i: (0, g)),
        ],
        out_specs=pl.BlockSpec((ts, pg), lambda g, b, i: (b * ns + i, g)),
        out_shape=jax.ShapeDtypeStruct((T, pool_w), BF16),
        scratch_shapes=[pltpu.VMEM((ts + halo, pg), F32)],
        compiler_params=_params(("parallel", "parallel", "arbitrary"), 32),
    )(proj_a, proj_a, w_pool_b, pool_scale.reshape(1, pool_w))


def _rope_tables(pos):
    inv = ROPE_THETA ** (-jnp.arange(0, HEAD_DIM, 2, dtype=F32) / HEAD_DIM)
    ang = pos.astype(F32)[:, None] * inv[None, :]
    c, s = jnp.cos(ang), jnp.sin(ang)
    return jnp.concatenate([c, c], axis=-1), jnp.concatenate([-s, s], axis=-1)


def _rope(x, cos2, sin2):
    return x * cos2 + pltpu.roll(x, HEAD_DIM // 2, 1) * sin2


def _compress_kernel(x_ref, pe_ref, w1_ref, w2_ref, cos_ref, sin_ref, o_ref, *, rope, nl):
    nc = x_ref.shape[0] // nl
    xs = [x_ref[pl.ds(l, nc, stride=nl), :] for l in range(nl)]
    lo = jnp.concatenate([(xs[l] + pe_ref[l:l + 1, :]).astype(BF16) for l in range(nl)], axis=1)
    hi = jnp.concatenate([(xs[l] + pe_ref[nl + l:nl + l + 1, :]).astype(BF16) for l in range(nl)], axis=1)
    half = nl * HEAD_DIM
    a = jnp.dot(lo, w1_ref[0:half, :], preferred_element_type=F32)
    b = jnp.dot(hi, w1_ref[half:2 * half, :], preferred_element_type=F32)
    hid = jax.nn.gelu(a + pltpu.roll(b, nc - 1, 0))
    out = jnp.dot(hid.astype(BF16), w2_ref[...], preferred_element_type=F32)
    if rope:
        out = _rope(out, cos_ref[...], sin_ref[...])
    o_ref[0, 0] = out.astype(o_ref.dtype)


def _compress(proj_b, field, pe, w1, w2, cos2, sin2, B, S, G, rope):
    nl = CMP_STRIDE
    nc = S // nl
    kern = functools.partial(_compress_kernel, rope=rope, nl=nl)
    const = lambda b, g: (0, 0)
    return pl.pallas_call(
        kern,
        name="compress_k" if rope else "compress_v",
        grid=(B, G),
        in_specs=[
            pl.BlockSpec((S, HEAD_DIM), lambda b, g: (b, field * G + g)),
            pl.BlockSpec((CMP_BLOCK, HEAD_DIM), const),
            pl.BlockSpec((CMP_BLOCK * HEAD_DIM, HEAD_DIM), const),
            pl.BlockSpec((HEAD_DIM, HEAD_DIM), const),
            pl.BlockSpec((nc, HEAD_DIM), const),
            pl.BlockSpec((nc, HEAD_DIM), const),
        ],
        out_specs=pl.BlockSpec((1, 1, nc, HEAD_DIM), lambda b, g: (b, g, 0, 0)),
        out_shape=jax.ShapeDtypeStruct((B, G, nc, HEAD_DIM), BF16),
        compiler_params=_params(("parallel", "parallel"), 32),
    )(proj_b, pe, w1.reshape(CMP_BLOCK * HEAD_DIM, HEAD_DIM).astype(BF16), w2.astype(BF16), cos2, sin2)


def _rope_prep_kernel(q_ref, kv_ref, cos_ref, sin_ref, qo_ref, ks_ref, vs_ref, kw_ref, vw_ref, *, nh, G, ts):
    i = pl.program_id(1)
    c = cos_ref[...]
    s = sin_ref[...]
    kvw = G * HEAD_DIM
    for h in range(nh):
        qo_ref[0, h] = _rope(q_ref[:, h * HEAD_DIM:(h + 1) * HEAD_DIM], c, s).astype(BF16)
    blk = (i * ts + lax.broadcasted_iota(jnp.int32, (ts, LANES), 0)) // SEL_BLOCK
    onehot = (blk == lax.broadcasted_iota(jnp.int32, (ts, LANES), 1)).astype(BF16)
    for g in range(G):
        lo = g * HEAD_DIM
        ks_ref[0, g, :, 0:HEAD_DIM] = _rope(kv_ref[:, lo:lo + HEAD_DIM], c, s).astype(BF16)
        ks_ref[0, g, :, HEAD_DIM:2 * HEAD_DIM] = onehot
        vs_ref[0, g] = kv_ref[:, kvw + lo:kvw + lo + HEAD_DIM].astype(BF16)
        kw_ref[0, g] = _rope(kv_ref[:, 2 * kvw + lo:2 * kvw + lo + HEAD_DIM], c, s).astype(BF16)
        vw_ref[0, g] = kv_ref[:, 3 * kvw + lo:3 * kvw + lo + HEAD_DIM].astype(BF16)


def _rope_prep(proj_a, proj_c, cos2, sin2, B, S, nsa_w, G):
    nh = nsa_w // HEAD_DIM
    ts = _pick_tile(S, (256, 128))
    ns = S // ts
    kern = functools.partial(_rope_prep_kernel, nh=nh, G=G, ts=ts)
    hm = lambda b, i: (b, 0, i, 0)
    return pl.pallas_call(
        kern,
        name="rope_prep",
        grid=(B, ns),
        in_specs=[
            pl.BlockSpec((ts, nsa_w), lambda b, i: (b * ns + i, 1)),
            pl.BlockSpec((ts, 4 * G * HEAD_DIM), lambda b, i: (b * ns + i, 0)),
            pl.BlockSpec((ts, HEAD_DIM), lambda b, i: (i, 0)),
            pl.BlockSpec((ts, HEAD_DIM), lambda b, i: (i, 0)),
        ],
        out_specs=[
            pl.BlockSpec((1, nh, ts, HEAD_DIM), hm),
            pl.BlockSpec((1, G, ts, 2 * HEAD_DIM), hm),
            pl.BlockSpec((1, G, ts, HEAD_DIM), hm),
            pl.BlockSpec((1, G, ts, HEAD_DIM), hm),
            pl.BlockSpec((1, G, ts, HEAD_DIM), hm),
        ],
        out_shape=[
            jax.ShapeDtypeStruct((B, nh, S, HEAD_DIM), BF16),
            jax.ShapeDtypeStruct((B, G, S, 2 * HEAD_DIM), BF16),
            jax.ShapeDtypeStruct((B, G, S, HEAD_DIM), BF16),
            jax.ShapeDtypeStruct((B, G, S, HEAD_DIM), BF16),
            jax.ShapeDtypeStruct((B, G, S, HEAD_DIM), BF16),
        ],
        compiler_params=_params(("parallel", "parallel"), 40),
    )(proj_a, proj_c, cos2, sin2)


def _cmp_select_kernel(q_ref, kc_ref, vc_ref, ovl_ref, o_ref, bias_ref, *, hg, tq, n_sel, n_keep, scale):
    i = pl.program_id(2)
    nc = kc_ref.shape[2]
    kc = kc_ref[0, 0]
    vc = vc_ref[0, 0]
    t = i * tq + lax.broadcasted_iota(jnp.int32, (tq, nc), 0)
    n = lax.broadcasted_iota(jnp.int32, (tq, nc), 1)
    valid = (n * CMP_STRIDE + (CMP_BLOCK - 1) <= t) & (n < nc - 1)
    ovl = ovl_ref[...]
    imp = jnp.zeros((n_sel, tq), F32)
    for h in range(hg):
        s = lax.dot_general(q_ref[0, h], kc, NT_DIMS, preferred_element_type=F32) * scale
        s = jnp.where(valid, s, NEG_INF)
        m = jnp.max(s, axis=-1, keepdims=True)
        e = jnp.where(valid, jnp.exp(s - m), 0.0)
        den = jnp.sum(e, axis=-1, keepdims=True)
        pc = (e / jnp.where(den > 0.0, den, 1.0)).astype(BF16)
        o_ref[0, h] = jnp.dot(pc, vc, preferred_element_type=F32)
        imp = imp + lax.dot_general(ovl, pc, NT_DIMS, preferred_element_type=F32)
    j = lax.broadcasted_iota(jnp.int32, (n_sel, tq), 0)
    tb = (i * tq + lax.broadcasted_iota(jnp.int32, (n_sel, tq), 1)) // SEL_BLOCK
    forced = (j == 0) | (j == tb) | (j == tb - 1)
    score = jnp.where(forced, FORCE_SCORE, jnp.where(j <= tb, imp, -1.0))
    rank = jnp.zeros((n_sel, tq), jnp.int32)
    for jp in range(n_sel):
        row = score[jp:jp + 1, :]
        ahead = (row > score) | ((row == score) & (jp < j))
        rank = rank + ahead.astype(jnp.int32)
    bias_t = jnp.where(rank < n_keep, 0.0, NEG_INF)
    if n_sel < LANES:
        bias_t = jnp.concatenate([bias_t, jnp.zeros((LANES - n_sel, tq), F32)], axis=0)
    bias_ref[0, 0] = bias_t.T.astype(BF16)


def _overlap_t(n_cmp_pad, n_sel):
    n_cmp = n_cmp_pad - 1
    cs = np.arange(n_cmp) * CMP_STRIDE
    ce = cs + CMP_BLOCK
    ss = np.arange(n_sel) * SEL_BLOCK
    se = ss + SEL_BLOCK
    ov = np.clip(np.minimum(ce[:, None], se[None, :]) - np.maximum(cs[:, None], ss[None, :]), 0, None)
    out = np.zeros((n_sel, n_cmp_pad), np.float32)
    out[:, :n_cmp] = (ov / CMP_BLOCK).T
    return jnp.asarray(out, dtype=BF16)


def _cmp_select(q_rot, kc, vc, B, S, G, hg):
    nc = kc.shape[2]
    n_sel = S // SEL_BLOCK
    assert n_sel <= LANES and n_sel % 8 == 0
    tq = _pick_tile(S, (256, 128))
    nq = S // tq
    nh = G * hg
    kern = functools.partial(_cmp_select_kernel, hg=hg, tq=tq, n_sel=n_sel,
                             n_keep=min(N_SEL, n_sel), scale=HEAD_DIM ** -0.5)
    return pl.pallas_call(
        kern,
        name="cmp_attn_select",
        grid=(B, G, nq),
        in_specs=[
            pl.BlockSpec((1, hg, tq, HEAD_DIM), lambda b, g, i: (b, g, i, 0)),
            pl.BlockSpec((1, 1, nc, HEAD_DIM), lambda b, g, i: (b, g, 0, 0)),
            pl.BlockSpec((1, 1, nc, HEAD_DIM), lambda b, g, i: (b, g, 0, 0)),
            pl.BlockSpec((n_sel, nc), lambda b, g, i: (0, 0)),
        ],
        out_specs=[
            pl.BlockSpec((1, hg, tq, HEAD_DIM), lambda b, g, i: (b, g, i, 0)),
            pl.BlockSpec((1, 1, tq, LANES), lambda b, g, i: (b, g, i, 0)),
        ],
        out_shape=[
            jax.ShapeDtypeStruct((B, nh, S, HEAD_DIM), F32),
            jax.ShapeDtypeStruct((B, G, S, LANES), BF16),
        ],
        compiler_params=_params(("parallel", "parallel", "parallel"), 40),
    )(q_rot, kc, vc, _overlap_t(nc, n_sel))


def _lane_fold(x, op):
    r = x[:, :LANES]
    for a in range(1, x.shape[1] // LANES):
        r = op(r, x[:, a * LANES:(a + 1) * LANES])
    return r


def _slc_kernel(q_ref, bias_ref, k_ref, v_ref, o_ref, s_ref, m_ref, l_ref, acc_ref, *, hg, tq, scale):
    i = pl.program_id(2)
    rows = hg * tq
    c = scale * LOG2E
    q = q_ref[0].reshape(rows, HEAD_DIM)
    b = bias_ref[0, 0]
    qa = jnp.concatenate([q, jnp.concatenate([b] * hg, axis=0)], axis=1)

    def k_tile(j):
        return k_ref[0, 0, pl.ds(pl.multiple_of(j * tq, tq), tq), :]

    def v_tile(j):
        return v_ref[0, 0, pl.ds(pl.multiple_of(j * tq, tq), tq), :]

    m_ref[...] = jnp.full_like(m_ref, NEG_INF)

    def scores(j, carry):
        s = lax.dot_general(qa, k_tile(j), NT_DIMS, preferred_element_type=F32)
        s_ref[j] = s
        m_ref[...] = jnp.maximum(m_ref[...], _lane_fold(s, jnp.maximum))
        return carry

    lax.fori_loop(0, i, scores, 0)
    sd = lax.dot_general(qa, k_tile(i), NT_DIMS, preferred_element_type=F32)
    tpos = lax.broadcasted_iota(jnp.int32, (rows, tq), 0) & (tq - 1)
    kpos = lax.broadcasted_iota(jnp.int32, (rows, tq), 1)
    sd = jnp.where(kpos <= tpos, sd, NEG_INF)
    mrow = jnp.max(jnp.maximum(m_ref[...], _lane_fold(sd, jnp.maximum)), axis=-1, keepdims=True) * c
    m_ref[...] = jnp.broadcast_to(mrow, (rows, LANES))
    l_ref[...] = jnp.zeros_like(l_ref)
    acc_ref[...] = jnp.zeros_like(acc_ref)

    def probs(s):
        mfull = jnp.concatenate([m_ref[...]] * (tq // LANES), axis=1)
        return jnp.exp2(s * c - mfull)

    def accumulate(j, carry):
        p = probs(s_ref[j])
        l_ref[...] += _lane_fold(p, jnp.add)
        acc_ref[...] += jnp.dot(p.astype(BF16), v_tile(j), preferred_element_type=F32)
        return carry

    lax.fori_loop(0, i, accumulate, 0)
    p = probs(sd)
    den = jnp.sum(l_ref[...] + _lane_fold(p, jnp.add), axis=-1, keepdims=True)
    acc = acc_ref[...] + jnp.dot(p.astype(BF16), v_tile(i), preferred_element_type=F32)
    o_ref[0] = (acc / den).reshape(hg, tq, HEAD_DIM)


def _selected_attention(q_rot, k_aug, v, bias, B, S, G, hg, tq):
    assert tq & (tq - 1) == 0 and tq % LANES == 0
    nq = S // tq
    rows = hg * tq
    kern = functools.partial(_slc_kernel, hg=hg, tq=tq, scale=HEAD_DIM ** -0.5)
    return pl.pallas_call(
        kern,
        name="selected_attn",
        grid=(B, G, nq),
        in_specs=[
            pl.BlockSpec((1, hg, tq, HEAD_DIM), lambda b, g, i: (b, g, i, 0)),
            pl.BlockSpec((1, 1, tq, LANES), lambda b, g, i: (b, g, i, 0)),
            pl.BlockSpec((1, 1, S, 2 * HEAD_DIM), lambda b, g, i: (b, g, 0, 0)),
            pl.BlockSpec((1, 1, S, HEAD_DIM), lambda b, g, i: (b, g, 0, 0)),
        ],
        out_specs=pl.BlockSpec((1, hg, tq, HEAD_DIM), lambda b, g, i: (b, g, i, 0)),
        out_shape=jax.ShapeDtypeStruct((B, G * hg, S, HEAD_DIM), F32),
        scratch_shapes=[pltpu.VMEM((nq, rows, tq), F32), pltpu.VMEM((rows, LANES), F32),
                        pltpu.VMEM((rows, LANES), F32), pltpu.VMEM((rows, HEAD_DIM), F32)],
        compiler_params=_params(("parallel", "parallel", "arbitrary"), 48),
    )(q_rot, bias, k_aug, v)


def _win_kernel(q_ref, g_ref, oc_ref, os_ref, *refs, hg, tq, nprev, window, scale):
    nt = nprev + 1
    k_refs, v_refs, y_ref = refs[:nt], refs[nt:2 * nt], refs[2 * nt]
    i = pl.program_id(2)
    gate = jax.nn.sigmoid(g_ref[0])
    c = scale * LOG2E
    nk = nt * tq
    kcat = jnp.concatenate([r[0, 0] for r in k_refs], axis=0)
    vcat = jnp.concatenate([r[0, 0] for r in v_refs], axis=0)
    tpos = i * tq + lax.broadcasted_iota(jnp.int32, (tq, nk), 0)
    kpos = (i - nprev) * tq + lax.broadcasted_iota(jnp.int32, (tq, nk), 1)
    mask = (kpos <= tpos) & (kpos > tpos - window) & (kpos >= 0)
    for h in range(hg):
        s = lax.dot_general(q_ref[0, h], kcat, NT_DIMS, preferred_element_type=F32)
        s = jnp.where(mask, s, NEG_INF)
        m = jnp.max(s, axis=-1, keepdims=True)
        p = jnp.exp2((s - m) * c)
        den = jnp.sum(p, axis=-1, keepdims=True)
        o_win = jnp.dot(p.astype(BF16), vcat, preferred_element_type=F32) / den
        gc = N_GATES * h
        y = (gate[:, gc:gc + 1] * oc_ref[0, h] + gate[:, gc + 1:gc + 2] * os_ref[0, h]
             + gate[:, gc + 2:gc + 3] * o_win)
        y_ref[:, h * HEAD_DIM:(h + 1) * HEAD_DIM] = y.astype(y_ref.dtype)


def _window_attention_gated(q_rot, k, v, gates_g, o_cmp, o_slc, B, S, G, hg, tq, window):
    assert window % tq == 0
    nprev = window // tq
    nq = S // tq
    kern = functools.partial(_win_kernel, hg=hg, tq=tq, nprev=nprev, window=window, scale=HEAD_DIM ** -0.5)

    def kv_spec(a):
        return pl.BlockSpec((1, 1, tq, HEAD_DIM), lambda b, g, i: (b, g, jnp.maximum(i - nprev + a, 0), 0))

    kv_specs = [kv_spec(a) for a in range(nprev + 1)]
    heads = pl.BlockSpec((1, hg, tq, HEAD_DIM), lambda b, g, i: (b, g, i, 0))
    return pl.pallas_call(
        kern,
        name="window_attn_gated_sum",
        grid=(B, G, nq),
        in_specs=[heads, pl.BlockSpec((1, tq, LANES), lambda b, g, i: (g, b * nq + i, 0)), heads, heads]
        + kv_specs + kv_specs,
        out_specs=pl.BlockSpec((tq, hg * HEAD_DIM), lambda b, g, i: (b * nq + i, g)),
        out_shape=jax.ShapeDtypeStruct((B * S, G * hg * HEAD_DIM), BF16),
        compiler_params=_params(("parallel", "parallel", "arbitrary"), 40),
    )(q_rot, gates_g, o_cmp, o_slc, *([k] * (nprev + 1)), *([v] * (nprev + 1)))


def _layer_norm(z, g, b):
    mu = jnp.mean(z, axis=-1, keepdims=True)
    zc = z - mu
    var = jnp.mean(zc * zc, axis=-1, keepdims=True)
    return zc * lax.rsqrt(var + LN_EPS) * g + b


def _mix_kernel(yp_ref, yn_ref, w_ref, x_ref, g_ref, b_ref, wr_ref, br_ref, h_ref, hb_ref, lg_ref, acc_ref,
                *, nkp, alpha):
    k = pl.program_id(1)

    @pl.when(k == 0)
    def _():
        acc_ref[...] = jnp.zeros_like(acc_ref)

    @pl.when(k < nkp)
    def _():
        acc_ref[...] += jnp.dot(yp_ref[...], w_ref[...], preferred_element_type=F32)

    @pl.when(k >= nkp)
    def _():
        acc_ref[...] += jnp.dot(yn_ref[...], w_ref[...], preferred_element_type=F32)

    @pl.when(k == pl.num_programs(1) - 1)
    def _():
        def epilogue(r, carry):
            rows = pl.ds(pl.multiple_of(r * EPI_ROWS, EPI_ROWS), EPI_ROWS)
            h = _layer_norm(alpha * x_ref[rows, :] + acc_ref[rows, :], g_ref[...], b_ref[...])
            hb = h.astype(BF16)
            h_ref[rows, :] = h
            hb_ref[rows, :] = hb
            lg_ref[rows, :] = jnp.dot(hb, wr_ref[...], preferred_element_type=F32) + br_ref[...]
            return carry

        lax.fori_loop(0, acc_ref.shape[0] // EPI_ROWS, epilogue, 0)


def _mix_ln_router(y_pool, y_nsa, w_out_b, x2, ln_g, ln_b, w_router_p, b_router_p, alpha):
    T, D = x2.shape
    kp = y_pool.shape[1]
    tm = _pick_tile(T, (512, 256, 128))
    tk = _pick_tile(math.gcd(kp, y_nsa.shape[1]), (512, 256, 128))
    nkp = kp // tk
    nk = nkp + y_nsa.shape[1] // tk
    ne = w_router_p.shape[1]
    kern = functools.partial(_mix_kernel, nkp=nkp, alpha=alpha)
    row = lambda i, k: (i, 0)
    const = lambda i, k: (0, 0)
    return pl.pallas_call(
        kern,
        name="out_proj_ln1_router",
        grid=(T // tm, nk),
        in_specs=[
            pl.BlockSpec((tm, tk), lambda i, k: (i, jnp.minimum(k, nkp - 1))),
            pl.BlockSpec((tm, tk), lambda i, k: (i, jnp.maximum(k - nkp, 0))),
            pl.BlockSpec((tk, D), lambda i, k: (k, 0)),
            pl.BlockSpec((tm, D), row, pipeline_mode=pl.Buffered(1)),
            pl.BlockSpec((1, D), const),
            pl.BlockSpec((1, D), const),
            pl.BlockSpec((D, ne), const),
            pl.BlockSpec((1, ne), const),
        ],
        out_specs=[pl.BlockSpec((tm, D), row), pl.BlockSpec((tm, D), row), pl.BlockSpec((tm, ne), row)],
        out_shape=[jax.ShapeDtypeStruct((T, D), F32), jax.ShapeDtypeStruct((T, D), BF16),
                   jax.ShapeDtypeStruct((T, ne), F32)],
        scratch_shapes=[pltpu.VMEM((tm, D), F32)],
        compiler_params=_params(("parallel", "arbitrary"), 56),
    )(y_pool, y_nsa, w_out_b, x2, ln_g.reshape(1, D), ln_b.reshape(1, D), w_router_p, b_router_p)


def _row_copy(src_hbm, row, dst_ref, slot, sem):
    return pltpu.make_async_copy(src_hbm.at[pl.ds(row, 1)], dst_ref.at[pl.ds(slot, 1)], sem)


def _rows_copy(src_hbm, dst_ref, sem, n):
    return pltpu.make_async_copy(src_hbm.at[pl.ds(0, n)], dst_ref, sem)


PAIR = 2 * LANES


def _pair_split_kernel(w_ref, perm_ref, o_ref):
    perm = perm_ref[...]
    for c in range(w_ref.shape[2] // PAIR):
        sl = slice(c * PAIR, (c + 1) * PAIR)
        o_ref[0, :, sl] = jnp.dot(w_ref[0, :, sl].astype(BF16), perm, preferred_element_type=F32).astype(BF16)


def _pair_split_cast(w_up):
    E, D, F2 = w_up.shape
    assert F2 % PAIR == 0
    td = _pick_tile(D, (512, 256, 128))
    perm = np.zeros((PAIR, PAIR), np.float32)
    perm[2 * np.arange(LANES), np.arange(LANES)] = 1.0
    perm[2 * np.arange(LANES) + 1, LANES + np.arange(LANES)] = 1.0
    return pl.pallas_call(
        _pair_split_kernel,
        name="w_up_pair_split",
        grid=(E, D // td),
        in_specs=[pl.BlockSpec((1, td, F2), lambda e, i: (e, i, 0)),
                  pl.BlockSpec((PAIR, PAIR), lambda e, i: (0, 0))],
        out_specs=pl.BlockSpec((1, td, F2), lambda e, i: (e, i, 0)),
        out_shape=jax.ShapeDtypeStruct((E, D, F2), BF16),
        compiler_params=_params(("parallel", "parallel"), 32),
    )(w_up, jnp.asarray(perm, dtype=BF16))


def _moe_up_kernel(tok_ref, be_ref, bv_ref, h_hbm, w_ref, b_ref, act_ref, xbuf_ref, sem, *, rb, nb):
    i = pl.program_id(0)

    def gather(blk, slot):
        def issue(r, carry):
            _row_copy(h_hbm, tok_ref[blk * rb + r], xbuf_ref.at[slot], r, sem.at[slot]).start()
            return carry

        lax.fori_loop(0, rb, issue, 0, unroll=ISSUE_UNROLL)

    def drain(slot):
        _rows_copy(h_hbm, xbuf_ref.at[slot], sem.at[slot], rb).wait()

    @pl.when((i == 0) & (bv_ref[0] != 0))
    def _():
        for a in range(GATHER_AHEAD):
            gather(a, a)

    slot = i % NBUF
    valid = bv_ref[i] != 0
    started = bv_ref[jnp.maximum(i - GATHER_AHEAD, 0)] != 0

    @pl.when(jnp.logical_not(valid))
    def _():
        act_ref[...] = jnp.zeros_like(act_ref)

    @pl.when(jnp.logical_not(valid) & started)
    def _():
        drain(slot)

    @pl.when(valid)
    def _():
        drain(slot)
        xb = xbuf_ref[slot].astype(BF16)
        nxt = jnp.minimum(i + GATHER_AHEAD, nb - 1)
        nslot = (i + GATHER_AHEAD) % NBUF
        for r in range(rb):
            _row_copy(h_hbm, tok_ref[nxt * rb + r], xbuf_ref.at[nslot], r, sem.at[nslot]).start()
        hcat = jnp.dot(xb, w_ref[0], preferred_element_type=F32) + b_ref[0]
        for c in range(hcat.shape[1] // PAIR):
            gate = jnp.minimum(hcat[:, c * PAIR:c * PAIR + LANES], SWIGLU_LIMIT)
            up = jnp.clip(hcat[:, c * PAIR + LANES:(c + 1) * PAIR], -SWIGLU_LIMIT, SWIGLU_LIMIT)
            act = gate * jax.nn.sigmoid(SWIGLU_ALPHA * gate) * (up + 1.0)
            act_ref[:, c * LANES:(c + 1) * LANES] = act.astype(act_ref.dtype)

    for a in range(GATHER_AHEAD):
        j = nb - 1 - a

        @pl.when((i == nb - 1) & (bv_ref[j] != 0))
        def _():
            drain((j + GATHER_AHEAD) % NBUF)


def _moe_up(row_tok, blk_e, blk_valid, h, w_pairs, b_pairs):
    n_rows = row_tok.shape[0]
    rb = EXPERT_ROW_BLOCK
    nb = n_rows // rb
    assert nb > GATHER_AHEAD
    E, D, F2 = w_pairs.shape
    wmap = lambda i, tok, be, bv: (be[i], 0, 0)
    return pl.pallas_call(
        functools.partial(_moe_up_kernel, rb=rb, nb=nb),
        name="moe_gather_up_swiglu",
        grid_spec=pltpu.PrefetchScalarGridSpec(
            num_scalar_prefetch=3,
            grid=(n_rows // rb,),
            in_specs=[
                pl.BlockSpec(memory_space=pl.ANY),
                pl.BlockSpec((1, D, F2), wmap),
                pl.BlockSpec((1, 1, F2), wmap),
            ],
            out_specs=pl.BlockSpec((rb, F2 // 2), lambda i, tok, be, bv: (i, 0)),
            scratch_shapes=[pltpu.VMEM((NBUF, rb, D), F32), pltpu.SemaphoreType.DMA((NBUF,))],
        ),
        out_shape=jax.ShapeDtypeStruct((n_rows, F2 // 2), BF16),
        compiler_params=_params(("arbitrary",), 56),
    )(row_tok, blk_e, blk_valid, h, w_pairs, b_pairs)


def _moe_down_kernel(be_ref, bv_ref, act_ref, wd_ref, bd_ref, rw_ref, y_ref, wb_ref):
    i = pl.program_id(0)

    @pl.when(bv_ref[i] == 0)
    def _():
        y_ref[...] = jnp.zeros_like(y_ref)

    @pl.when((bv_ref[i] != 0) & ((i == 0) | (be_ref[i] != be_ref[jnp.maximum(i - 1, 0)])))
    def _():
        wb_ref[...] = wd_ref[0].astype(BF16)

    @pl.when(bv_ref[i] != 0)
    def _():
        y = (jnp.dot(act_ref[...], wb_ref[...], preferred_element_type=F32) + bd_ref[0]) * rw_ref[...]
        bits = pltpu.bitcast(y.astype(BF16).astype(F32), jnp.uint32)
        half = bits.shape[1] // 2
        y_ref[...] = (bits[:, :half] >> 16) | (bits[:, half:] & jnp.uint32(0xFFFF0000))


def _moe_down(blk_e, blk_valid, act, wd, bd, row_w):
    n_rows, F = act.shape
    rb = EXPERT_ROW_BLOCK
    E, _, D = wd.shape
    wmap = lambda i, be, bv: (be[i], 0, 0)
    return pl.pallas_call(
        _moe_down_kernel,
        name="moe_down",
        grid_spec=pltpu.PrefetchScalarGridSpec(
            num_scalar_prefetch=2,
            grid=(n_rows // rb,),
            in_specs=[
                pl.BlockSpec((rb, F), lambda i, be, bv: (i, 0)),
                pl.BlockSpec((1, F, D), wmap),
                pl.BlockSpec((1, 1, D), wmap),
                pl.BlockSpec((rb, 1), lambda i, be, bv: (i, 0)),
            ],
            out_specs=pl.BlockSpec((rb, D // 2), lambda i, be, bv: (i, 0)),
            scratch_shapes=[pltpu.VMEM((F, D), BF16)],
        ),
        out_shape=jax.ShapeDtypeStruct((n_rows, D // 2), jnp.uint32),
        compiler_params=_params(("arbitrary",), 56),
    )(blk_e, blk_valid, act, wd, bd, row_w.reshape(n_rows, 1))


def _ple_kernel(hb_ref, w_ref, bg_ref, p_ref, wp_ref, h_ref, z_ref, *, alpha):
    k = pl.program_id(2)

    @pl.when(k == 0)
    def _():
        z_ref[...] = jnp.zeros_like(z_ref)

    z_ref[...] += jnp.dot(hb_ref[...], w_ref[...], preferred_element_type=F32)

    @pl.when(k == pl.num_programs(2) - 1)
    def _():
        gate = jax.nn.sigmoid(z_ref[...] + bg_ref[...])
        emb = jnp.dot(p_ref[...].astype(BF16), wp_ref[...], preferred_element_type=F32)
        z_ref[...] = alpha * h_ref[...] + gate * emb


def _ple_residual(hb, w_gate_b, b_gate, p2, w_ple_b, h, alpha):
    T, D = h.shape
    pd = p2.shape[1]
    tm = _pick_tile(T, (1024, 512, 256, 128))
    tn = _pick_tile(D, (1024, 512, 256, 128))
    tk = _pick_tile(D, (2048, 1024, 512, 256, 128))
    return pl.pallas_call(
        functools.partial(_ple_kernel, alpha=alpha),
        name="ple_gate_residual",
        grid=(T // tm, D // tn, D // tk),
        in_specs=[
            pl.BlockSpec((tm, tk), lambda i, j, k: (i, k)),
            pl.BlockSpec((tk, tn), lambda i, j, k: (k, j)),
            pl.BlockSpec((1, tn), lambda i, j, k: (0, j)),
            pl.BlockSpec((tm, pd), lambda i, j, k: (i, 0)),
            pl.BlockSpec((pd, tn), lambda i, j, k: (0, j)),
            pl.BlockSpec((tm, tn), lambda i, j, k: (i, j)),
        ],
        out_specs=pl.BlockSpec((tm, tn), lambda i, j, k: (i, j)),
        out_shape=jax.ShapeDtypeStruct((T, D), F32),
        compiler_params=_params(("parallel", "parallel", "arbitrary"), 48),
    )(hb, w_gate_b, b_gate.reshape(1, D), p2, w_ple_b, h)


def _combine_kernel(pos_ref, y_hbm, z_ref, g_ref, b_ref, o_ref, buf_ref, sem, *, tc, topk, nb):
    i = pl.program_id(0)

    def gather(blk, slot):
        def issue(t, carry):
            for k in range(topk):
                _row_copy(y_hbm, pos_ref[(blk * tc + t) * topk + k], buf_ref.at[slot, k], t, sem.at[slot]).start()
            return carry

        lax.fori_loop(0, tc, issue, 0, unroll=ISSUE_UNROLL // topk)

    def drain(slot):
        for k in range(topk):
            _rows_copy(y_hbm, buf_ref.at[slot, k], sem.at[slot], tc).wait()

    @pl.when(i == 0)
    def _():
        for a in range(GATHER_AHEAD):
            gather(a, a)

    slot = i % NBUF
    drain(slot)
    nxt = jnp.minimum(i + GATHER_AHEAD, nb - 1)
    nslot = (i + GATHER_AHEAD) % NBUF
    for t in range(tc):
        for k in range(topk):
            _row_copy(y_hbm, pos_ref[(nxt * tc + t) * topk + k], buf_ref.at[nslot, k], t, sem.at[nslot]).start()
    lo = hi = None
    for k in range(topk):
        word = buf_ref[slot, k]
        lo_k = pltpu.bitcast(word << 16, F32)
        hi_k = pltpu.bitcast(word & jnp.uint32(0xFFFF0000), F32)
        lo = lo_k if lo is None else lo + lo_k
        hi = hi_k if hi is None else hi + hi_k
    ffn = jnp.concatenate([lo, hi], axis=1)
    o_ref[...] = _layer_norm(z_ref[...] + ffn, g_ref[...], b_ref[...])

    @pl.when(i == nb - 1)
    def _():
        for a in range(GATHER_AHEAD):
            drain((nb - 1 - a + GATHER_AHEAD) % NBUF)


def _combine_ln(pos, y, z, ln_g, ln_b):
    T, D = z.shape
    tc = _pick_tile(T, (128, 64, 32))
    row = lambda i, pos: (i, 0)
    const = lambda i, pos: (0, 0)
    nb = T // tc
    assert nb > GATHER_AHEAD
    return pl.pallas_call(
        functools.partial(_combine_kernel, tc=tc, topk=TOP_K, nb=nb),
        name="moe_combine_ln2",
        grid_spec=pltpu.PrefetchScalarGridSpec(
            num_scalar_prefetch=1,
            grid=(T // tc,),
            in_specs=[
                pl.BlockSpec(memory_space=pl.ANY),
                pl.BlockSpec((tc, D), row),
                pl.BlockSpec((1, D), const),
                pl.BlockSpec((1, D), const),
            ],
            out_specs=pl.BlockSpec((tc, D), row),
            scratch_shapes=[pltpu.VMEM((NBUF, TOP_K, tc, D // 2), jnp.uint32), pltpu.SemaphoreType.DMA((NBUF,))],
        ),
        out_shape=jax.ShapeDtypeStruct((T, D), F32),
        compiler_params=_params(("arbitrary",), 32),
    )(pos, y, z, ln_g.reshape(1, D), ln_b.reshape(1, D))


def _routing_tables(logits, n_experts):
    T = logits.shape[0]
    A = T * TOP_K
    rb = EXPERT_ROW_BLOCK
    i32 = jnp.int32
    top_logit, top_e = lax.top_k(logits[:, :n_experts], TOP_K)
    top_w = jax.nn.softmax(top_logit, axis=-1)
    flat_e = top_e.reshape(-1).astype(i32)
    order = jnp.argsort(flat_e).astype(i32)
    sorted_pos = jnp.argsort(order).astype(i32)
    onehot = (flat_e[:, None] == jnp.arange(n_experts, dtype=i32)[None, :]).astype(i32)
    counts = jnp.sum(onehot, axis=0)
    padded = (counts + rb - 1) // rb * rb
    pad_end = jnp.cumsum(padded)
    pad_start = pad_end - padded
    grp_start = jnp.cumsum(counts) - counts
    pos = (jnp.sum(onehot * (pad_start - grp_start)[None, :], axis=1) + sorted_pos).astype(i32)
    n_blocks = -(-A // rb) + n_experts
    blk_start = jnp.arange(n_blocks, dtype=i32) * rb
    blk_e = jnp.minimum(jnp.sum((pad_end[None, :] <= blk_start[:, None]).astype(i32), axis=1), n_experts - 1)
    blk_valid = (blk_start < pad_end[-1]).astype(i32)
    off = (blk_start - pad_start[blk_e])[:, None] + jnp.arange(rb, dtype=i32)[None, :]
    valid = off < counts[blk_e][:, None]
    src = order[jnp.clip(grp_start[blk_e][:, None] + off, 0, A - 1)]
    row_tok = jnp.where(valid, src // TOP_K, 0).astype(i32).reshape(-1)
    row_w = jnp.where(valid, top_w.reshape(-1)[src], 0.0).reshape(-1)
    return row_tok, row_w, blk_e.astype(i32), blk_valid, pos


def _layer(x2, p2, B, S, w_in, w_pool, pool_scale, cmp_k_pe, cmp_k_w1, cmp_k_w2, cmp_v_pe, cmp_v_w1, cmp_v_w2,
           w_out, ln1_g, ln1_b, w_router, b_router, w_up, b_up, w_down, b_down,
           w_ple_gate, b_ple_gate, w_ple, ln2_g, ln2_b, alpha):
    T, D = x2.shape
    pool_w = D // 2
    nsa_w = D - pool_w
    nh = nsa_w // HEAD_DIM
    in_w = w_in.shape[1]
    kvw = (in_w - D - N_GATES * nh) // N_KV_GROUPS_FIELDS
    G = kvw // HEAD_DIM
    hg = nh // G
    n_experts = w_router.shape[1]

    xb = x2.astype(BF16)
    w_in_b = w_in.astype(BF16)
    proj_a = _matmul_cols(xb, w_in_b, 0, D)
    proj_b = _matmul_cols(xb, w_in_b, D, 2 * kvw)
    proj_c = _matmul_cols(xb, w_in_b, D + 2 * kvw, 4 * kvw)
    w_gates = jnp.pad(w_in_b[:, D + 6 * kvw:], ((0, 0), (0, LANES - N_GATES * nh)))
    gates = _matmul_cols(xb, w_gates, 0, LANES)

    y_pool = _pool_mixer(proj_a, w_pool.astype(BF16), pool_scale, B, S, pool_w)

    cos2, sin2 = _rope_tables(jnp.arange(S))
    nc = S // CMP_STRIDE
    ccos, csin = _rope_tables(jnp.arange(nc) * CMP_STRIDE + CMP_BLOCK - 1)
    kc = _compress(proj_b, 0, cmp_k_pe, cmp_k_w1, cmp_k_w2, ccos, csin, B, S, G, rope=True)
    vc = _compress(proj_b, 1, cmp_v_pe, cmp_v_w1, cmp_v_w2, ccos, csin, B, S, G, rope=False)
    q_rot, k_slc, v_slc, k_win, v_win = _rope_prep(proj_a, proj_c, cos2, sin2, B, S, nsa_w, G)
    o_cmp, sel_bias = _cmp_select(q_rot, kc, vc, B, S, G, hg)
    tq = _pick_tile(S, (256, 128))
    o_slc = _selected_attention(q_rot, k_slc, v_slc, sel_bias, B, S, G, hg, tq)
    gates_g = gates[:, :N_GATES * nh].reshape(T, G, N_GATES * hg).transpose(1, 0, 2)
    gates_g = jnp.pad(gates_g, ((0, 0), (0, 0), (0, LANES - N_GATES * hg)))
    y_nsa = _window_attention_gated(q_rot, k_win, v_win, gates_g, o_cmp, o_slc, B, S, G, hg, tq, WINDOW)

    ne_pad = -(-n_experts // LANES) * LANES
    w_router_p = jnp.pad(w_router, ((0, 0), (0, ne_pad - n_experts))).astype(BF16)
    b_router_p = jnp.pad(b_router, (0, ne_pad - n_experts)).reshape(1, ne_pad)
    h, hb, logits = _mix_ln_router(y_pool, y_nsa, w_out.astype(BF16), x2, ln1_g, ln1_b,
                                   w_router_p, b_router_p, alpha)

    row_tok, row_w, blk_e, blk_valid, pos = _routing_tables(logits, n_experts)
    F = w_down.shape[1]
    w_pairs = _pair_split_cast(w_up)
    b_pairs = b_up.reshape(n_experts, F // LANES, LANES, 2).transpose(0, 1, 3, 2).reshape(n_experts, 1, 2 * F)
    act = _moe_up(row_tok, blk_e, blk_valid, h, w_pairs, b_pairs)
    y = _moe_down(blk_e, blk_valid, act, w_down, b_down.reshape(n_experts, 1, D), row_w)

    z = _ple_residual(hb, w_ple_gate.astype(BF16), b_ple_gate, p2, w_ple.astype(BF16), h, alpha)
    return _combine_ln(pos, y, z, ln2_g, ln2_b)


def kernel(x, p, w_in, w_pool, pool_scale, cmp_k_pe, cmp_k_w1, cmp_k_w2, cmp_v_pe, cmp_v_w1, cmp_v_w2, w_out, ln1_g, ln1_b, w_router, b_router, w_up, b_up, w_down, b_down, w_ple_gate, b_ple_gate, w_ple, ln2_g, ln2_b):
    B, S, D = x.shape
    depth = w_in.shape[0]
    alpha = (2 * depth) ** 0.25
    x2 = x.reshape(B * S, D)
    for i in range(depth):
        x2 = _layer(x2, p[i].reshape(B * S, -1), B, S, w_in[i], w_pool[i], pool_scale[i],
                    cmp_k_pe[i], cmp_k_w1[i], cmp_k_w2[i], cmp_v_pe[i], cmp_v_w1[i], cmp_v_w2[i],
                    w_out[i], ln1_g[i], ln1_b[i], w_router[i], b_router[i], w_up[i], b_up[i],
                    w_down[i], b_down[i], w_ple_gate[i], b_ple_gate[i], w_ple[i], ln2_g[i], ln2_b[i], alpha)
    return x2.reshape(B, S, D)
```

```python
import functools
import math

import jax
import jax.numpy as jnp
import numpy as np
from jax import lax
from jax.experimental import pallas as pl
from jax.experimental.pallas import tpu as pltpu

HEAD_DIM = 128
LANES = 128
N_KV_GROUPS_FIELDS = 6
N_GATES = 3
POOL_WINDOWS = (2, 4, 8, 16)
CMP_BLOCK = 32
CMP_STRIDE = 16
SEL_BLOCK = 64
N_SEL = 16
WINDOW = 512
ROPE_THETA = 10000.0
TOP_K = 4
SWIGLU_LIMIT = 7.0
SWIGLU_ALPHA = 1.702
EXPERT_ROW_BLOCK = 256
LN_EPS = 1e-5
NEG_INF = -1e30
FORCE_SCORE = 1e4
LOG2E = 1.4426950408889634
ISSUE_UNROLL = 8
GATHER_AHEAD = 2
NBUF = GATHER_AHEAD + 1
MIB = 1024 * 1024

BF16 = jnp.bfloat16
F32 = jnp.float32
NT_DIMS = (((1,), (1,)), ((), ()))


def _params(sem, vmem_mib=None):
    kw = dict(dimension_semantics=sem)
    if vmem_mib is not None:
        kw["vmem_limit_bytes"] = vmem_mib * MIB
    return pltpu.CompilerParams(**kw)


def _pick_tile(n, candidates):
    for c in candidates:
        if n % c == 0:
            return c
    raise ValueError(f"no tile for {n}")


def _mm_kernel(x_ref, w_ref, o_ref):
    @pl.when(pl.program_id(2) == 0)
    def _():
        o_ref[...] = jnp.zeros_like(o_ref)

    o_ref[...] += jnp.dot(x_ref[...], w_ref[...], preferred_element_type=F32)


def _matmul_cols(x, w, col0, ncols):
    M, K = x.shape
    tm = _pick_tile(M, (1024, 512, 256))
    tk = _pick_tile(K, (2048, 1024, 512, 256, 128))
    tn = _pick_tile(math.gcd(col0, ncols) if col0 else ncols, (1024, 512, 256, 128))
    off = col0 // tn
    return pl.pallas_call(
        _mm_kernel,
        name="in_proj",
        grid=(M // tm, ncols // tn, K // tk),
        in_specs=[pl.BlockSpec((tm, tk), lambda i, j, k: (i, k)),
                  pl.BlockSpec((tk, tn), lambda i, j, k: (k, j + off))],
        out_specs=pl.BlockSpec((tm, tn), lambda i, j, k: (i, j)),
        out_shape=jax.ShapeDtypeStruct((M, ncols), F32),
        compiler_params=_params(("parallel", "parallel", "arbitrary"), 40),
    )(x, w)


def _pool_kernel(u_ref, prev_ref, w_ref, sc_ref, o_ref, ext_ref, *, ts, halo):
    g = pl.program_id(0)
    i = pl.program_id(2)
    cur = u_ref[...]
    ext_ref[pl.ds(halo, ts), :] = cur

    @pl.when(i == 0)
    def _():
        ext_ref[pl.ds(0, halo), :] = jnp.zeros((halo, cur.shape[1]), F32)

    @pl.when(i > 0)
    def _():
        ext_ref[pl.ds(0, halo), :] = prev_ref[...]

    def back(d):
        return ext_ref[pl.ds(halo - d, ts), :]

    s2 = cur + back(1)
    s4 = s2 + back(2) + back(3)
    s8 = s4 + back(4) + back(5) + back(6) + back(7)
    s16 = s8
    for d in range(8, 16):
        s16 = s16 + back(d)
    ssum = jnp.where(g == 0, s2, jnp.where(g == 1, s4, jnp.where(g == 2, s8, s16)))
    win = jnp.left_shift(2, g)
    t = i * ts + lax.broadcasted_iota(jnp.int32, (ts, 1), 0)
    cnt = jnp.minimum(t + 1, win).astype(F32)
    pooled = ssum / cnt - cur
    mixed = jnp.dot(pooled.astype(BF16), w_ref[0], preferred_element_type=F32)
    o_ref[...] = (mixed * sc_ref[...]).astype(o_ref.dtype)


def _pool_mixer(proj_a, w_pool_b, pool_scale, B, S, pool_w):
    T = B * S
    ng = len(POOL_WINDOWS)
    pg = pool_w // ng
    halo = POOL_WINDOWS[-1]
    ts = _pick_tile(S, (512, 256, 128))
    ns = S // ts
    hb = ts // halo
    kern = functools.partial(_pool_kernel, ts=ts, halo=halo)
    return pl.pallas_call(
        kern,
        name="pool_mixer",
        grid=(ng, B, ns),
        in_specs=[
            pl.BlockSpec((ts, pg), lambda g, b, i: (b * ns + i, g)),
            pl.BlockSpec((halo, pg), lambda g, b, i: (jnp.maximum((b * ns + i) * hb - 1, 0), g)),
            pl.BlockSpec((1, pg, pg), lambda g, b, i: (g, 0, 0)),
            pl.BlockSpec((1, pg), lambda g, b, i: (0, g)),
        ],
        out_specs=pl.BlockSpec((ts, pg), lambda g, b, i: (b * ns + i, g)),
        out_shape=jax.ShapeDtypeStruct((T, pool_w), BF16),
        scratch_shapes=[pltpu.VMEM((ts + halo, pg), F32)],
        compiler_params=_params(("parallel", "parallel", "arbitrary"), 32),
    )(proj_a, proj_a, w_pool_b, pool_scale.reshape(1, pool_w))


def _rope_tables(pos):
    inv = ROPE_THETA ** (-jnp.arange(0, HEAD_DIM, 2, dtype=F32) / HEAD_DIM)
    ang = pos.astype(F32)[:, None] * inv[None, :]
    c, s = jnp.cos(ang), jnp.sin(ang)
    return jnp.concatenate([c, c], axis=-1), jnp.concatenate([-s, s], axis=-1)


def _rope(x, cos2, sin2):
    return x * cos2 + pltpu.roll(x, HEAD_DIM // 2, 1) * sin2


def _compress_kernel(x_ref, pe_ref, w1_ref, w2_ref, cos_ref, sin_ref, o_ref, *, rope, nl):
    nc = x_ref.shape[0] // nl
    xs = [x_ref[pl.ds(l, nc, stride=nl), :] for l in range(nl)]
    lo = jnp.concatenate([(xs[l] + pe_ref[l:l + 1, :]).astype(BF16) for l in range(nl)], axis=1)
    hi = jnp.concatenate([(xs[l] + pe_ref[nl + l:nl + l + 1, :]).astype(BF16) for l in range(nl)], axis=1)
    half = nl * HEAD_DIM
    a = jnp.dot(lo, w1_ref[0:half, :], preferred_element_type=F32)
    b = jnp.dot(hi, w1_ref[half:2 * half, :], preferred_element_type=F32)
    hid = jax.nn.gelu(a + pltpu.roll(b, nc - 1, 0))
    out = jnp.dot(hid.astype(BF16), w2_ref[...], preferred_element_type=F32)
    if rope:
        out = _rope(out, cos_ref[...], sin_ref[...])
    o_ref[0, 0] = out.astype(o_ref.dtype)


def _compress(proj_b, field, pe, w1, w2, cos2, sin2, B, S, G, rope):
    nl = CMP_STRIDE
    nc = S // nl
    kern = functools.partial(_compress_kernel, rope=rope, nl=nl)
    const = lambda b, g: (0, 0)
    return pl.pallas_call(
        kern,
        name="compress_k" if rope else "compress_v",
        grid=(B, G),
        in_specs=[
            pl.BlockSpec((S, HEAD_DIM), lambda b, g: (b, field * G + g)),
            pl.BlockSpec((CMP_BLOCK, HEAD_DIM), const),
            pl.BlockSpec((CMP_BLOCK * HEAD_DIM, HEAD_DIM), const),
            pl.BlockSpec((HEAD_DIM, HEAD_DIM), const),
            pl.BlockSpec((nc, HEAD_DIM), const),
            pl.BlockSpec((nc, HEAD_DIM), const),
        ],
        out_specs=pl.BlockSpec((1, 1, nc, HEAD_DIM), lambda b, g: (b, g, 0, 0)),
        out_shape=jax.ShapeDtypeStruct((B, G, nc, HEAD_DIM), BF16),
        compiler_params=_params(("parallel", "parallel"), 32),
    )(proj_b, pe, w1.reshape(CMP_BLOCK * HEAD_DIM, HEAD_DIM).astype(BF16), w2.astype(BF16), cos2, sin2)


def _rope_prep_kernel(q_ref, kv_ref, cos_ref, sin_ref, qo_ref, ks_ref, vs_ref, kw_ref, vw_ref, *, nh, G, ts):
    i = pl.program_id(1)
    c = cos_ref[...]
    s = sin_ref[...]
    kvw = G * HEAD_DIM
    for h in range(nh):
        qo_ref[0, h] = _rope(q_ref[:, h * HEAD_DIM:(h + 1) * HEAD_DIM], c, s).astype(BF16)
    blk = (i * ts + lax.broadcasted_iota(jnp.int32, (ts, LANES), 0)) // SEL_BLOCK
    onehot = (blk == lax.broadcasted_iota(jnp.int32, (ts, LANES), 1)).astype(BF16)
    for g in range(G):
        lo = g * HEAD_DIM
        ks_ref[0, g, :, 0:HEAD_DIM] = _rope(kv_ref[:, lo:lo + HEAD_DIM], c, s).astype(BF16)
        ks_ref[0, g, :, HEAD_DIM:2 * HEAD_DIM] = onehot
        vs_ref[0, g] = kv_ref[:, kvw + lo:kvw + lo + HEAD_DIM].astype(BF16)
        kw_ref[0, g] = _rope(kv_ref[:, 2 * kvw + lo:2 * kvw + lo + HEAD_DIM], c, s).astype(BF16)
        vw_ref[0, g] = kv_ref[:, 3 * kvw + lo:3 * kvw + lo + HEAD_DIM].astype(BF16)


def _rope_prep(proj_a, proj_c, cos2, sin2, B, S, nsa_w, G):
    nh = nsa_w // HEAD_DIM
    ts = _pick_tile(S, (256, 128))
    ns = S // ts
    kern = functools.partial(_rope_prep_kernel, nh=nh, G=G, ts=ts)
    hm = lambda b, i: (b, 0, i, 0)
    return pl.pallas_call(
        kern,
        name="rope_prep",
        grid=(B, ns),
        in_specs=[
            pl.BlockSpec((ts, nsa_w), lambda b, i: (b * ns + i, 1)),
            pl.BlockSpec((ts, 4 * G * HEAD_DIM), lambda b, i: (b * ns + i, 0)),
            pl.BlockSpec((ts, HEAD_DIM), lambda b, i: (i, 0)),
            pl.BlockSpec((ts, HEAD_DIM), lambda b, i: (i, 0)),
        ],
        out_specs=[
            pl.BlockSpec((1, nh, ts, HEAD_DIM), hm),
            pl.BlockSpec((1, G, ts, 2 * HEAD_DIM), hm),
            pl.BlockSpec((1, G, ts, HEAD_DIM), hm),
            pl.BlockSpec((1, G, ts, HEAD_DIM), hm),
            pl.BlockSpec((1, G, ts, HEAD_DIM), hm),
        ],
        out_shape=[
            jax.ShapeDtypeStruct((B, nh, S, HEAD_DIM), BF16),
            jax.ShapeDtypeStruct((B, G, S, 2 * HEAD_DIM), BF16),
            jax.ShapeDtypeStruct((B, G, S, HEAD_DIM), BF16),
            jax.ShapeDtypeStruct((B, G, S, HEAD_DIM), BF16),
            jax.ShapeDtypeStruct((B, G, S, HEAD_DIM), BF16),
        ],
        compiler_params=_params(("parallel", "parallel"), 40),
    )(proj_a, proj_c, cos2, sin2)


def _cmp_select_kernel(q_ref, kc_ref, vc_ref, ovl_ref, o_ref, bias_ref, *, hg, tq, n_sel, n_keep, scale):
    i = pl.program_id(2)
    nc = kc_ref.shape[2]
    kc = kc_ref[0, 0]
    vc = vc_ref[0, 0]
    t = i * tq + lax.broadcasted_iota(jnp.int32, (tq, nc), 0)
    n = lax.broadcasted_iota(jnp.int32, (tq, nc), 1)
    valid = (n * CMP_STRIDE + (CMP_BLOCK - 1) <= t) & (n < nc - 1)
    ovl = ovl_ref[...]
    imp = jnp.zeros((n_sel, tq), F32)
    for h in range(hg):
        s = lax.dot_general(q_ref[0, h], kc, NT_DIMS, preferred_element_type=F32) * scale
        s = jnp.where(valid, s, NEG_INF)
        m = jnp.max(s, axis=-1, keepdims=True)
        e = jnp.where(valid, jnp.exp(s - m), 0.0)
        den = jnp.sum(e, axis=-1, keepdims=True)
        pc = (e / jnp.where(den > 0.0, den, 1.0)).astype(BF16)
        o_ref[0, h] = jnp.dot(pc, vc, preferred_element_type=F32)
        imp = imp + lax.dot_general(ovl, pc, NT_DIMS, preferred_element_type=F32)
    j = lax.broadcasted_iota(jnp.int32, (n_sel, tq), 0)
    tb = (i * tq + lax.broadcasted_iota(jnp.int32, (n_sel, tq), 1)) // SEL_BLOCK
    forced = (j == 0) | (j == tb) | (j == tb - 1)
    score = jnp.where(forced, FORCE_SCORE, jnp.where(j <= tb, imp, -1.0))
    rank = jnp.zeros((n_sel, tq), jnp.int32)
    for jp in range(n_sel):
        row = score[jp:jp + 1, :]
        ahead = (row > score) | ((row == score) & (jp < j))
        rank = rank + ahead.astype(jnp.int32)
    bias_t = jnp.where(rank < n_keep, 0.0, NEG_INF)
    if n_sel < LANES:
        bias_t = jnp.concatenate([bias_t, jnp.zeros((LANES - n_sel, tq), F32)], axis=0)
    bias_ref[0, 0] = bias_t.T.astype(BF16)


def _overlap_t(n_cmp_pad, n_sel):
    n_cmp = n_cmp_pad - 1
    cs = np.arange(n_cmp) * CMP_STRIDE
    ce = cs + CMP_BLOCK
    ss = np.arange(n_sel) * SEL_BLOCK
    se = ss + SEL_BLOCK
    ov = np.clip(np.minimum(ce[:, None], se[None, :]) - np.maximum(cs[:, None], ss[None, :]), 0, None)
    out = np.zeros((n_sel, n_cmp_pad), np.float32)
    out[:, :n_cmp] = (ov / CMP_BLOCK).T
    return jnp.asarray(out, dtype=BF16)


def _cmp_select(q_rot, kc, vc, B, S, G, hg):
    nc = kc.shape[2]
    n_sel = S // SEL_BLOCK
    assert n_sel <= LANES and n_sel % 8 == 0
    tq = _pick_tile(S, (256, 128))
    nq = S // tq
    nh = G * hg
    kern = functools.partial(_cmp_select_kernel, hg=hg, tq=tq, n_sel=n_sel,
                             n_keep=min(N_SEL, n_sel), scale=HEAD_DIM ** -0.5)
    return pl.pallas_call(
        kern,
        name="cmp_attn_select",
        grid=(B, G, nq),
        in_specs=[
            pl.BlockSpec((1, hg, tq, HEAD_DIM), lambda b, g, i: (b, g, i, 0)),
            pl.BlockSpec((1, 1, nc, HEAD_DIM), lambda b, g, i: (b, g, 0, 0)),
            pl.BlockSpec((1, 1, nc, HEAD_DIM), lambda b, g, i: (b, g, 0, 0)),
            pl.BlockSpec((n_sel, nc), lambda b, g, i: (0, 0)),
        ],
        out_specs=[
            pl.BlockSpec((1, hg, tq, HEAD_DIM), lambda b, g, i: (b, g, i, 0)),
            pl.BlockSpec((1, 1, tq, LANES), lambda b, g, i: (b, g, i, 0)),
        ],
        out_shape=[
            jax.ShapeDtypeStruct((B, nh, S, HEAD_DIM), F32),
            jax.ShapeDtypeStruct((B, G, S, LANES), BF16),
        ],
        compiler_params=_params(("parallel", "parallel", "parallel"), 40),
    )(q_rot, kc, vc, _overlap_t(nc, n_sel))


def _lane_fold(x, op):
    r = x[:, :LANES]
    for a in range(1, x.shape[1] // LANES):
        r = op(r, x[:, a * LANES:(a + 1) * LANES])
    return r


def _slc_kernel(q_ref, bias_ref, k_ref, v_ref, o_ref, s_ref, m_ref, l_ref, acc_ref, *, hg, tq, scale):
    i = pl.program_id(2)
    rows = hg * tq
    c = scale * LOG2E
    q = q_ref[0].reshape(rows, HEAD_DIM)
    b = bias_ref[0, 0]
    qa = jnp.concatenate([q, jnp.concatenate([b] * hg, axis=0)], axis=1)

    def k_tile(j):
        return k_ref[0, 0, pl.ds(pl.multiple_of(j * tq, tq), tq), :]

    def v_tile(j):
        return v_ref[0, 0, pl.ds(pl.multiple_of(j * tq, tq), tq), :]

    m_ref[...] = jnp.full_like(m_ref, NEG_INF)

    def scores(j, carry):
        s = lax.dot_general(qa, k_tile(j), NT_DIMS, preferred_element_type=F32)
        s_ref[j] = s
        m_ref[...] = jnp.maximum(m_ref[...], _lane_fold(s, jnp.maximum))
        return carry

    lax.fori_loop(0, i, scores, 0)
    sd = lax.dot_general(qa, k_tile(i), NT_DIMS, preferred_element_type=F32)
    tpos = lax.broadcasted_iota(jnp.int32, (rows, tq), 0) & (tq - 1)
    kpos = lax.broadcasted_iota(jnp.int32, (rows, tq), 1)
    sd = jnp.where(kpos <= tpos, sd, NEG_INF)
    mrow = jnp.max(jnp.maximum(m_ref[...], _lane_fold(sd, jnp.maximum)), axis=-1, keepdims=True) * c
    m_ref[...] = jnp.broadcast_to(mrow, (rows, LANES))
    l_ref[...] = jnp.zeros_like(l_ref)
    acc_ref[...] = jnp.zeros_like(acc_ref)

    def probs(s):
        mfull = jnp.concatenate([m_ref[...]] * (tq // LANES), axis=1)
        return jnp.exp2(s * c - mfull)

    def accumulate(j, carry):
        p = probs(s_ref[j])
        l_ref[...] += _lane_fold(p, jnp.add)
        acc_ref[...] += jnp.dot(p.astype(BF16), v_tile(j), preferred_element_type=F32)
        return carry

    lax.fori_loop(0, i, accumulate, 0)
    p = probs(sd)
    den = jnp.sum(l_ref[...] + _lane_fold(p, jnp.add), axis=-1, keepdims=True)
    acc = acc_ref[...] + jnp.dot(p.astype(BF16), v_tile(i), preferred_element_type=F32)
    o_ref[0] = (acc / den).reshape(hg, tq, HEAD_DIM)


def _selected_attention(q_rot, k_aug, v, bias, B, S, G, hg, tq):
    assert tq & (tq - 1) == 0 and tq % LANES == 0
    nq = S // tq
    rows = hg * tq
    kern = functools.partial(_slc_kernel, hg=hg, tq=tq, scale=HEAD_DIM ** -0.5)
    return pl.pallas_call(
        kern,
        name="selected_attn",
        grid=(B, G, nq),
        in_specs=[
            pl.BlockSpec((1, hg, tq, HEAD_DIM), lambda b, g, i: (b, g, i, 0)),
            pl.BlockSpec((1, 1, tq, LANES), lambda b, g, i: (b, g, i, 0)),
            pl.BlockSpec((1, 1, S, 2 * HEAD_DIM), lambda b, g, i: (b, g, 0, 0)),
            pl.BlockSpec((1, 1, S, HEAD_DIM), lambda b, g, i: (b, g, 0, 0)),
        ],
        out_specs=pl.BlockSpec((1, hg, tq, HEAD_DIM), lambda b, g, i: (b, g, i, 0)),
        out_shape=jax.ShapeDtypeStruct((B, G * hg, S, HEAD_DIM), F32),
        scratch_shapes=[pltpu.VMEM((nq, rows, tq), F32), pltpu.VMEM((rows, LANES), F32),
                        pltpu.VMEM((rows, LANES), F32), pltpu.VMEM((rows, HEAD_DIM), F32)],
        compiler_params=_params(("parallel", "parallel", "arbitrary"), 48),
    )(q_rot, bias, k_aug, v)


def _win_kernel(q_ref, g_ref, oc_ref, os_ref, *refs, hg, tq, nprev, window, scale):
    nt = nprev + 1
    k_refs, v_refs, y_ref = refs[:nt], refs[nt:2 * nt], refs[2 * nt]
    i = pl.program_id(2)
    gate = jax.nn.sigmoid(g_ref[0])
    c = scale * LOG2E
    nk = nt * tq
    kcat = jnp.concatenate([r[0, 0] for r in k_refs], axis=0)
    vcat = jnp.concatenate([r[0, 0] for r in v_refs], axis=0)
    tpos = i * tq + lax.broadcasted_iota(jnp.int32, (tq, nk), 0)
    kpos = (i - nprev) * tq + lax.broadcasted_iota(jnp.int32, (tq, nk), 1)
    mask = (kpos <= tpos) & (kpos > tpos - window) & (kpos >= 0)
    for h in range(hg):
        s = lax.dot_general(q_ref[0, h], kcat, NT_DIMS, preferred_element_type=F32)
        s = jnp.where(mask, s, NEG_INF)
        m = jnp.max(s, axis=-1, keepdims=True)
        p = jnp.exp2((s - m) * c)
        den = jnp.sum(p, axis=-1, keepdims=True)
        o_win = jnp.dot(p.astype(BF16), vcat, preferred_element_type=F32) / den
        gc = N_GATES * h
        y = (gate[:, gc:gc + 1] * oc_ref[0, h] + gate[:, gc + 1:gc + 2] * os_ref[0, h]
             + gate[:, gc + 2:gc + 3] * o_win)
        y_ref[:, h * HEAD_DIM:(h + 1) * HEAD_DIM] = y.astype(y_ref.dtype)


def _window_attention_gated(q_rot, k, v, gates_g, o_cmp, o_slc, B, S, G, hg, tq, window):
    assert window % tq == 0
    nprev = window // tq
    nq = S // tq
    kern = functools.partial(_win_kernel, hg=hg, tq=tq, nprev=nprev, window=window, scale=HEAD_DIM ** -0.5)

    def kv_spec(a):
        return pl.BlockSpec((1, 1, tq, HEAD_DIM), lambda b, g, i: (b, g, jnp.maximum(i - nprev + a, 0), 0))

    kv_specs = [kv_spec(a) for a in range(nprev + 1)]
    heads = pl.BlockSpec((1, hg, tq, HEAD_DIM), lambda b, g, i: (b, g, i, 0))
    return pl.pallas_call(
        kern,
        name="window_attn_gated_sum",
        grid=(B, G, nq),
        in_specs=[heads, pl.BlockSpec((1, tq, LANES), lambda b, g, i: (g, b * nq + i, 0)), heads, heads]
        + kv_specs + kv_specs,
        out_specs=pl.BlockSpec((tq, hg * HEAD_DIM), lambda b, g, i: (b * nq + i, g)),
        out_shape=jax.ShapeDtypeStruct((B * S, G * hg * HEAD_DIM), BF16),
        compiler_params=_params(("parallel", "parallel", "arbitrary"), 40),
    )(q_rot, gates_g, o_cmp, o_slc, *([k] * (nprev + 1)), *([v] * (nprev + 1)))


def _layer_norm(z, g, b):
    mu = jnp.mean(z, axis=-1, keepdims=True)
    zc = z - mu
    var = jnp.mean(zc * zc, axis=-1, keepdims=True)
    return zc * lax.rsqrt(var + LN_EPS) * g + b


def _mix_kernel(yp_ref, yn_ref, w_ref, x_ref, z_ref, *, nkp, alpha):
    k = pl.program_id(2)

    @pl.when(k == 0)
    def _():
        z_ref[...] = alpha * x_ref[...]

    @pl.when(k < nkp)
    def _():
        z_ref[...] += jnp.dot(yp_ref[...], w_ref[...], preferred_element_type=F32)

    @pl.when(k >= nkp)
    def _():
        z_ref[...] += jnp.dot(yn_ref[...], w_ref[...], preferred_element_type=F32)


def _ln_router_kernel(z_ref, g_ref, b_ref, wr_ref, br_ref, h_ref, hb_ref, lg_ref):
    h = _layer_norm(z_ref[...], g_ref[...], b_ref[...])
    hb = h.astype(BF16)
    h_ref[...] = h
    hb_ref[...] = hb
    lg_ref[...] = jnp.dot(hb, wr_ref[...], preferred_element_type=F32) + br_ref[...]


def _mix_ln_router(y_pool, y_nsa, w_out_b, x2, ln_g, ln_b, w_router_p, b_router_p, alpha):
    T, D = x2.shape
    kp = y_pool.shape[1]
    tm = _pick_tile(T, (1024, 512, 256, 128))
    tn = _pick_tile(D, (1024, 512, 256, 128))
    tk = _pick_tile(math.gcd(kp, y_nsa.shape[1]), (1024, 512, 256, 128))
    nkp = kp // tk
    nk = nkp + y_nsa.shape[1] // tk
    z = pl.pallas_call(
        functools.partial(_mix_kernel, nkp=nkp, alpha=alpha),
        name="out_proj_residual",
        grid=(T // tm, D // tn, nk),
        in_specs=[
            pl.BlockSpec((tm, tk), lambda i, j, k: (i, jnp.minimum(k, nkp - 1))),
            pl.BlockSpec((tm, tk), lambda i, j, k: (i, jnp.maximum(k - nkp, 0))),
            pl.BlockSpec((tk, tn), lambda i, j, k: (k, j)),
            pl.BlockSpec((tm, tn), lambda i, j, k: (i, j)),
        ],
        out_specs=pl.BlockSpec((tm, tn), lambda i, j, k: (i, j)),
        out_shape=jax.ShapeDtypeStruct((T, D), F32),
        compiler_params=_params(("parallel", "parallel", "arbitrary"), 40),
    )(y_pool, y_nsa, w_out_b, x2)

    ne = w_router_p.shape[1]
    tr = _pick_tile(T, (256, 128))
    row = lambda i: (i, 0)
    const = lambda i: (0, 0)
    return pl.pallas_call(
        _ln_router_kernel,
        name="ln1_router",
        grid=(T // tr,),
        in_specs=[
            pl.BlockSpec((tr, D), row),
            pl.BlockSpec((1, D), const),
            pl.BlockSpec((1, D), const),
            pl.BlockSpec((D, ne), const),
            pl.BlockSpec((1, ne), const),
        ],
        out_specs=[pl.BlockSpec((tr, D), row), pl.BlockSpec((tr, D), row), pl.BlockSpec((tr, ne), row)],
        out_shape=[jax.ShapeDtypeStruct((T, D), F32), jax.ShapeDtypeStruct((T, D), BF16),
                   jax.ShapeDtypeStruct((T, ne), F32)],
        compiler_params=_params(("parallel",), 48),
    )(z, ln_g.reshape(1, D), ln_b.reshape(1, D), w_router_p, b_router_p)


def _row_copy(src_hbm, row, dst_ref, slot, sem):
    return pltpu.make_async_copy(src_hbm.at[pl.ds(row, 1)], dst_ref.at[pl.ds(slot, 1)], sem)


def _rows_copy(src_hbm, dst_ref, sem, n):
    return pltpu.make_async_copy(src_hbm.at[pl.ds(0, n)], dst_ref, sem)


PAIR = 2 * LANES


def _pair_split_kernel(w_ref, perm_ref, o_ref):
    perm = perm_ref[...]
    for c in range(w_ref.shape[2] // PAIR):
        sl = slice(c * PAIR, (c + 1) * PAIR)
        o_ref[0, :, sl] = jnp.dot(w_ref[0, :, sl].astype(BF16), perm, preferred_element_type=F32).astype(BF16)


def _pair_split_cast(w_up):
    E, D, F2 = w_up.shape
    assert F2 % PAIR == 0
    td = _pick_tile(D, (512, 256, 128))
    perm = np.zeros((PAIR, PAIR), np.float32)
    perm[2 * np.arange(LANES), np.arange(LANES)] = 1.0
    perm[2 * np.arange(LANES) + 1, LANES + np.arange(LANES)] = 1.0
    return pl.pallas_call(
        _pair_split_kernel,
        name="w_up_pair_split",
        grid=(E, D // td),
        in_specs=[pl.BlockSpec((1, td, F2), lambda e, i: (e, i, 0)),
                  pl.BlockSpec((PAIR, PAIR), lambda e, i: (0, 0))],
        out_specs=pl.BlockSpec((1, td, F2), lambda e, i: (e, i, 0)),
        out_shape=jax.ShapeDtypeStruct((E, D, F2), BF16),
        compiler_params=_params(("parallel", "parallel"), 32),
    )(w_up, jnp.asarray(perm, dtype=BF16))


def _moe_up_kernel(tok_ref, be_ref, bv_ref, h_hbm, w_ref, b_ref, act_ref, xbuf_ref, sem, *, rb, nb):
    i = pl.program_id(0)

    def gather(blk, slot):
        def issue(r, carry):
            _row_copy(h_hbm, tok_ref[blk * rb + r], xbuf_ref.at[slot], r, sem.at[slot]).start()
            return carry

        lax.fori_loop(0, rb, issue, 0, unroll=ISSUE_UNROLL)

    def drain(slot):
        _rows_copy(h_hbm, xbuf_ref.at[slot], sem.at[slot], rb).wait()

    @pl.when((i == 0) & (bv_ref[0] != 0))
    def _():
        for a in range(GATHER_AHEAD):
            gather(a, a)

    slot = i % NBUF
    valid = bv_ref[i] != 0
    started = bv_ref[jnp.maximum(i - GATHER_AHEAD, 0)] != 0

    @pl.when(jnp.logical_not(valid))
    def _():
        act_ref[...] = jnp.zeros_like(act_ref)

    @pl.when(jnp.logical_not(valid) & started)
    def _():
        drain(slot)

    @pl.when(valid)
    def _():
        drain(slot)
        xb = xbuf_ref[slot].astype(BF16)
        nxt = jnp.minimum(i + GATHER_AHEAD, nb - 1)
        nslot = (i + GATHER_AHEAD) % NBUF
        for r in range(rb):
            _row_copy(h_hbm, tok_ref[nxt * rb + r], xbuf_ref.at[nslot], r, sem.at[nslot]).start()
        hcat = jnp.dot(xb, w_ref[0], preferred_element_type=F32) + b_ref[0]
        for c in range(hcat.shape[1] // PAIR):
            gate = jnp.minimum(hcat[:, c * PAIR:c * PAIR + LANES], SWIGLU_LIMIT)
            up = jnp.clip(hcat[:, c * PAIR + LANES:(c + 1) * PAIR], -SWIGLU_LIMIT, SWIGLU_LIMIT)
            act = gate * jax.nn.sigmoid(SWIGLU_ALPHA * gate) * (up + 1.0)
            act_ref[:, c * LANES:(c + 1) * LANES] = act.astype(act_ref.dtype)

    for a in range(GATHER_AHEAD):
        j = nb - 1 - a

        @pl.when((i == nb - 1) & (bv_ref[j] != 0))
        def _():
            drain((j + GATHER_AHEAD) % NBUF)


def _moe_up(row_tok, blk_e, blk_valid, h, w_pairs, b_pairs):
    n_rows = row_tok.shape[0]
    rb = EXPERT_ROW_BLOCK
    nb = n_rows // rb
    assert nb > GATHER_AHEAD
    E, D, F2 = w_pairs.shape
    wmap = lambda i, tok, be, bv: (be[i], 0, 0)
    return pl.pallas_call(
        functools.partial(_moe_up_kernel, rb=rb, nb=nb),
        name="moe_gather_up_swiglu",
        grid_spec=pltpu.PrefetchScalarGridSpec(
            num_scalar_prefetch=3,
            grid=(n_rows // rb,),
            in_specs=[
                pl.BlockSpec(memory_space=pl.ANY),
                pl.BlockSpec((1, D, F2), wmap),
                pl.BlockSpec((1, 1, F2), wmap),
            ],
            out_specs=pl.BlockSpec((rb, F2 // 2), lambda i, tok, be, bv: (i, 0)),
            scratch_shapes=[pltpu.VMEM((NBUF, rb, D), F32), pltpu.SemaphoreType.DMA((NBUF,))],
        ),
        out_shape=jax.ShapeDtypeStruct((n_rows, F2 // 2), BF16),
        compiler_params=_params(("arbitrary",), 56),
    )(row_tok, blk_e, blk_valid, h, w_pairs, b_pairs)


def _moe_down_kernel(be_ref, bv_ref, act_ref, wd_ref, bd_ref, rw_ref, y_ref, wb_ref):
    i = pl.program_id(0)

    @pl.when(bv_ref[i] == 0)
    def _():
        y_ref[...] = jnp.zeros_like(y_ref)

    @pl.when((bv_ref[i] != 0) & ((i == 0) | (be_ref[i] != be_ref[jnp.maximum(i - 1, 0)])))
    def _():
        wb_ref[...] = wd_ref[0].astype(BF16)

    @pl.when(bv_ref[i] != 0)
    def _():
        y = (jnp.dot(act_ref[...], wb_ref[...], preferred_element_type=F32) + bd_ref[0]) * rw_ref[...]
        bits = pltpu.bitcast(y.astype(BF16).astype(F32), jnp.uint32)
        half = bits.shape[1] // 2
        y_ref[...] = (bits[:, :half] >> 16) | (bits[:, half:] & jnp.uint32(0xFFFF0000))


def _moe_down(blk_e, blk_valid, act, wd, bd, row_w):
    n_rows, F = act.shape
    rb = EXPERT_ROW_BLOCK
    E, _, D = wd.shape
    wmap = lambda i, be, bv: (be[i], 0, 0)
    return pl.pallas_call(
        _moe_down_kernel,
        name="moe_down",
        grid_spec=pltpu.PrefetchScalarGridSpec(
            num_scalar_prefetch=2,
            grid=(n_rows // rb,),
            in_specs=[
                pl.BlockSpec((rb, F), lambda i, be, bv: (i, 0)),
                pl.BlockSpec((1, F, D), wmap),
                pl.BlockSpec((1, 1, D), wmap),
                pl.BlockSpec((rb, 1), lambda i, be, bv: (i, 0)),
            ],
            out_specs=pl.BlockSpec((rb, D // 2), lambda i, be, bv: (i, 0)),
            scratch_shapes=[pltpu.VMEM((F, D), BF16)],
        ),
        out_shape=jax.ShapeDtypeStruct((n_rows, D // 2), jnp.uint32),
        compiler_params=_params(("arbitrary",), 56),
    )(blk_e, blk_valid, act, wd, bd, row_w.reshape(n_rows, 1))


def _ple_kernel(hb_ref, w_ref, bg_ref, p_ref, wp_ref, h_ref, z_ref, *, alpha):
    k = pl.program_id(2)

    @pl.when(k == 0)
    def _():
        z_ref[...] = jnp.zeros_like(z_ref)

    z_ref[...] += jnp.dot(hb_ref[...], w_ref[...], preferred_element_type=F32)

    @pl.when(k == pl.num_programs(2) - 1)
    def _():
        gate = jax.nn.sigmoid(z_ref[...] + bg_ref[...])
        emb = jnp.dot(p_ref[...].astype(BF16), wp_ref[...], preferred_element_type=F32)
        z_ref[...] = alpha * h_ref[...] + gate * emb


def _ple_residual(hb, w_gate_b, b_gate, p2, w_ple_b, h, alpha):
    T, D = h.shape
    pd = p2.shape[1]
    tm = _pick_tile(T, (1024, 512, 256, 128))
    tn = _pick_tile(D, (1024, 512, 256, 128))
    tk = _pick_tile(D, (2048, 1024, 512, 256, 128))
    return pl.pallas_call(
        functools.partial(_ple_kernel, alpha=alpha),
        name="ple_gate_residual",
        grid=(T // tm, D // tn, D // tk),
        in_specs=[
            pl.BlockSpec((tm, tk), lambda i, j, k: (i, k)),
            pl.BlockSpec((tk, tn), lambda i, j, k: (k, j)),
            pl.BlockSpec((1, tn), lambda i, j, k: (0, j)),
            pl.BlockSpec((tm, pd), lambda i, j, k: (i, 0)),
            pl.BlockSpec((pd, tn), lambda i, j, k: (0, j)),
            pl.BlockSpec((tm, tn), lambda i, j, k: (i, j)),
        ],
        out_specs=pl.BlockSpec((tm, tn), lambda i, j, k: (i, j)),
        out_shape=jax.ShapeDtypeStruct((T, D), F32),
        compiler_params=_params(("parallel", "parallel", "arbitrary"), 48),
    )(hb, w_gate_b, b_gate.reshape(1, D), p2, w_ple_b, h)


def _combine_kernel(pos_ref, y_hbm, z_ref, g_ref, b_ref, o_ref, buf_ref, sem, *, tc, topk, nb):
    i = pl.program_id(0)

    def gather(blk, slot):
        def issue(t, carry):
            for k in range(topk):
                _row_copy(y_hbm, pos_ref[(blk * tc + t) * topk + k], buf_ref.at[slot, k], t, sem.at[slot]).start()
            return carry

        lax.fori_loop(0, tc, issue, 0, unroll=ISSUE_UNROLL // topk)

    def drain(slot):
        for k in range(topk):
            _rows_copy(y_hbm, buf_ref.at[slot, k], sem.at[slot], tc).wait()

    @pl.when(i == 0)
    def _():
        for a in range(GATHER_AHEAD):
            gather(a, a)

    slot = i % NBUF
    drain(slot)
    nxt = jnp.minimum(i + GATHER_AHEAD, nb - 1)
    nslot = (i + GATHER_AHEAD) % NBUF
    for t in range(tc):
        for k in range(topk):
            _row_copy(y_hbm, pos_ref[(nxt * tc + t) * topk + k], buf_ref.at[nslot, k], t, sem.at[nslot]).start()
    lo = hi = None
    for k in range(topk):
        word = buf_ref[slot, k]
        lo_k = pltpu.bitcast(word << 16, F32)
        hi_k = pltpu.bitcast(word & jnp.uint32(0xFFFF0000), F32)
        lo = lo_k if lo is None else lo + lo_k
        hi = hi_k if hi is None else hi + hi_k
    ffn = jnp.concatenate([lo, hi], axis=1)
    o_ref[...] = _layer_norm(z_ref[...] + ffn, g_ref[...], b_ref[...])

    @pl.when(i == nb - 1)
    def _():
        for a in range(GATHER_AHEAD):
            drain((nb - 1 - a + GATHER_AHEAD) % NBUF)


def _combine_ln(pos, y, z, ln_g, ln_b):
    T, D = z.shape
    tc = _pick_tile(T, (128, 64, 32))
    row = lambda i, pos: (i, 0)
    const = lambda i, pos: (0, 0)
    nb = T // tc
    assert nb > GATHER_AHEAD
    return pl.pallas_call(
        functools.partial(_combine_kernel, tc=tc, topk=TOP_K, nb=nb),
        name="moe_combine_ln2",
        grid_spec=pltpu.PrefetchScalarGridSpec(
            num_scalar_prefetch=1,
            grid=(T // tc,),
            in_specs=[
                pl.BlockSpec(memory_space=pl.ANY),
                pl.BlockSpec((tc, D), row),
                pl.BlockSpec((1, D), const),
                pl.BlockSpec((1, D), const),
            ],
            out_specs=pl.BlockSpec((tc, D), row),
            scratch_shapes=[pltpu.VMEM((NBUF, TOP_K, tc, D // 2), jnp.uint32), pltpu.SemaphoreType.DMA((NBUF,))],
        ),
        out_shape=jax.ShapeDtypeStruct((T, D), F32),
        compiler_params=_params(("arbitrary",), 32),
    )(pos, y, z, ln_g.reshape(1, D), ln_b.reshape(1, D))


def _routing_tables(logits, n_experts):
    T = logits.shape[0]
    A = T * TOP_K
    rb = EXPERT_ROW_BLOCK
    i32 = jnp.int32
    top_logit, top_e = lax.top_k(logits[:, :n_experts], TOP_K)
    top_w = jax.nn.softmax(top_logit, axis=-1)
    flat_e = top_e.reshape(-1).astype(i32)
    order = jnp.argsort(flat_e).astype(i32)
    sorted_pos = jnp.argsort(order).astype(i32)
    onehot = (flat_e[:, None] == jnp.arange(n_experts, dtype=i32)[None, :]).astype(i32)
    counts = jnp.sum(onehot, axis=0)
    padded = (counts + rb - 1) // rb * rb
    pad_end = jnp.cumsum(padded)
    pad_start = pad_end - padded
    grp_start = jnp.cumsum(counts) - counts
    pos = (jnp.sum(onehot * (pad_start - grp_start)[None, :], axis=1) + sorted_pos).astype(i32)
    n_blocks = -(-A // rb) + n_experts
    blk_start = jnp.arange(n_blocks, dtype=i32) * rb
    blk_e = jnp.minimum(jnp.sum((pad_end[None, :] <= blk_start[:, None]).astype(i32), axis=1), n_experts - 1)
    blk_valid = (blk_start < pad_end[-1]).astype(i32)
    off = (blk_start - pad_start[blk_e])[:, None] + jnp.arange(rb, dtype=i32)[None, :]
    valid = off < counts[blk_e][:, None]
    src = order[jnp.clip(grp_start[blk_e][:, None] + off, 0, A - 1)]
    row_tok = jnp.where(valid, src // TOP_K, 0).astype(i32).reshape(-1)
    row_w = jnp.where(valid, top_w.reshape(-1)[src], 0.0).reshape(-1)
    return row_tok, row_w, blk_e.astype(i32), blk_valid, pos


def _layer(x2, p2, B, S, w_in, w_pool, pool_scale, cmp_k_pe, cmp_k_w1, cmp_k_w2, cmp_v_pe, cmp_v_w1, cmp_v_w2,
           w_out, ln1_g, ln1_b, w_router, b_router, w_up, b_up, w_down, b_down,
           w_ple_gate, b_ple_gate, w_ple, ln2_g, ln2_b, alpha):
    T, D = x2.shape
    pool_w = D // 2
    nsa_w = D - pool_w
    nh = nsa_w // HEAD_DIM
    in_w = w_in.shape[1]
    kvw = (in_w - D - N_GATES * nh) // N_KV_GROUPS_FIELDS
    G = kvw // HEAD_DIM
    hg = nh // G
    n_experts = w_router.shape[1]

    xb = x2.astype(BF16)
    w_in_b = w_in.astype(BF16)
    proj_a = _matmul_cols(xb, w_in_b, 0, D)
    proj_b = _matmul_cols(xb, w_in_b, D, 2 * kvw)
    proj_c = _matmul_cols(xb, w_in_b, D + 2 * kvw, 4 * kvw)
    w_gates = jnp.pad(w_in_b[:, D + 6 * kvw:], ((0, 0), (0, LANES - N_GATES * nh)))
    gates = _matmul_cols(xb, w_gates, 0, LANES)

    y_pool = _pool_mixer(proj_a, w_pool.astype(BF16), pool_scale, B, S, pool_w)

    cos2, sin2 = _rope_tables(jnp.arange(S))
    nc = S // CMP_STRIDE
    ccos, csin = _rope_tables(jnp.arange(nc) * CMP_STRIDE + CMP_BLOCK - 1)
    kc = _compress(proj_b, 0, cmp_k_pe, cmp_k_w1, cmp_k_w2, ccos, csin, B, S, G, rope=True)
    vc = _compress(proj_b, 1, cmp_v_pe, cmp_v_w1, cmp_v_w2, ccos, csin, B, S, G, rope=False)
    q_rot, k_slc, v_slc, k_win, v_win = _rope_prep(proj_a, proj_c, cos2, sin2, B, S, nsa_w, G)
    o_cmp, sel_bias = _cmp_select(q_rot, kc, vc, B, S, G, hg)
    tq = _pick_tile(S, (256, 128))
    o_slc = _selected_attention(q_rot, k_slc, v_slc, sel_bias, B, S, G, hg, tq)
    gates_g = gates[:, :N_GATES * nh].reshape(T, G, N_GATES * hg).transpose(1, 0, 2)
    gates_g = jnp.pad(gates_g, ((0, 0), (0, 0), (0, LANES - N_GATES * hg)))
    y_nsa = _window_attention_gated(q_rot, k_win, v_win, gates_g, o_cmp, o_slc, B, S, G, hg, tq, WINDOW)

    ne_pad = -(-n_experts // LANES) * LANES
    w_router_p = jnp.pad(w_router, ((0, 0), (0, ne_pad - n_experts))).astype(BF16)
    b_router_p = jnp.pad(b_router, (0, ne_pad - n_experts)).reshape(1, ne_pad)
    h, hb, logits = _mix_ln_router(y_pool, y_nsa, w_out.astype(BF16), x2, ln1_g, ln1_b,
                                   w_router_p, b_router_p, alpha)

    row_tok, row_w, blk_e, blk_valid, pos = _routing_tables(logits, n_experts)
    F = w_down.shape[1]
    w_pairs = _pair_split_cast(w_up)
    b_pairs = b_up.reshape(n_experts, F // LANES, LANES, 2).transpose(0, 1, 3, 2).reshape(n_experts, 1, 2 * F)
    act = _moe_up(row_tok, blk_e, blk_valid, h, w_pairs, b_pairs)
    y = _moe_down(blk_e, blk_valid, act, w_down, b_down.reshape(n_experts, 1, D), row_w)

    z = _ple_residual(hb, w_ple_gate.astype(BF16), b_ple_gate, p2, w_ple.astype(BF16), h, alpha)
    return _combine_ln(pos, y, z, ln2_g, ln2_b)


def kernel(x, p, w_in, w_pool, pool_scale, cmp_k_pe, cmp_k_w1, cmp_k_w2, cmp_v_pe, cmp_v_w1, cmp_v_w2, w_out, ln1_g, ln1_b, w_router, b_router, w_up, b_up, w_down, b_down, w_ple_gate, b_ple_gate, w_ple, ln2_g, ln2_b):
    B, S, D = x.shape
    depth = w_in.shape[0]
    alpha = (2 * depth) ** 0.25
    x2 = x.reshape(B * S, D)
    for i in range(depth):
        x2 = _layer(x2, p[i].reshape(B * S, -1), B, S, w_in[i], w_pool[i], pool_scale[i],
                    cmp_k_pe[i], cmp_k_w1[i], cmp_k_w2[i], cmp_v_pe[i], cmp_v_w1[i], cmp_v_w2[i],
                    w_out[i], ln1_g[i], ln1_b[i], w_router[i], b_router[i], w_up[i], b_up[i],
                    w_down[i], b_down[i], w_ple_gate[i], b_ple_gate[i], w_ple[i], ln2_g[i], ln2_b[i], alpha)
    return x2.reshape(B, S, D)
```

```python
import functools
import math

import jax
import jax.numpy as jnp
import numpy as np
from jax import lax
from jax.experimental import pallas as pl
from jax.experimental.pallas import tpu as pltpu

HEAD_DIM = 128
LANES = 128
N_KV_GROUPS_FIELDS = 6
N_GATES = 3
POOL_WINDOWS = (2, 4, 8, 16)
CMP_BLOCK = 32
CMP_STRIDE = 16
SEL_BLOCK = 64
N_SEL = 16
WINDOW = 512
ROPE_THETA = 10000.0
TOP_K = 4
SWIGLU_LIMIT = 7.0
SWIGLU_ALPHA = 1.702
EXPERT_ROW_BLOCK = 256
LN_EPS = 1e-5
NEG_INF = -1e30
FORCE_SCORE = 1e4
LOG2E = 1.4426950408889634
ISSUE_UNROLL = 8
KEY_TILE_GROUP = 4
GATHER_AHEAD = 2
NBUF = GATHER_AHEAD + 1
MIB = 1024 * 1024

BF16 = jnp.bfloat16
F32 = jnp.float32
NT_DIMS = (((1,), (1,)), ((), ()))


def _params(sem, vmem_mib=None):
    kw = dict(dimension_semantics=sem)
    if vmem_mib is not None:
        kw["vmem_limit_bytes"] = vmem_mib * MIB
    return pltpu.CompilerParams(**kw)


def _pick_tile(n, candidates):
    for c in candidates:
        if n % c == 0:
            return c
    raise ValueError(f"no tile for {n}")


def _mm_kernel(x_ref, w_ref, o_ref):
    @pl.when(pl.program_id(2) == 0)
    def _():
        o_ref[...] = jnp.zeros_like(o_ref)

    o_ref[...] += jnp.dot(x_ref[...], w_ref[...], preferred_element_type=F32)


def _matmul_cols(x, w, col0, ncols):
    M, K = x.shape
    tm = _pick_tile(M, (1024, 512, 256))
    tk = _pick_tile(K, (2048, 1024, 512, 256, 128))
    tn = _pick_tile(math.gcd(col0, ncols) if col0 else ncols, (1024, 512, 256, 128))
    off = col0 // tn
    return pl.pallas_call(
        _mm_kernel,
        name="in_proj",
        grid=(M // tm, ncols // tn, K // tk),
        in_specs=[pl.BlockSpec((tm, tk), lambda i, j, k: (i, k)),
                  pl.BlockSpec((tk, tn), lambda i, j, k: (k, j + off))],
        out_specs=pl.BlockSpec((tm, tn), lambda i, j, k: (i, j)),
        out_shape=jax.ShapeDtypeStruct((M, ncols), F32),
        compiler_params=_params(("parallel", "parallel", "arbitrary"), 40),
    )(x, w)


def _pool_kernel(u_ref, prev_ref, w_ref, sc_ref, o_ref, ext_ref, *, ts, halo):
    g = pl.program_id(0)
    i = pl.program_id(2)
    cur = u_ref[...]
    ext_ref[pl.ds(halo, ts), :] = cur

    @pl.when(i == 0)
    def _():
        ext_ref[pl.ds(0, halo), :] = jnp.zeros((halo, cur.shape[1]), F32)

    @pl.when(i > 0)
    def _():
        ext_ref[pl.ds(0, halo), :] = prev_ref[...]

    def back(d):
        return ext_ref[pl.ds(halo - d, ts), :]

    s2 = cur + back(1)
    s4 = s2 + back(2) + back(3)
    s8 = s4 + back(4) + back(5) + back(6) + back(7)
    s16 = s8
    for d in range(8, 16):
        s16 = s16 + back(d)
    ssum = jnp.where(g == 0, s2, jnp.where(g == 1, s4, jnp.where(g == 2, s8, s16)))
    win = jnp.left_shift(2, g)
    t = i * ts + lax.broadcasted_iota(jnp.int32, (ts, 1), 0)
    cnt = jnp.minimum(t + 1, win).astype(F32)
    pooled = ssum / cnt - cur
    mixed = jnp.dot(pooled.astype(BF16), w_ref[0], preferred_element_type=F32)
    o_ref[...] = (mixed * sc_ref[...]).astype(o_ref.dtype)


def _pool_mixer(proj_a, w_pool_b, pool_scale, B, S, pool_w):
    T = B * S
    ng = len(POOL_WINDOWS)
    pg = pool_w // ng
    halo = POOL_WINDOWS[-1]
    ts = _pick_tile(S, (512, 256, 128))
    ns = S // ts
    hb = ts // halo
    kern = functools.partial(_pool_kernel, ts=ts, halo=halo)
    return pl.pallas_call(
        kern,
        name="pool_mixer",
        grid=(ng, B, ns),
        in_specs=[
            pl.BlockSpec((ts, pg), lambda g, b, i: (b * ns + i, g)),
            pl.BlockSpec((halo, pg), lambda g, b, i: (jnp.maximum((b * ns + i) * hb - 1, 0), g)),
            pl.BlockSpec((1, pg, pg), lambda g, b, i: (g, 0, 0)),
            pl.BlockSpec((1, pg), lambda g, b, i: (0, g)),
        ],
        out_specs=pl.BlockSpec((ts, pg), lambda g, b, i: (b * ns + i, g)),
        out_shape=jax.ShapeDtypeStruct((T, pool_w), BF16),
        scratch_shapes=[pltpu.VMEM((ts + halo, pg), F32)],
        compiler_params=_params(("parallel", "parallel", "arbitrary"), 32),
    )(proj_a, proj_a, w_pool_b, pool_scale.reshape(1, pool_w))


def _rope_tables(pos):
    inv = ROPE_THETA ** (-jnp.arange(0, HEAD_DIM, 2, dtype=F32) / HEAD_DIM)
    ang = pos.astype(F32)[:, None] * inv[None, :]
    c, s = jnp.cos(ang), jnp.sin(ang)
    return jnp.concatenate([c, c], axis=-1), jnp.concatenate([-s, s], axis=-1)


def _rope(x, cos2, sin2):
    return x * cos2 + pltpu.roll(x, HEAD_DIM // 2, 1) * sin2


def _compress_kernel(x_ref, pe_ref, w1_ref, w2_ref, cos_ref, sin_ref, o_ref, *, rope, nl):
    nc = x_ref.shape[0] // nl
    xs = [x_ref[pl.ds(l, nc, stride=nl), :] for l in range(nl)]
    lo = jnp.concatenate([(xs[l] + pe_ref[l:l + 1, :]).astype(BF16) for l in range(nl)], axis=1)
    hi = jnp.concatenate([(xs[l] + pe_ref[nl + l:nl + l + 1, :]).astype(BF16) for l in range(nl)], axis=1)
    half = nl * HEAD_DIM
    a = jnp.dot(lo, w1_ref[0:half, :], preferred_element_type=F32)
    b = jnp.dot(hi, w1_ref[half:2 * half, :], preferred_element_type=F32)
    hid = jax.nn.gelu(a + pltpu.roll(b, nc - 1, 0))
    out = jnp.dot(hid.astype(BF16), w2_ref[...], preferred_element_type=F32)
    if rope:
        out = _rope(out, cos_ref[...], sin_ref[...])
    o_ref[0, 0] = out.astype(o_ref.dtype)


def _compress(proj_b, field, pe, w1, w2, cos2, sin2, B, S, G, rope):
    nl = CMP_STRIDE
    nc = S // nl
    kern = functools.partial(_compress_kernel, rope=rope, nl=nl)
    const = lambda b, g: (0, 0)
    return pl.pallas_call(
        kern,
        name="compress_k" if rope else "compress_v",
        grid=(B, G),
        in_specs=[
            pl.BlockSpec((S, HEAD_DIM), lambda b, g: (b, field * G + g)),
            pl.BlockSpec((CMP_BLOCK, HEAD_DIM), const),
            pl.BlockSpec((CMP_BLOCK * HEAD_DIM, HEAD_DIM), const),
            pl.BlockSpec((HEAD_DIM, HEAD_DIM), const),
            pl.BlockSpec((nc, HEAD_DIM), const),
            pl.BlockSpec((nc, HEAD_DIM), const),
        ],
        out_specs=pl.BlockSpec((1, 1, nc, HEAD_DIM), lambda b, g: (b, g, 0, 0)),
        out_shape=jax.ShapeDtypeStruct((B, G, nc, HEAD_DIM), BF16),
        compiler_params=_params(("parallel", "parallel"), 32),
    )(proj_b, pe, w1.reshape(CMP_BLOCK * HEAD_DIM, HEAD_DIM).astype(BF16), w2.astype(BF16), cos2, sin2)


def _rope_prep_kernel(q_ref, kv_ref, cos_ref, sin_ref, qo_ref, ks_ref, vs_ref, kw_ref, vw_ref, *, nh, G, ts):
    i = pl.program_id(1)
    c = cos_ref[...]
    s = sin_ref[...]
    kvw = G * HEAD_DIM
    for h in range(nh):
        qo_ref[0, h] = _rope(q_ref[:, h * HEAD_DIM:(h + 1) * HEAD_DIM], c, s).astype(BF16)
    blk = (i * ts + lax.broadcasted_iota(jnp.int32, (ts, LANES), 0)) // SEL_BLOCK
    onehot = (blk == lax.broadcasted_iota(jnp.int32, (ts, LANES), 1)).astype(BF16)
    for g in range(G):
        lo = g * HEAD_DIM
        ks_ref[0, g, :, 0:HEAD_DIM] = _rope(kv_ref[:, lo:lo + HEAD_DIM], c, s).astype(BF16)
        ks_ref[0, g, :, HEAD_DIM:2 * HEAD_DIM] = onehot
        vs_ref[0, g] = kv_ref[:, kvw + lo:kvw + lo + HEAD_DIM].astype(BF16)
        kw_ref[0, g] = _rope(kv_ref[:, 2 * kvw + lo:2 * kvw + lo + HEAD_DIM], c, s).astype(BF16)
        vw_ref[0, g] = kv_ref[:, 3 * kvw + lo:3 * kvw + lo + HEAD_DIM].astype(BF16)


def _rope_prep(proj_a, proj_c, cos2, sin2, B, S, nsa_w, G):
    nh = nsa_w // HEAD_DIM
    ts = _pick_tile(S, (256, 128))
    ns = S // ts
    kern = functools.partial(_rope_prep_kernel, nh=nh, G=G, ts=ts)
    hm = lambda b, i: (b, 0, i, 0)
    return pl.pallas_call(
        kern,
        name="rope_prep",
        grid=(B, ns),
        in_specs=[
            pl.BlockSpec((ts, nsa_w), lambda b, i: (b * ns + i, 1)),
            pl.BlockSpec((ts, 4 * G * HEAD_DIM), lambda b, i: (b * ns + i, 0)),
            pl.BlockSpec((ts, HEAD_DIM), lambda b, i: (i, 0)),
            pl.BlockSpec((ts, HEAD_DIM), lambda b, i: (i, 0)),
        ],
        out_specs=[
            pl.BlockSpec((1, nh, ts, HEAD_DIM), hm),
            pl.BlockSpec((1, G, ts, 2 * HEAD_DIM), hm),
            pl.BlockSpec((1, G, ts, HEAD_DIM), hm),
            pl.BlockSpec((1, G, ts, HEAD_DIM), hm),
            pl.BlockSpec((1, G, ts, HEAD_DIM), hm),
        ],
        out_shape=[
            jax.ShapeDtypeStruct((B, nh, S, HEAD_DIM), BF16),
            jax.ShapeDtypeStruct((B, G, S, 2 * HEAD_DIM), BF16),
            jax.ShapeDtypeStruct((B, G, S, HEAD_DIM), BF16),
            jax.ShapeDtypeStruct((B, G, S, HEAD_DIM), BF16),
            jax.ShapeDtypeStruct((B, G, S, HEAD_DIM), BF16),
        ],
        compiler_params=_params(("parallel", "parallel"), 40),
    )(proj_a, proj_c, cos2, sin2)


def _cmp_select_kernel(q_ref, kc_ref, vc_ref, ovl_ref, o_ref, bias_ref, *, hg, tq, n_sel, n_keep, scale):
    i = pl.program_id(2)
    nc = kc_ref.shape[2]
    kc = kc_ref[0, 0]
    vc = vc_ref[0, 0]
    t = i * tq + lax.broadcasted_iota(jnp.int32, (tq, nc), 0)
    n = lax.broadcasted_iota(jnp.int32, (tq, nc), 1)
    valid = (n * CMP_STRIDE + (CMP_BLOCK - 1) <= t) & (n < nc - 1)
    ovl = ovl_ref[...]
    imp = jnp.zeros((n_sel, tq), F32)
    for h in range(hg):
        s = lax.dot_general(q_ref[0, h], kc, NT_DIMS, preferred_element_type=F32) * scale
        s = jnp.where(valid, s, NEG_INF)
        m = jnp.max(s, axis=-1, keepdims=True)
        e = jnp.where(valid, jnp.exp(s - m), 0.0)
        den = jnp.sum(e, axis=-1, keepdims=True)
        pc = (e / jnp.where(den > 0.0, den, 1.0)).astype(BF16)
        o_ref[0, h] = jnp.dot(pc, vc, preferred_element_type=F32)
        imp = imp + lax.dot_general(ovl, pc, NT_DIMS, preferred_element_type=F32)
    j = lax.broadcasted_iota(jnp.int32, (n_sel, tq), 0)
    tb = (i * tq + lax.broadcasted_iota(jnp.int32, (n_sel, tq), 1)) // SEL_BLOCK
    forced = (j == 0) | (j == tb) | (j == tb - 1)
    score = jnp.where(forced, FORCE_SCORE, jnp.where(j <= tb, imp, -1.0))
    rank = jnp.zeros((n_sel, tq), jnp.int32)
    for jp in range(n_sel):
        row = score[jp:jp + 1, :]
        ahead = (row > score) | ((row == score) & (jp < j))
        rank = rank + ahead.astype(jnp.int32)
    bias_t = jnp.where(rank < n_keep, 0.0, NEG_INF)
    if n_sel < LANES:
        bias_t = jnp.concatenate([bias_t, jnp.zeros((LANES - n_sel, tq), F32)], axis=0)
    bias_ref[0, 0] = bias_t.T.astype(BF16)


def _overlap_t(n_cmp_pad, n_sel):
    n_cmp = n_cmp_pad - 1
    cs = np.arange(n_cmp) * CMP_STRIDE
    ce = cs + CMP_BLOCK
    ss = np.arange(n_sel) * SEL_BLOCK
    se = ss + SEL_BLOCK
    ov = np.clip(np.minimum(ce[:, None], se[None, :]) - np.maximum(cs[:, None], ss[None, :]), 0, None)
    out = np.zeros((n_sel, n_cmp_pad), np.float32)
    out[:, :n_cmp] = (ov / CMP_BLOCK).T
    return jnp.asarray(out, dtype=BF16)


def _cmp_select(q_rot, kc, vc, B, S, G, hg):
    nc = kc.shape[2]
    n_sel = S // SEL_BLOCK
    assert n_sel <= LANES and n_sel % 8 == 0
    tq = _pick_tile(S, (256, 128))
    nq = S // tq
    nh = G * hg
    kern = functools.partial(_cmp_select_kernel, hg=hg, tq=tq, n_sel=n_sel,
                             n_keep=min(N_SEL, n_sel), scale=HEAD_DIM ** -0.5)
    return pl.pallas_call(
        kern,
        name="cmp_attn_select",
        grid=(B, G, nq),
        in_specs=[
            pl.BlockSpec((1, hg, tq, HEAD_DIM), lambda b, g, i: (b, g, i, 0)),
            pl.BlockSpec((1, 1, nc, HEAD_DIM), lambda b, g, i: (b, g, 0, 0)),
            pl.BlockSpec((1, 1, nc, HEAD_DIM), lambda b, g, i: (b, g, 0, 0)),
            pl.BlockSpec((n_sel, nc), lambda b, g, i: (0, 0)),
        ],
        out_specs=[
            pl.BlockSpec((1, hg, tq, HEAD_DIM), lambda b, g, i: (b, g, i, 0)),
            pl.BlockSpec((1, 1, tq, LANES), lambda b, g, i: (b, g, i, 0)),
        ],
        out_shape=[
            jax.ShapeDtypeStruct((B, nh, S, HEAD_DIM), F32),
            jax.ShapeDtypeStruct((B, G, S, LANES), BF16),
        ],
        compiler_params=_params(("parallel", "parallel", "parallel"), 40),
    )(q_rot, kc, vc, _overlap_t(nc, n_sel))


def _lane_fold(x, op):
    r = x[:, :LANES]
    for a in range(1, x.shape[1] // LANES):
        r = op(r, x[:, a * LANES:(a + 1) * LANES])
    return r


def _grouped_loop(n, tile_fn, combine, commit, group):
    def run(start, count):
        total = tile_fn(start)
        for u in range(1, count):
            total = combine(total, tile_fn(start + u))
        commit(total)

    def body(g, carry):
        run(g * group, group)
        return carry

    lax.fori_loop(0, n // group, body, 0)
    rem = n % group
    step = group // 2
    while step >= 1:
        @pl.when((rem & step) != 0)
        def _(step=step):
            run(n - rem + (rem & ~(2 * step - 1)), step)

        step //= 2


def _slc_kernel(q_ref, bias_ref, k_ref, v_ref, o_ref, s_ref, m_ref, l_ref, acc_ref, *, hg, tq, scale):
    i = pl.program_id(2)
    rows = hg * tq
    c = scale * LOG2E
    q = q_ref[0].reshape(rows, HEAD_DIM)
    b = bias_ref[0, 0]
    qa = jnp.concatenate([q, jnp.concatenate([b] * hg, axis=0)], axis=1)

    def k_tile(j):
        return k_ref[0, 0, pl.ds(pl.multiple_of(j * tq, tq), tq), :]

    def v_tile(j):
        return v_ref[0, 0, pl.ds(pl.multiple_of(j * tq, tq), tq), :]

    m_ref[...] = jnp.full_like(m_ref, NEG_INF)

    def scores(j):
        s = lax.dot_general(qa, k_tile(j), NT_DIMS, preferred_element_type=F32)
        s_ref[j] = s
        return _lane_fold(s, jnp.maximum)

    def commit_max(m):
        m_ref[...] = jnp.maximum(m_ref[...], m)

    _grouped_loop(i, scores, jnp.maximum, commit_max, KEY_TILE_GROUP)
    sd = lax.dot_general(qa, k_tile(i), NT_DIMS, preferred_element_type=F32)
    tpos = lax.broadcasted_iota(jnp.int32, (rows, tq), 0) & (tq - 1)
    kpos = lax.broadcasted_iota(jnp.int32, (rows, tq), 1)
    sd = jnp.where(kpos <= tpos, sd, NEG_INF)
    mrow = jnp.max(jnp.maximum(m_ref[...], _lane_fold(sd, jnp.maximum)), axis=-1, keepdims=True) * c
    m_ref[...] = jnp.broadcast_to(mrow, (rows, LANES))
    l_ref[...] = jnp.zeros_like(l_ref)
    acc_ref[...] = jnp.zeros_like(acc_ref)

    def probs(s):
        mfull = jnp.concatenate([m_ref[...]] * (tq // LANES), axis=1)
        return jnp.exp2(s * c - mfull)

    def weighted(j):
        p = probs(s_ref[j])
        return _lane_fold(p, jnp.add), jnp.dot(p.astype(BF16), v_tile(j), preferred_element_type=F32)

    def commit_sums(t):
        l_ref[...] += t[0]
        acc_ref[...] += t[1]

    _grouped_loop(i, weighted, lambda a, b: (a[0] + b[0], a[1] + b[1]), commit_sums, KEY_TILE_GROUP)
    p = probs(sd)
    den = jnp.sum(l_ref[...] + _lane_fold(p, jnp.add), axis=-1, keepdims=True)
    acc = acc_ref[...] + jnp.dot(p.astype(BF16), v_tile(i), preferred_element_type=F32)
    o_ref[0] = (acc / den).reshape(hg, tq, HEAD_DIM)


def _selected_attention(q_rot, k_aug, v, bias, B, S, G, hg, tq):
    assert tq & (tq - 1) == 0 and tq % LANES == 0
    nq = S // tq
    rows = hg * tq
    kern = functools.partial(_slc_kernel, hg=hg, tq=tq, scale=HEAD_DIM ** -0.5)
    return pl.pallas_call(
        kern,
        name="selected_attn",
        grid=(B, G, nq),
        in_specs=[
            pl.BlockSpec((1, hg, tq, HEAD_DIM), lambda b, g, i: (b, g, i, 0)),
            pl.BlockSpec((1, 1, tq, LANES), lambda b, g, i: (b, g, i, 0)),
            pl.BlockSpec((1, 1, S, 2 * HEAD_DIM), lambda b, g, i: (b, g, 0, 0)),
            pl.BlockSpec((1, 1, S, HEAD_DIM), lambda b, g, i: (b, g, 0, 0)),
        ],
        out_specs=pl.BlockSpec((1, hg, tq, HEAD_DIM), lambda b, g, i: (b, g, i, 0)),
        out_shape=jax.ShapeDtypeStruct((B, G * hg, S, HEAD_DIM), F32),
        scratch_shapes=[pltpu.VMEM((nq, rows, tq), F32), pltpu.VMEM((rows, LANES), F32),
                        pltpu.VMEM((rows, LANES), F32), pltpu.VMEM((rows, HEAD_DIM), F32)],
        compiler_params=_params(("parallel", "parallel", "arbitrary"), 48),
    )(q_rot, bias, k_aug, v)


def _win_kernel(q_ref, g_ref, oc_ref, os_ref, *refs, hg, tq, nprev, window, scale):
    nt = nprev + 1
    k_refs, v_refs, y_ref = refs[:nt], refs[nt:2 * nt], refs[2 * nt]
    i = pl.program_id(2)
    gate = jax.nn.sigmoid(g_ref[0])
    c = scale * LOG2E
    nk = nt * tq
    kcat = jnp.concatenate([r[0, 0] for r in k_refs], axis=0)
    vcat = jnp.concatenate([r[0, 0] for r in v_refs], axis=0)
    tpos = i * tq + lax.broadcasted_iota(jnp.int32, (tq, nk), 0)
    kpos = (i - nprev) * tq + lax.broadcasted_iota(jnp.int32, (tq, nk), 1)
    mask = (kpos <= tpos) & (kpos > tpos - window) & (kpos >= 0)
    for h in range(hg):
        s = lax.dot_general(q_ref[0, h], kcat, NT_DIMS, preferred_element_type=F32)
        s = jnp.where(mask, s, NEG_INF)
        m = jnp.max(s, axis=-1, keepdims=True)
        p = jnp.exp2((s - m) * c)
        den = jnp.sum(p, axis=-1, keepdims=True)
        o_win = jnp.dot(p.astype(BF16), vcat, preferred_element_type=F32) / den
        gc = N_GATES * h
        y = (gate[:, gc:gc + 1] * oc_ref[0, h] + gate[:, gc + 1:gc + 2] * os_ref[0, h]
             + gate[:, gc + 2:gc + 3] * o_win)
        y_ref[:, h * HEAD_DIM:(h + 1) * HEAD_DIM] = y.astype(y_ref.dtype)


def _window_attention_gated(q_rot, k, v, gates_g, o_cmp, o_slc, B, S, G, hg, tq, window):
    assert window % tq == 0
    nprev = window // tq
    nq = S // tq
    kern = functools.partial(_win_kernel, hg=hg, tq=tq, nprev=nprev, window=window, scale=HEAD_DIM ** -0.5)

    def kv_spec(a):
        return pl.BlockSpec((1, 1, tq, HEAD_DIM), lambda b, g, i: (b, g, jnp.maximum(i - nprev + a, 0), 0))

    kv_specs = [kv_spec(a) for a in range(nprev + 1)]
    heads = pl.BlockSpec((1, hg, tq, HEAD_DIM), lambda b, g, i: (b, g, i, 0))
    return pl.pallas_call(
        kern,
        name="window_attn_gated_sum",
        grid=(B, G, nq),
        in_specs=[heads, pl.BlockSpec((1, tq, LANES), lambda b, g, i: (g, b * nq + i, 0)), heads, heads]
        + kv_specs + kv_specs,
        out_specs=pl.BlockSpec((tq, hg * HEAD_DIM), lambda b, g, i: (b * nq + i, g)),
        out_shape=jax.ShapeDtypeStruct((B * S, G * hg * HEAD_DIM), BF16),
        compiler_params=_params(("parallel", "parallel", "arbitrary"), 40),
    )(q_rot, gates_g, o_cmp, o_slc, *([k] * (nprev + 1)), *([v] * (nprev + 1)))


def _layer_norm(z, g, b):
    mu = jnp.mean(z, axis=-1, keepdims=True)
    zc = z - mu
    var = jnp.mean(zc * zc, axis=-1, keepdims=True)
    return zc * lax.rsqrt(var + LN_EPS) * g + b


def _mix_kernel(yp_ref, yn_ref, w_ref, x_ref, z_ref, *, nkp, alpha):
    k = pl.program_id(2)

    @pl.when(k == 0)
    def _():
        z_ref[...] = alpha * x_ref[...]

    @pl.when(k < nkp)
    def _():
        z_ref[...] += jnp.dot(yp_ref[...], w_ref[...], preferred_element_type=F32)

    @pl.when(k >= nkp)
    def _():
        z_ref[...] += jnp.dot(yn_ref[...], w_ref[...], preferred_element_type=F32)


def _ln_router_kernel(z_ref, g_ref, b_ref, wr_ref, br_ref, h_ref, hb_ref, lg_ref):
    h = _layer_norm(z_ref[...], g_ref[...], b_ref[...])
    hb = h.astype(BF16)
    h_ref[...] = h
    hb_ref[...] = hb
    lg_ref[...] = jnp.dot(hb, wr_ref[...], preferred_element_type=F32) + br_ref[...]


def _mix_ln_router(y_pool, y_nsa, w_out_b, x2, ln_g, ln_b, w_router_p, b_router_p, alpha):
    T, D = x2.shape
    kp = y_pool.shape[1]
    tm = _pick_tile(T, (1024, 512, 256, 128))
    tn = _pick_tile(D, (1024, 512, 256, 128))
    tk = _pick_tile(math.gcd(kp, y_nsa.shape[1]), (2048, 1024, 512, 256, 128))
    nkp = kp // tk
    nk = nkp + y_nsa.shape[1] // tk
    z = pl.pallas_call(
        functools.partial(_mix_kernel, nkp=nkp, alpha=alpha),
        name="out_proj_residual",
        grid=(T // tm, D // tn, nk),
        in_specs=[
            pl.BlockSpec((tm, tk), lambda i, j, k: (i, jnp.minimum(k, nkp - 1))),
            pl.BlockSpec((tm, tk), lambda i, j, k: (i, jnp.maximum(k - nkp, 0))),
            pl.BlockSpec((tk, tn), lambda i, j, k: (k, j)),
            pl.BlockSpec((tm, tn), lambda i, j, k: (i, j)),
        ],
        out_specs=pl.BlockSpec((tm, tn), lambda i, j, k: (i, j)),
        out_shape=jax.ShapeDtypeStruct((T, D), F32),
        compiler_params=_params(("parallel", "parallel", "arbitrary"), 48),
    )(y_pool, y_nsa, w_out_b, x2)

    ne = w_router_p.shape[1]
    tr = _pick_tile(T, (256, 128))
    row = lambda i: (i, 0)
    const = lambda i: (0, 0)
    return pl.pallas_call(
        _ln_router_kernel,
        name="ln1_router",
        grid=(T // tr,),
        in_specs=[
            pl.BlockSpec((tr, D), row),
            pl.BlockSpec((1, D), const),
            pl.BlockSpec((1, D), const),
            pl.BlockSpec((D, ne), const),
            pl.BlockSpec((1, ne), const),
        ],
        out_specs=[pl.BlockSpec((tr, D), row), pl.BlockSpec((tr, D), row), pl.BlockSpec((tr, ne), row)],
        out_shape=[jax.ShapeDtypeStruct((T, D), F32), jax.ShapeDtypeStruct((T, D), BF16),
                   jax.ShapeDtypeStruct((T, ne), F32)],
        compiler_params=_params(("parallel",), 48),
    )(z, ln_g.reshape(1, D), ln_b.reshape(1, D), w_router_p, b_router_p)


def _row_copy(src_hbm, row, dst_ref, slot, sem):
    return pltpu.make_async_copy(src_hbm.at[pl.ds(row, 1)], dst_ref.at[pl.ds(slot, 1)], sem)


def _rows_copy(src_hbm, dst_ref, sem, n):
    return pltpu.make_async_copy(src_hbm.at[pl.ds(0, n)], dst_ref, sem)


PAIR = 2 * LANES


def _pair_split_kernel(w_ref, perm_ref, o_ref):
    perm = perm_ref[...]
    for c in range(w_ref.shape[2] // PAIR):
        sl = slice(c * PAIR, (c + 1) * PAIR)
        o_ref[0, :, sl] = jnp.dot(w_ref[0, :, sl].astype(BF16), perm, preferred_element_type=F32).astype(BF16)


def _pair_split_cast(w_up):
    E, D, F2 = w_up.shape
    assert F2 % PAIR == 0
    td = _pick_tile(D, (512, 256, 128))
    perm = np.zeros((PAIR, PAIR), np.float32)
    perm[2 * np.arange(LANES), np.arange(LANES)] = 1.0
    perm[2 * np.arange(LANES) + 1, LANES + np.arange(LANES)] = 1.0
    return pl.pallas_call(
        _pair_split_kernel,
        name="w_up_pair_split",
        grid=(E, D // td),
        in_specs=[pl.BlockSpec((1, td, F2), lambda e, i: (e, i, 0)),
                  pl.BlockSpec((PAIR, PAIR), lambda e, i: (0, 0))],
        out_specs=pl.BlockSpec((1, td, F2), lambda e, i: (e, i, 0)),
        out_shape=jax.ShapeDtypeStruct((E, D, F2), BF16),
        compiler_params=_params(("parallel", "parallel"), 32),
    )(w_up, jnp.asarray(perm, dtype=BF16))


def _moe_up_kernel(tok_ref, be_ref, bv_ref, h_hbm, w_ref, b_ref, act_ref, xbuf_ref, sem, *, rb, nb):
    i = pl.program_id(0)

    def gather(blk, slot):
        def issue(r, carry):
            _row_copy(h_hbm, tok_ref[blk * rb + r], xbuf_ref.at[slot], r, sem.at[slot]).start()
            return carry

        lax.fori_loop(0, rb, issue, 0, unroll=ISSUE_UNROLL)

    def drain(slot):
        _rows_copy(h_hbm, xbuf_ref.at[slot], sem.at[slot], rb).wait()

    @pl.when((i == 0) & (bv_ref[0] != 0))
    def _():
        for a in range(GATHER_AHEAD):
            gather(a, a)

    slot = i % NBUF
    valid = bv_ref[i] != 0
    started = bv_ref[jnp.maximum(i - GATHER_AHEAD, 0)] != 0

    @pl.when(jnp.logical_not(valid))
    def _():
        act_ref[...] = jnp.zeros_like(act_ref)

    @pl.when(jnp.logical_not(valid) & started)
    def _():
        drain(slot)

    @pl.when(valid)
    def _():
        drain(slot)
        xb = xbuf_ref[slot].astype(BF16)
        nxt = jnp.minimum(i + GATHER_AHEAD, nb - 1)
        nslot = (i + GATHER_AHEAD) % NBUF
        for r in range(rb):
            _row_copy(h_hbm, tok_ref[nxt * rb + r], xbuf_ref.at[nslot], r, sem.at[nslot]).start()
        hcat = jnp.dot(xb, w_ref[0], preferred_element_type=F32) + b_ref[0]
        for c in range(hcat.shape[1] // PAIR):
            gate = jnp.minimum(hcat[:, c * PAIR:c * PAIR + LANES], SWIGLU_LIMIT)
            up = jnp.clip(hcat[:, c * PAIR + LANES:(c + 1) * PAIR], -SWIGLU_LIMIT, SWIGLU_LIMIT)
            act = gate * jax.nn.sigmoid(SWIGLU_ALPHA * gate) * (up + 1.0)
            act_ref[:, c * LANES:(c + 1) * LANES] = act.astype(act_ref.dtype)

    for a in range(GATHER_AHEAD):
        j = nb - 1 - a

        @pl.when((i == nb - 1) & (bv_ref[j] != 0))
        def _():
            drain((j + GATHER_AHEAD) % NBUF)


def _moe_up(row_tok, blk_e, blk_valid, h, w_pairs, b_pairs):
    n_rows = row_tok.shape[0]
    rb = EXPERT_ROW_BLOCK
    nb = n_rows // rb
    assert nb > GATHER_AHEAD
    E, D, F2 = w_pairs.shape
    wmap = lambda i, tok, be, bv: (be[i], 0, 0)
    return pl.pallas_call(
        functools.partial(_moe_up_kernel, rb=rb, nb=nb),
        name="moe_gather_up_swiglu",
        grid_spec=pltpu.PrefetchScalarGridSpec(
            num_scalar_prefetch=3,
            grid=(n_rows // rb,),
            in_specs=[
                pl.BlockSpec(memory_space=pl.ANY),
                pl.BlockSpec((1, D, F2), wmap),
                pl.BlockSpec((1, 1, F2), wmap),
            ],
            out_specs=pl.BlockSpec((rb, F2 // 2), lambda i, tok, be, bv: (i, 0)),
            scratch_shapes=[pltpu.VMEM((NBUF, rb, D), F32), pltpu.SemaphoreType.DMA((NBUF,))],
        ),
        out_shape=jax.ShapeDtypeStruct((n_rows, F2 // 2), BF16),
        compiler_params=_params(("arbitrary",), 56),
    )(row_tok, blk_e, blk_valid, h, w_pairs, b_pairs)


def _moe_down_kernel(be_ref, bv_ref, act_ref, wd_ref, bd_ref, rw_ref, y_ref, wb_ref):
    i = pl.program_id(0)

    @pl.when(bv_ref[i] == 0)
    def _():
        y_ref[...] = jnp.zeros_like(y_ref)

    @pl.when((bv_ref[i] != 0) & ((i == 0) | (be_ref[i] != be_ref[jnp.maximum(i - 1, 0)])))
    def _():
        wb_ref[...] = wd_ref[0].astype(BF16)

    @pl.when(bv_ref[i] != 0)
    def _():
        y = (jnp.dot(act_ref[...], wb_ref[...], preferred_element_type=F32) + bd_ref[0]) * rw_ref[...]
        bits = pltpu.bitcast(y.astype(BF16).astype(F32), jnp.uint32)
        half = bits.shape[1] // 2
        y_ref[...] = (bits[:, :half] >> 16) | (bits[:, half:] & jnp.uint32(0xFFFF0000))


def _moe_down(blk_e, blk_valid, act, wd, bd, row_w):
    n_rows, F = act.shape
    rb = EXPERT_ROW_BLOCK
    E, _, D = wd.shape
    wmap = lambda i, be, bv: (be[i], 0, 0)
    return pl.pallas_call(
        _moe_down_kernel,
        name="moe_down",
        grid_spec=pltpu.PrefetchScalarGridSpec(
            num_scalar_prefetch=2,
            grid=(n_rows // rb,),
            in_specs=[
                pl.BlockSpec((rb, F), lambda i, be, bv: (i, 0)),
                pl.BlockSpec((1, F, D), wmap),
                pl.BlockSpec((1, 1, D), wmap),
                pl.BlockSpec((rb, 1), lambda i, be, bv: (i, 0)),
            ],
            out_specs=pl.BlockSpec((rb, D // 2), lambda i, be, bv: (i, 0)),
            scratch_shapes=[pltpu.VMEM((F, D), BF16)],
        ),
        out_shape=jax.ShapeDtypeStruct((n_rows, D // 2), jnp.uint32),
        compiler_params=_params(("arbitrary",), 56),
    )(blk_e, blk_valid, act, wd, bd, row_w.reshape(n_rows, 1))


def _ple_kernel(hb_ref, w_ref, bg_ref, p_ref, wp_ref, h_ref, z_ref, *, alpha):
    k = pl.program_id(2)

    @pl.when(k == 0)
    def _():
        z_ref[...] = jnp.zeros_like(z_ref)

    z_ref[...] += jnp.dot(hb_ref[...], w_ref[...], preferred_element_type=F32)

    @pl.when(k == pl.num_programs(2) - 1)
    def _():
        gate = jax.nn.sigmoid(z_ref[...] + bg_ref[...])
        emb = jnp.dot(p_ref[...].astype(BF16), wp_ref[...], preferred_element_type=F32)
        z_ref[...] = alpha * h_ref[...] + gate * emb


def _ple_residual(hb, w_gate_b, b_gate, p2, w_ple_b, h, alpha):
    T, D = h.shape
    pd = p2.shape[1]
    tm = _pick_tile(T, (1024, 512, 256, 128))
    tn = _pick_tile(D, (1024, 512, 256, 128))
    tk = _pick_tile(D, (2048, 1024, 512, 256, 128))
    return pl.pallas_call(
        functools.partial(_ple_kernel, alpha=alpha),
        name="ple_gate_residual",
        grid=(T // tm, D // tn, D // tk),
        in_specs=[
            pl.BlockSpec((tm, tk), lambda i, j, k: (i, k)),
            pl.BlockSpec((tk, tn), lambda i, j, k: (k, j)),
            pl.BlockSpec((1, tn), lambda i, j, k: (0, j)),
            pl.BlockSpec((tm, pd), lambda i, j, k: (i, 0)),
            pl.BlockSpec((pd, tn), lambda i, j, k: (0, j)),
            pl.BlockSpec((tm, tn), lambda i, j, k: (i, j)),
        ],
        out_specs=pl.BlockSpec((tm, tn), lambda i, j, k: (i, j)),
        out_shape=jax.ShapeDtypeStruct((T, D), F32),
        compiler_params=_params(("parallel", "parallel", "arbitrary"), 48),
    )(hb, w_gate_b, b_gate.reshape(1, D), p2, w_ple_b, h)


def _combine_kernel(pos_ref, y_hbm, z_ref, g_ref, b_ref, o_ref, buf_ref, sem, *, tc, topk, nb):
    i = pl.program_id(0)

    def gather(blk, slot):
        def issue(t, carry):
            for k in range(topk):
                _row_copy(y_hbm, pos_ref[(blk * tc + t) * topk + k], buf_ref.at[slot, k], t, sem.at[slot]).start()
            return carry

        lax.fori_loop(0, tc, issue, 0, unroll=ISSUE_UNROLL // topk)

    def drain(slot):
        for k in range(topk):
            _rows_copy(y_hbm, buf_ref.at[slot, k], sem.at[slot], tc).wait()

    @pl.when(i == 0)
    def _():
        for a in range(GATHER_AHEAD):
            gather(a, a)

    slot = i % NBUF
    drain(slot)
    nxt = jnp.minimum(i + GATHER_AHEAD, nb - 1)
    nslot = (i + GATHER_AHEAD) % NBUF
    for t in range(tc):
        for k in range(topk):
            _row_copy(y_hbm, pos_ref[(nxt * tc + t) * topk + k], buf_ref.at[nslot, k], t, sem.at[nslot]).start()
    lo = hi = None
    for k in range(topk):
        word = buf_ref[slot, k]
        lo_k = pltpu.bitcast(word << 16, F32)
        hi_k = pltpu.bitcast(word & jnp.uint32(0xFFFF0000), F32)
        lo = lo_k if lo is None else lo + lo_k
        hi = hi_k if hi is None else hi + hi_k
    ffn = jnp.concatenate([lo, hi], axis=1)
    o_ref[...] = _layer_norm(z_ref[...] + ffn, g_ref[...], b_ref[...])

    @pl.when(i == nb - 1)
    def _():
        for a in range(GATHER_AHEAD):
            drain((nb - 1 - a + GATHER_AHEAD) % NBUF)


def _combine_ln(pos, y, z, ln_g, ln_b):
    T, D = z.shape
    tc = _pick_tile(T, (128, 64, 32))
    row = lambda i, pos: (i, 0)
    const = lambda i, pos: (0, 0)
    nb = T // tc
    assert nb > GATHER_AHEAD
    return pl.pallas_call(
        functools.partial(_combine_kernel, tc=tc, topk=TOP_K, nb=nb),
        name="moe_combine_ln2",
        grid_spec=pltpu.PrefetchScalarGridSpec(
            num_scalar_prefetch=1,
            grid=(T // tc,),
            in_specs=[
                pl.BlockSpec(memory_space=pl.ANY),
                pl.BlockSpec((tc, D), row),
                pl.BlockSpec((1, D), const),
                pl.BlockSpec((1, D), const),
            ],
            out_specs=pl.BlockSpec((tc, D), row),
            scratch_shapes=[pltpu.VMEM((NBUF, TOP_K, tc, D // 2), jnp.uint32), pltpu.SemaphoreType.DMA((NBUF,))],
        ),
        out_shape=jax.ShapeDtypeStruct((T, D), F32),
        compiler_params=_params(("arbitrary",), 32),
    )(pos, y, z, ln_g.reshape(1, D), ln_b.reshape(1, D))


def _routing_tables(logits, n_experts):
    T = logits.shape[0]
    A = T * TOP_K
    rb = EXPERT_ROW_BLOCK
    i32 = jnp.int32
    top_logit, top_e = lax.top_k(logits[:, :n_experts], TOP_K)
    top_w = jax.nn.softmax(top_logit, axis=-1)
    flat_e = top_e.reshape(-1).astype(i32)
    order = jnp.argsort(flat_e).astype(i32)
    sorted_pos = jnp.argsort(order).astype(i32)
    onehot = (flat_e[:, None] == jnp.arange(n_experts, dtype=i32)[None, :]).astype(i32)
    counts = jnp.sum(onehot, axis=0)
    padded = (counts + rb - 1) // rb * rb
    pad_end = jnp.cumsum(padded)
    pad_start = pad_end - padded
    grp_start = jnp.cumsum(counts) - counts
    pos = (jnp.sum(onehot * (pad_start - grp_start)[None, :], axis=1) + sorted_pos).astype(i32)
    n_blocks = -(-A // rb) + n_experts
    blk_start = jnp.arange(n_blocks, dtype=i32) * rb
    blk_e = jnp.minimum(jnp.sum((pad_end[None, :] <= blk_start[:, None]).astype(i32), axis=1), n_experts - 1)
    blk_valid = (blk_start < pad_end[-1]).astype(i32)
    off = (blk_start - pad_start[blk_e])[:, None] + jnp.arange(rb, dtype=i32)[None, :]
    valid = off < counts[blk_e][:, None]
    src = order[jnp.clip(grp_start[blk_e][:, None] + off, 0, A - 1)]
    row_tok = jnp.where(valid, src // TOP_K, 0).astype(i32).reshape(-1)
    row_w = jnp.where(valid, top_w.reshape(-1)[src], 0.0).reshape(-1)
    return row_tok, row_w, blk_e.astype(i32), blk_valid, pos


def _layer(x2, p2, B, S, w_in, w_pool, pool_scale, cmp_k_pe, cmp_k_w1, cmp_k_w2, cmp_v_pe, cmp_v_w1, cmp_v_w2,
           w_out, ln1_g, ln1_b, w_router, b_router, w_up, b_up, w_down, b_down,
           w_ple_gate, b_ple_gate, w_ple, ln2_g, ln2_b, alpha):
    T, D = x2.shape
    pool_w = D // 2
    nsa_w = D - pool_w
    nh = nsa_w // HEAD_DIM
    in_w = w_in.shape[1]
    kvw = (in_w - D - N_GATES * nh) // N_KV_GROUPS_FIELDS
    G = kvw // HEAD_DIM
    hg = nh // G
    n_experts = w_router.shape[1]

    xb = x2.astype(BF16)
    w_in_b = w_in.astype(BF16)
    proj_a = _matmul_cols(xb, w_in_b, 0, D)
    proj_b = _matmul_cols(xb, w_in_b, D, 2 * kvw)
    proj_c = _matmul_cols(xb, w_in_b, D + 2 * kvw, 4 * kvw)
    w_gates = jnp.pad(w_in_b[:, D + 6 * kvw:], ((0, 0), (0, LANES - N_GATES * nh)))
    gates = _matmul_cols(xb, w_gates, 0, LANES)

    y_pool = _pool_mixer(proj_a, w_pool.astype(BF16), pool_scale, B, S, pool_w)

    cos2, sin2 = _rope_tables(jnp.arange(S))
    nc = S // CMP_STRIDE
    ccos, csin = _rope_tables(jnp.arange(nc) * CMP_STRIDE + CMP_BLOCK - 1)
    kc = _compress(proj_b, 0, cmp_k_pe, cmp_k_w1, cmp_k_w2, ccos, csin, B, S, G, rope=True)
    vc = _compress(proj_b, 1, cmp_v_pe, cmp_v_w1, cmp_v_w2, ccos, csin, B, S, G, rope=False)
    q_rot, k_slc, v_slc, k_win, v_win = _rope_prep(proj_a, proj_c, cos2, sin2, B, S, nsa_w, G)
    o_cmp, sel_bias = _cmp_select(q_rot, kc, vc, B, S, G, hg)
    tq = _pick_tile(S, (256, 128))
    o_slc = _selected_attention(q_rot, k_slc, v_slc, sel_bias, B, S, G, hg, tq)
    gates_g = gates[:, :N_GATES * nh].reshape(T, G, N_GATES * hg).transpose(1, 0, 2)
    gates_g = jnp.pad(gates_g, ((0, 0), (0, 0), (0, LANES - N_GATES * hg)))
    y_nsa = _window_attention_gated(q_rot, k_win, v_win, gates_g, o_cmp, o_slc, B, S, G, hg, tq, WINDOW)

    ne_pad = -(-n_experts // LANES) * LANES
    w_router_p = jnp.pad(w_router, ((0, 0), (0, ne_pad - n_experts))).astype(BF16)
    b_router_p = jnp.pad(b_router, (0, ne_pad - n_experts)).reshape(1, ne_pad)
    h, hb, logits = _mix_ln_router(y_pool, y_nsa, w_out.astype(BF16), x2, ln1_g, ln1_b,
                                   w_router_p, b_router_p, alpha)

    row_tok, row_w, blk_e, blk_valid, pos = _routing_tables(logits, n_experts)
    F = w_down.shape[1]
    w_pairs = _pair_split_cast(w_up)
    b_pairs = b_up.reshape(n_experts, F // LANES, LANES, 2).transpose(0, 1, 3, 2).reshape(n_experts, 1, 2 * F)
    act = _moe_up(row_tok, blk_e, blk_valid, h, w_pairs, b_pairs)
    y = _moe_down(blk_e, blk_valid, act, w_down, b_down.reshape(n_experts, 1, D), row_w)

    z = _ple_residual(hb, w_ple_gate.astype(BF16), b_ple_gate, p2, w_ple.astype(BF16), h, alpha)
    return _combine_ln(pos, y, z, ln2_g, ln2_b)


def kernel(x, p, w_in, w_pool, pool_scale, cmp_k_pe, cmp_k_w1, cmp_k_w2, cmp_v_pe, cmp_v_w1, cmp_v_w2, w_out, ln1_g, ln1_b, w_router, b_router, w_up, b_up, w_down, b_down, w_ple_gate, b_ple_gate, w_ple, ln2_g, ln2_b):
    B, S, D = x.shape
    depth = w_in.shape[0]
    alpha = (2 * depth) ** 0.25
    x2 = x.reshape(B * S, D)
    for i in range(depth):
        x2 = _layer(x2, p[i].reshape(B * S, -1), B, S, w_in[i], w_pool[i], pool_scale[i],
                    cmp_k_pe[i], cmp_k_w1[i], cmp_k_w2[i], cmp_v_pe[i], cmp_v_w1[i], cmp_v_w2[i],
                    w_out[i], ln1_g[i], ln1_b[i], w_router[i], b_router[i], w_up[i], b_up[i],
                    w_down[i], b_down[i], w_ple_gate[i], b_ple_gate[i], w_ple[i], ln2_g[i], ln2_b[i], alpha)
    return x2.reshape(B, S, D)
```

```python
import functools
import math

import jax
import jax.numpy as jnp
import numpy as np
from jax import lax
from jax.experimental import pallas as pl
from jax.experimental.pallas import tpu as pltpu

HEAD_DIM = 128
LANES = 128
N_KV_GROUPS_FIELDS = 6
N_GATES = 3
POOL_WINDOWS = (2, 4, 8, 16)
CMP_BLOCK = 32
CMP_STRIDE = 16
SEL_BLOCK = 64
N_SEL = 16
WINDOW = 512
ROPE_THETA = 10000.0
TOP_K = 4
SWIGLU_LIMIT = 7.0
SWIGLU_ALPHA = 1.702
EXPERT_ROW_BLOCK = 256
LN_EPS = 1e-5
NEG_INF = -1e30
FORCE_SCORE = 1e4
LOG2E = 1.4426950408889634
ISSUE_UNROLL = 8
KEY_TILE_GROUP = 4
GATHER_AHEAD = 2
NBUF = GATHER_AHEAD + 1
MIB = 1024 * 1024

BF16 = jnp.bfloat16
F32 = jnp.float32
NT_DIMS = (((1,), (1,)), ((), ()))


def _params(sem, vmem_mib=None):
    kw = dict(dimension_semantics=sem)
    if vmem_mib is not None:
        kw["vmem_limit_bytes"] = vmem_mib * MIB
    return pltpu.CompilerParams(**kw)


def _pick_tile(n, candidates):
    for c in candidates:
        if n % c == 0:
            return c
    raise ValueError(f"no tile for {n}")


def _mm_kernel(x_ref, w_ref, o_ref):
    @pl.when(pl.program_id(2) == 0)
    def _():
        o_ref[...] = jnp.zeros_like(o_ref)

    o_ref[...] += jnp.dot(x_ref[...], w_ref[...], preferred_element_type=F32)


def _matmul_cols(x, w, col0, ncols):
    M, K = x.shape
    tm = _pick_tile(M, (1024, 512, 256))
    tk = _pick_tile(K, (2048, 1024, 512, 256, 128))
    tn = _pick_tile(math.gcd(col0, ncols) if col0 else ncols, (1024, 512, 256, 128))
    off = col0 // tn
    return pl.pallas_call(
        _mm_kernel,
        name="in_proj",
        grid=(M // tm, ncols // tn, K // tk),
        in_specs=[pl.BlockSpec((tm, tk), lambda i, j, k: (i, k)),
                  pl.BlockSpec((tk, tn), lambda i, j, k: (k, j + off))],
        out_specs=pl.BlockSpec((tm, tn), lambda i, j, k: (i, j)),
        out_shape=jax.ShapeDtypeStruct((M, ncols), F32),
        compiler_params=_params(("parallel", "parallel", "arbitrary"), 40),
    )(x, w)


def _pool_kernel(u_ref, prev_ref, w_ref, sc_ref, o_ref, ext_ref, *, ts, halo):
    g = pl.program_id(0)
    i = pl.program_id(2)
    cur = u_ref[...]
    ext_ref[pl.ds(halo, ts), :] = cur

    @pl.when(i == 0)
    def _():
        ext_ref[pl.ds(0, halo), :] = jnp.zeros((halo, cur.shape[1]), F32)

    @pl.when(i > 0)
    def _():
        ext_ref[pl.ds(0, halo), :] = prev_ref[...]

    def back(d):
        return ext_ref[pl.ds(halo - d, ts), :]

    t = i * ts + lax.broadcasted_iota(jnp.int32, (ts, 1), 0)
    for gi, win in enumerate(POOL_WINDOWS):
        @pl.when(g == gi)
        def _(win=win):
            ssum = cur
            for d in range(1, win):
                ssum = ssum + back(d)
            cnt = jnp.minimum(t + 1, win).astype(F32)
            pooled = ssum / cnt - cur
            mixed = jnp.dot(pooled.astype(BF16), w_ref[0], preferred_element_type=F32)
            o_ref[...] = (mixed * sc_ref[...]).astype(o_ref.dtype)


def _pool_mixer(proj_a, w_pool_b, pool_scale, B, S, pool_w):
    T = B * S
    ng = len(POOL_WINDOWS)
    pg = pool_w // ng
    halo = POOL_WINDOWS[-1]
    ts = _pick_tile(S, (512, 256, 128))
    ns = S // ts
    hb = ts // halo
    kern = functools.partial(_pool_kernel, ts=ts, halo=halo)
    return pl.pallas_call(
        kern,
        name="pool_mixer",
        grid=(ng, B, ns),
        in_specs=[
            pl.BlockSpec((ts, pg), lambda g, b, i: (b * ns + i, g)),
            pl.BlockSpec((halo, pg), lambda g, b, i: (jnp.maximum((b * ns + i) * hb - 1, 0), g)),
            pl.BlockSpec((1, pg, pg), lambda g, b, i: (g, 0, 0)),
            pl.BlockSpec((1, pg), lambda g, b, i: (0, g)),
        ],
        out_specs=pl.BlockSpec((ts, pg), lambda g, b, i: (b * ns + i, g)),
        out_shape=jax.ShapeDtypeStruct((T, pool_w), BF16),
        scratch_shapes=[pltpu.VMEM((ts + halo, pg), F32)],
        compiler_params=_params(("parallel", "parallel", "arbitrary"), 32),
    )(proj_a, proj_a, w_pool_b, pool_scale.reshape(1, pool_w))


def _rope_tables(pos):
    inv = ROPE_THETA ** (-jnp.arange(0, HEAD_DIM, 2, dtype=F32) / HEAD_DIM)
    ang = pos.astype(F32)[:, None] * inv[None, :]
    c, s = jnp.cos(ang), jnp.sin(ang)
    return jnp.concatenate([c, c], axis=-1), jnp.concatenate([-s, s], axis=-1)


def _rope(x, cos2, sin2):
    return x * cos2 + pltpu.roll(x, HEAD_DIM // 2, 1) * sin2


def _compress_kernel(x_ref, pe_ref, w1_ref, w2_ref, cos_ref, sin_ref, o_ref, *, rope, nl):
    nc = x_ref.shape[0] // nl
    xs = [x_ref[pl.ds(l, nc, stride=nl), :] for l in range(nl)]
    lo = jnp.concatenate([(xs[l] + pe_ref[l:l + 1, :]).astype(BF16) for l in range(nl)], axis=1)
    hi = jnp.concatenate([(xs[l] + pe_ref[nl + l:nl + l + 1, :]).astype(BF16) for l in range(nl)], axis=1)
    half = nl * HEAD_DIM
    a = jnp.dot(lo, w1_ref[0:half, :], preferred_element_type=F32)
    b = jnp.dot(hi, w1_ref[half:2 * half, :], preferred_element_type=F32)
    hid = jax.nn.gelu(a + pltpu.roll(b, nc - 1, 0))
    out = jnp.dot(hid.astype(BF16), w2_ref[...], preferred_element_type=F32)
    if rope:
        out = _rope(out, cos_ref[...], sin_ref[...])
    o_ref[0, 0] = out.astype(o_ref.dtype)


def _compress(proj_b, field, pe, w1, w2, cos2, sin2, B, S, G, rope):
    nl = CMP_STRIDE
    nc = S // nl
    kern = functools.partial(_compress_kernel, rope=rope, nl=nl)
    const = lambda b, g: (0, 0)
    return pl.pallas_call(
        kern,
        name="compress_k" if rope else "compress_v",
        grid=(B, G),
        in_specs=[
            pl.BlockSpec((S, HEAD_DIM), lambda b, g: (b, field * G + g)),
            pl.BlockSpec((CMP_BLOCK, HEAD_DIM), const),
            pl.BlockSpec((CMP_BLOCK * HEAD_DIM, HEAD_DIM), const),
            pl.BlockSpec((HEAD_DIM, HEAD_DIM), const),
            pl.BlockSpec((nc, HEAD_DIM), const),
            pl.BlockSpec((nc, HEAD_DIM), const),
        ],
        out_specs=pl.BlockSpec((1, 1, nc, HEAD_DIM), lambda b, g: (b, g, 0, 0)),
        out_shape=jax.ShapeDtypeStruct((B, G, nc, HEAD_DIM), BF16),
        compiler_params=_params(("parallel", "parallel"), 32),
    )(proj_b, pe, w1.reshape(CMP_BLOCK * HEAD_DIM, HEAD_DIM).astype(BF16), w2.astype(BF16), cos2, sin2)


def _rope_prep_kernel(q_ref, kv_ref, cos_ref, sin_ref, qo_ref, ks_ref, vs_ref, kw_ref, vw_ref, *, nh, G, ts):
    i = pl.program_id(1)
    c = cos_ref[...]
    s = sin_ref[...]
    kvw = G * HEAD_DIM
    for h in range(nh):
        qo_ref[0, h] = _rope(q_ref[:, h * HEAD_DIM:(h + 1) * HEAD_DIM], c, s).astype(BF16)
    blk = (i * ts + lax.broadcasted_iota(jnp.int32, (ts, LANES), 0)) // SEL_BLOCK
    onehot = (blk == lax.broadcasted_iota(jnp.int32, (ts, LANES), 1)).astype(BF16)
    for g in range(G):
        lo = g * HEAD_DIM
        ks_ref[0, g, :, 0:HEAD_DIM] = _rope(kv_ref[:, lo:lo + HEAD_DIM], c, s).astype(BF16)
        ks_ref[0, g, :, HEAD_DIM:2 * HEAD_DIM] = onehot
        vs_ref[0, g] = kv_ref[:, kvw + lo:kvw + lo + HEAD_DIM].astype(BF16)
        kw_ref[0, g] = _rope(kv_ref[:, 2 * kvw + lo:2 * kvw + lo + HEAD_DIM], c, s).astype(BF16)
        vw_ref[0, g] = kv_ref[:, 3 * kvw + lo:3 * kvw + lo + HEAD_DIM].astype(BF16)


def _rope_prep(proj_a, proj_c, cos2, sin2, B, S, nsa_w, G):
    nh = nsa_w // HEAD_DIM
    ts = _pick_tile(S, (256, 128))
    ns = S // ts
    kern = functools.partial(_rope_prep_kernel, nh=nh, G=G, ts=ts)
    hm = lambda b, i: (b, 0, i, 0)
    return pl.pallas_call(
        kern,
        name="rope_prep",
        grid=(B, ns),
        in_specs=[
            pl.BlockSpec((ts, nsa_w), lambda b, i: (b * ns + i, 1)),
            pl.BlockSpec((ts, 4 * G * HEAD_DIM), lambda b, i: (b * ns + i, 0)),
            pl.BlockSpec((ts, HEAD_DIM), lambda b, i: (i, 0)),
            pl.BlockSpec((ts, HEAD_DIM), lambda b, i: (i, 0)),
        ],
        out_specs=[
            pl.BlockSpec((1, nh, ts, HEAD_DIM), hm),
            pl.BlockSpec((1, G, ts, 2 * HEAD_DIM), hm),
            pl.BlockSpec((1, G, ts, HEAD_DIM), hm),
            pl.BlockSpec((1, G, ts, HEAD_DIM), hm),
            pl.BlockSpec((1, G, ts, HEAD_DIM), hm),
        ],
        out_shape=[
            jax.ShapeDtypeStruct((B, nh, S, HEAD_DIM), BF16),
            jax.ShapeDtypeStruct((B, G, S, 2 * HEAD_DIM), BF16),
            jax.ShapeDtypeStruct((B, G, S, HEAD_DIM), BF16),
            jax.ShapeDtypeStruct((B, G, S, HEAD_DIM), BF16),
            jax.ShapeDtypeStruct((B, G, S, HEAD_DIM), BF16),
        ],
        compiler_params=_params(("parallel", "parallel"), 40),
    )(proj_a, proj_c, cos2, sin2)


def _cmp_select_kernel(q_ref, kc_ref, vc_ref, ovl_ref, o_ref, bias_ref, *, hg, tq, n_sel, n_keep, scale):
    i = pl.program_id(2)
    nc = kc_ref.shape[2]
    kc = kc_ref[0, 0]
    vc = vc_ref[0, 0]
    t = i * tq + lax.broadcasted_iota(jnp.int32, (tq, nc), 0)
    n = lax.broadcasted_iota(jnp.int32, (tq, nc), 1)
    valid = (n * CMP_STRIDE + (CMP_BLOCK - 1) <= t) & (n < nc - 1)
    ovl = ovl_ref[...]
    imp = jnp.zeros((n_sel, tq), F32)
    for h in range(hg):
        s = lax.dot_general(q_ref[0, h], kc, NT_DIMS, preferred_element_type=F32) * scale
        s = jnp.where(valid, s, NEG_INF)
        m = jnp.max(s, axis=-1, keepdims=True)
        e = jnp.where(valid, jnp.exp(s - m), 0.0)
        den = jnp.sum(e, axis=-1, keepdims=True)
        pc = (e / jnp.where(den > 0.0, den, 1.0)).astype(BF16)
        o_ref[0, h] = jnp.dot(pc, vc, preferred_element_type=F32)
        imp = imp + lax.dot_general(ovl, pc, NT_DIMS, preferred_element_type=F32)
    j = lax.broadcasted_iota(jnp.int32, (n_sel, tq), 0)
    tb = (i * tq + lax.broadcasted_iota(jnp.int32, (n_sel, tq), 1)) // SEL_BLOCK
    forced = (j == 0) | (j == tb) | (j == tb - 1)
    score = jnp.where(forced, FORCE_SCORE, jnp.where(j <= tb, imp, -1.0))
    rank = jnp.zeros((n_sel, tq), jnp.int32)
    for jp in range(n_sel):
        row = score[jp:jp + 1, :]
        ahead = (row > score) | ((row == score) & (jp < j))
        rank = rank + ahead.astype(jnp.int32)
    bias_t = jnp.where(rank < n_keep, 0.0, NEG_INF)
    if n_sel < LANES:
        bias_t = jnp.concatenate([bias_t, jnp.zeros((LANES - n_sel, tq), F32)], axis=0)
    bias_ref[0, 0] = bias_t.T.astype(BF16)


def _overlap_t(n_cmp_pad, n_sel):
    n_cmp = n_cmp_pad - 1
    cs = np.arange(n_cmp) * CMP_STRIDE
    ce = cs + CMP_BLOCK
    ss = np.arange(n_sel) * SEL_BLOCK
    se = ss + SEL_BLOCK
    ov = np.clip(np.minimum(ce[:, None], se[None, :]) - np.maximum(cs[:, None], ss[None, :]), 0, None)
    out = np.zeros((n_sel, n_cmp_pad), np.float32)
    out[:, :n_cmp] = (ov / CMP_BLOCK).T
    return jnp.asarray(out, dtype=BF16)


def _cmp_select(q_rot, kc, vc, B, S, G, hg):
    nc = kc.shape[2]
    n_sel = S // SEL_BLOCK
    assert n_sel <= LANES and n_sel % 8 == 0
    tq = _pick_tile(S, (512, 256, 128))
    nq = S // tq
    nh = G * hg
    kern = functools.partial(_cmp_select_kernel, hg=hg, tq=tq, n_sel=n_sel,
                             n_keep=min(N_SEL, n_sel), scale=HEAD_DIM ** -0.5)
    return pl.pallas_call(
        kern,
        name="cmp_attn_select",
        grid=(B, G, nq),
        in_specs=[
            pl.BlockSpec((1, hg, tq, HEAD_DIM), lambda b, g, i: (b, g, i, 0)),
            pl.BlockSpec((1, 1, nc, HEAD_DIM), lambda b, g, i: (b, g, 0, 0)),
            pl.BlockSpec((1, 1, nc, HEAD_DIM), lambda b, g, i: (b, g, 0, 0)),
            pl.BlockSpec((n_sel, nc), lambda b, g, i: (0, 0)),
        ],
        out_specs=[
            pl.BlockSpec((1, hg, tq, HEAD_DIM), lambda b, g, i: (b, g, i, 0)),
            pl.BlockSpec((1, 1, tq, LANES), lambda b, g, i: (b, g, i, 0)),
        ],
        out_shape=[
            jax.ShapeDtypeStruct((B, nh, S, HEAD_DIM), F32),
            jax.ShapeDtypeStruct((B, G, S, LANES), BF16),
        ],
        compiler_params=_params(("parallel", "parallel", "parallel"), 40),
    )(q_rot, kc, vc, _overlap_t(nc, n_sel))


def _lane_fold(x, op):
    r = x[:, :LANES]
    for a in range(1, x.shape[1] // LANES):
        r = op(r, x[:, a * LANES:(a + 1) * LANES])
    return r


def _grouped_loop(n, tile_fn, combine, commit, group):
    def run(start, count):
        total = tile_fn(start)
        for u in range(1, count):
            total = combine(total, tile_fn(start + u))
        commit(total)

    def body(g, carry):
        run(g * group, group)
        return carry

    lax.fori_loop(0, n // group, body, 0)
    rem = n % group
    step = group // 2
    while step >= 1:
        @pl.when((rem & step) != 0)
        def _(step=step):
            run(n - rem + (rem & ~(2 * step - 1)), step)

        step //= 2


def _slc_kernel(q_ref, bias_ref, k_ref, v_ref, o_ref, s_ref, m_ref, l_ref, acc_ref, *, hg, tq, scale):
    i = pl.program_id(2)
    rows = hg * tq
    c = scale * LOG2E
    q = q_ref[0].reshape(rows, HEAD_DIM)
    b = bias_ref[0, 0]
    qa = jnp.concatenate([q, jnp.concatenate([b] * hg, axis=0)], axis=1)

    def k_tile(j):
        return k_ref[0, 0, pl.ds(pl.multiple_of(j * tq, tq), tq), :]

    def v_tile(j):
        return v_ref[0, 0, pl.ds(pl.multiple_of(j * tq, tq), tq), :]

    m_ref[...] = jnp.full_like(m_ref, NEG_INF)

    def scores(j):
        s = lax.dot_general(qa, k_tile(j), NT_DIMS, preferred_element_type=F32) * c
        s_ref[j] = s
        return _lane_fold(s, jnp.maximum)

    def commit_max(m):
        m_ref[...] = jnp.maximum(m_ref[...], m)

    _grouped_loop(i, scores, jnp.maximum, commit_max, KEY_TILE_GROUP)
    sd = lax.dot_general(qa, k_tile(i), NT_DIMS, preferred_element_type=F32) * c
    tpos = lax.broadcasted_iota(jnp.int32, (rows, tq), 0) & (tq - 1)
    kpos = lax.broadcasted_iota(jnp.int32, (rows, tq), 1)
    sd = jnp.where(kpos <= tpos, sd, NEG_INF)
    mrow = jnp.max(jnp.maximum(m_ref[...], _lane_fold(sd, jnp.maximum)), axis=-1, keepdims=True)
    m_ref[...] = jnp.broadcast_to(mrow, (rows, LANES))
    l_ref[...] = jnp.zeros_like(l_ref)
    acc_ref[...] = jnp.zeros_like(acc_ref)

    def probs(s):
        mfull = jnp.concatenate([m_ref[...]] * (tq // LANES), axis=1)
        return jnp.exp2(s - mfull)

    def weighted(j):
        p = probs(s_ref[j])
        return _lane_fold(p, jnp.add), jnp.dot(p.astype(BF16), v_tile(j), preferred_element_type=F32)

    def commit_sums(t):
        l_ref[...] += t[0]
        acc_ref[...] += t[1]

    _grouped_loop(i, weighted, lambda a, b: (a[0] + b[0], a[1] + b[1]), commit_sums, KEY_TILE_GROUP)
    p = probs(sd)
    den = jnp.sum(l_ref[...] + _lane_fold(p, jnp.add), axis=-1, keepdims=True)
    acc = acc_ref[...] + jnp.dot(p.astype(BF16), v_tile(i), preferred_element_type=F32)
    o_ref[0] = (acc / den).reshape(hg, tq, HEAD_DIM)


def _selected_attention(q_rot, k_aug, v, bias, B, S, G, hg, tq):
    assert tq & (tq - 1) == 0 and tq % LANES == 0
    nq = S // tq
    rows = hg * tq
    kern = functools.partial(_slc_kernel, hg=hg, tq=tq, scale=HEAD_DIM ** -0.5)
    return pl.pallas_call(
        kern,
        name="selected_attn",
        grid=(B, G, nq),
        in_specs=[
            pl.BlockSpec((1, hg, tq, HEAD_DIM), lambda b, g, i: (b, g, i, 0)),
            pl.BlockSpec((1, 1, tq, LANES), lambda b, g, i: (b, g, i, 0)),
            pl.BlockSpec((1, 1, S, 2 * HEAD_DIM), lambda b, g, i: (b, g, 0, 0)),
            pl.BlockSpec((1, 1, S, HEAD_DIM), lambda b, g, i: (b, g, 0, 0)),
        ],
        out_specs=pl.BlockSpec((1, hg, tq, HEAD_DIM), lambda b, g, i: (b, g, i, 0)),
        out_shape=jax.ShapeDtypeStruct((B, G * hg, S, HEAD_DIM), F32),
        scratch_shapes=[pltpu.VMEM((nq, rows, tq), F32), pltpu.VMEM((rows, LANES), F32),
                        pltpu.VMEM((rows, LANES), F32), pltpu.VMEM((rows, HEAD_DIM), F32)],
        compiler_params=_params(("parallel", "parallel", "arbitrary"), 48),
    )(q_rot, bias, k_aug, v)


def _win_kernel(q_ref, g_ref, oc_ref, os_ref, *refs, hg, tq, nprev, window, scale):
    nt = nprev + 1
    k_refs, v_refs, y_ref = refs[:nt], refs[nt:2 * nt], refs[2 * nt]
    i = pl.program_id(2)
    gate = jax.nn.sigmoid(g_ref[0])
    c = scale * LOG2E
    nk = nt * tq
    kcat = jnp.concatenate([r[0, 0] for r in k_refs], axis=0)
    vcat = jnp.concatenate([r[0, 0] for r in v_refs], axis=0)
    tpos = i * tq + lax.broadcasted_iota(jnp.int32, (tq, nk), 0)
    kpos = (i - nprev) * tq + lax.broadcasted_iota(jnp.int32, (tq, nk), 1)
    mask = (kpos <= tpos) & (kpos > tpos - window) & (kpos >= 0)
    for h in range(hg):
        s = lax.dot_general(q_ref[0, h], kcat, NT_DIMS, preferred_element_type=F32)
        s = jnp.where(mask, s, NEG_INF)
        m = jnp.max(s, axis=-1, keepdims=True)
        p = jnp.exp2((s - m) * c)
        den = jnp.sum(p, axis=-1, keepdims=True)
        o_win = jnp.dot(p.astype(BF16), vcat, preferred_element_type=F32) / den
        gc = N_GATES * h
        y = (gate[:, gc:gc + 1] * oc_ref[0, h] + gate[:, gc + 1:gc + 2] * os_ref[0, h]
             + gate[:, gc + 2:gc + 3] * o_win)
        y_ref[:, h * HEAD_DIM:(h + 1) * HEAD_DIM] = y.astype(y_ref.dtype)


def _window_attention_gated(q_rot, k, v, gates_g, o_cmp, o_slc, B, S, G, hg, tq, window):
    assert window % tq == 0
    nprev = window // tq
    nq = S // tq
    kern = functools.partial(_win_kernel, hg=hg, tq=tq, nprev=nprev, window=window, scale=HEAD_DIM ** -0.5)

    def kv_spec(a):
        return pl.BlockSpec((1, 1, tq, HEAD_DIM), lambda b, g, i: (b, g, jnp.maximum(i - nprev + a, 0), 0))

    kv_specs = [kv_spec(a) for a in range(nprev + 1)]
    heads = pl.BlockSpec((1, hg, tq, HEAD_DIM), lambda b, g, i: (b, g, i, 0))
    return pl.pallas_call(
        kern,
        name="window_attn_gated_sum",
        grid=(B, G, nq),
        in_specs=[heads, pl.BlockSpec((1, tq, LANES), lambda b, g, i: (g, b * nq + i, 0)), heads, heads]
        + kv_specs + kv_specs,
        out_specs=pl.BlockSpec((tq, hg * HEAD_DIM), lambda b, g, i: (b * nq + i, g)),
        out_shape=jax.ShapeDtypeStruct((B * S, G * hg * HEAD_DIM), BF16),
        compiler_params=_params(("parallel", "parallel", "arbitrary"), 40),
    )(q_rot, gates_g, o_cmp, o_slc, *([k] * (nprev + 1)), *([v] * (nprev + 1)))


def _layer_norm(z, g, b):
    mu = jnp.mean(z, axis=-1, keepdims=True)
    zc = z - mu
    var = jnp.mean(zc * zc, axis=-1, keepdims=True)
    return zc * lax.rsqrt(var + LN_EPS) * g + b


def _mix_kernel(yp_ref, yn_ref, w_ref, x_ref, z_ref, *, nkp, alpha):
    k = pl.program_id(2)

    @pl.when(k == 0)
    def _():
        z_ref[...] = alpha * x_ref[...]

    @pl.when(k < nkp)
    def _():
        z_ref[...] += jnp.dot(yp_ref[...], w_ref[...], preferred_element_type=F32)

    @pl.when(k >= nkp)
    def _():
        z_ref[...] += jnp.dot(yn_ref[...], w_ref[...], preferred_element_type=F32)


def _ln_router_kernel(z_ref, g_ref, b_ref, wr_ref, br_ref, h_ref, hb_ref, lg_ref):
    h = _layer_norm(z_ref[...], g_ref[...], b_ref[...])
    hb = h.astype(BF16)
    h_ref[...] = h
    hb_ref[...] = hb
    lg_ref[...] = jnp.dot(hb, wr_ref[...], preferred_element_type=F32) + br_ref[...]


def _mix_ln_router(y_pool, y_nsa, w_out_b, x2, ln_g, ln_b, w_router_p, b_router_p, alpha):
    T, D = x2.shape
    kp = y_pool.shape[1]
    tm = _pick_tile(T, (1024, 512, 256, 128))
    tn = _pick_tile(D, (1024, 512, 256, 128))
    tk = _pick_tile(math.gcd(kp, y_nsa.shape[1]), (2048, 1024, 512, 256, 128))
    nkp = kp // tk
    nk = nkp + y_nsa.shape[1] // tk
    z = pl.pallas_call(
        functools.partial(_mix_kernel, nkp=nkp, alpha=alpha),
        name="out_proj_residual",
        grid=(T // tm, D // tn, nk),
        in_specs=[
            pl.BlockSpec((tm, tk), lambda i, j, k: (i, jnp.minimum(k, nkp - 1))),
            pl.BlockSpec((tm, tk), lambda i, j, k: (i, jnp.maximum(k - nkp, 0))),
            pl.BlockSpec((tk, tn), lambda i, j, k: (k, j)),
            pl.BlockSpec((tm, tn), lambda i, j, k: (i, j)),
        ],
        out_specs=pl.BlockSpec((tm, tn), lambda i, j, k: (i, j)),
        out_shape=jax.ShapeDtypeStruct((T, D), F32),
        compiler_params=_params(("parallel", "parallel", "arbitrary"), 48),
    )(y_pool, y_nsa, w_out_b, x2)

    ne = w_router_p.shape[1]
    tr = _pick_tile(T, (256, 128))
    row = lambda i: (i, 0)
    const = lambda i: (0, 0)
    return pl.pallas_call(
        _ln_router_kernel,
        name="ln1_router",
        grid=(T // tr,),
        in_specs=[
            pl.BlockSpec((tr, D), row),
            pl.BlockSpec((1, D), const),
            pl.BlockSpec((1, D), const),
            pl.BlockSpec((D, ne), const),
            pl.BlockSpec((1, ne), const),
        ],
        out_specs=[pl.BlockSpec((tr, D), row), pl.BlockSpec((tr, D), row), pl.BlockSpec((tr, ne), row)],
        out_shape=[jax.ShapeDtypeStruct((T, D), F32), jax.ShapeDtypeStruct((T, D), BF16),
                   jax.ShapeDtypeStruct((T, ne), F32)],
        compiler_params=_params(("parallel",), 48),
    )(z, ln_g.reshape(1, D), ln_b.reshape(1, D), w_router_p, b_router_p)


def _row_copy(src_hbm, row, dst_ref, slot, sem):
    return pltpu.make_async_copy(src_hbm.at[pl.ds(row, 1)], dst_ref.at[pl.ds(slot, 1)], sem)


def _rows_copy(src_hbm, dst_ref, sem, n):
    return pltpu.make_async_copy(src_hbm.at[pl.ds(0, n)], dst_ref, sem)


PAIR = 2 * LANES


def _pair_split_kernel(w_ref, perm_ref, o_ref):
    perm = perm_ref[...]
    for c in range(w_ref.shape[2] // PAIR):
        sl = slice(c * PAIR, (c + 1) * PAIR)
        o_ref[0, :, sl] = jnp.dot(w_ref[0, :, sl].astype(BF16), perm, preferred_element_type=F32).astype(BF16)


def _pair_split_cast(w_up):
    E, D, F2 = w_up.shape
    assert F2 % PAIR == 0
    td = _pick_tile(D, (512, 256, 128))
    perm = np.zeros((PAIR, PAIR), np.float32)
    perm[2 * np.arange(LANES), np.arange(LANES)] = 1.0
    perm[2 * np.arange(LANES) + 1, LANES + np.arange(LANES)] = 1.0
    return pl.pallas_call(
        _pair_split_kernel,
        name="w_up_pair_split",
        grid=(E, D // td),
        in_specs=[pl.BlockSpec((1, td, F2), lambda e, i: (e, i, 0)),
                  pl.BlockSpec((PAIR, PAIR), lambda e, i: (0, 0))],
        out_specs=pl.BlockSpec((1, td, F2), lambda e, i: (e, i, 0)),
        out_shape=jax.ShapeDtypeStruct((E, D, F2), BF16),
        compiler_params=_params(("parallel", "parallel"), 32),
    )(w_up, jnp.asarray(perm, dtype=BF16))


def _moe_up_kernel(tok_ref, be_ref, bv_ref, h_hbm, w_ref, b_ref, act_ref, xbuf_ref, sem, *, rb, nb):
    i = pl.program_id(0)

    def gather(blk, slot):
        def issue(r, carry):
            _row_copy(h_hbm, tok_ref[blk * rb + r], xbuf_ref.at[slot], r, sem.at[slot]).start()
            return carry

        lax.fori_loop(0, rb, issue, 0, unroll=ISSUE_UNROLL)

    def drain(slot):
        _rows_copy(h_hbm, xbuf_ref.at[slot], sem.at[slot], rb).wait()

    @pl.when((i == 0) & (bv_ref[0] != 0))
    def _():
        for a in range(GATHER_AHEAD):
            gather(a, a)

    slot = i % NBUF
    valid = bv_ref[i] != 0
    started = bv_ref[jnp.maximum(i - GATHER_AHEAD, 0)] != 0

    @pl.when(jnp.logical_not(valid))
    def _():
        act_ref[...] = jnp.zeros_like(act_ref)

    @pl.when(jnp.logical_not(valid) & started)
    def _():
        drain(slot)

    @pl.when(valid)
    def _():
        drain(slot)
        xb = xbuf_ref[slot].astype(BF16)
        nxt = jnp.minimum(i + GATHER_AHEAD, nb - 1)
        nslot = (i + GATHER_AHEAD) % NBUF
        for r in range(rb):
            _row_copy(h_hbm, tok_ref[nxt * rb + r], xbuf_ref.at[nslot], r, sem.at[nslot]).start()
        hcat = jnp.dot(xb, w_ref[0], preferred_element_type=F32) + b_ref[0]
        for c in range(hcat.shape[1] // PAIR):
            gate = jnp.minimum(hcat[:, c * PAIR:c * PAIR + LANES], SWIGLU_LIMIT)
            up = jnp.clip(hcat[:, c * PAIR + LANES:(c + 1) * PAIR], -SWIGLU_LIMIT, SWIGLU_LIMIT)
            act = gate * jax.nn.sigmoid(SWIGLU_ALPHA * gate) * (up + 1.0)
            act_ref[:, c * LANES:(c + 1) * LANES] = act.astype(act_ref.dtype)

    for a in range(GATHER_AHEAD):
        j = nb - 1 - a

        @pl.when((i == nb - 1) & (bv_ref[j] != 0))
        def _():
            drain((j + GATHER_AHEAD) % NBUF)


def _moe_up(row_tok, blk_e, blk_valid, h, w_pairs, b_pairs):
    n_rows = row_tok.shape[0]
    rb = EXPERT_ROW_BLOCK
    nb = n_rows // rb
    assert nb > GATHER_AHEAD
    E, D, F2 = w_pairs.shape
    wmap = lambda i, tok, be, bv: (be[i], 0, 0)
    return pl.pallas_call(
        functools.partial(_moe_up_kernel, rb=rb, nb=nb),
        name="moe_gather_up_swiglu",
        grid_spec=pltpu.PrefetchScalarGridSpec(
            num_scalar_prefetch=3,
            grid=(n_rows // rb,),
            in_specs=[
                pl.BlockSpec(memory_space=pl.ANY),
                pl.BlockSpec((1, D, F2), wmap),
                pl.BlockSpec((1, 1, F2), wmap),
            ],
            out_specs=pl.BlockSpec((rb, F2 // 2), lambda i, tok, be, bv: (i, 0)),
            scratch_shapes=[pltpu.VMEM((NBUF, rb, D), F32), pltpu.SemaphoreType.DMA((NBUF,))],
        ),
        out_shape=jax.ShapeDtypeStruct((n_rows, F2 // 2), BF16),
        compiler_params=_params(("arbitrary",), 56),
    )(row_tok, blk_e, blk_valid, h, w_pairs, b_pairs)


def _moe_down_kernel(be_ref, bv_ref, act_ref, wd_ref, bd_ref, rw_ref, y_ref, wb_ref):
    i = pl.program_id(0)

    @pl.when(bv_ref[i] == 0)
    def _():
        y_ref[...] = jnp.zeros_like(y_ref)

    @pl.when((bv_ref[i] != 0) & ((i == 0) | (be_ref[i] != be_ref[jnp.maximum(i - 1, 0)])))
    def _():
        wb_ref[...] = wd_ref[0].astype(BF16)

    @pl.when(bv_ref[i] != 0)
    def _():
        y = (jnp.dot(act_ref[...], wb_ref[...], preferred_element_type=F32) + bd_ref[0]) * rw_ref[...]
        bits = pltpu.bitcast(y.astype(BF16).astype(F32), jnp.uint32)
        half = bits.shape[1] // 2
        y_ref[...] = (bits[:, :half] >> 16) | (bits[:, half:] & jnp.uint32(0xFFFF0000))


def _moe_down(blk_e, blk_valid, act, wd, bd, row_w):
    n_rows, F = act.shape
    rb = EXPERT_ROW_BLOCK
    E, _, D = wd.shape
    wmap = lambda i, be, bv: (be[i], 0, 0)
    return pl.pallas_call(
        _moe_down_kernel,
        name="moe_down",
        grid_spec=pltpu.PrefetchScalarGridSpec(
            num_scalar_prefetch=2,
            grid=(n_rows // rb,),
            in_specs=[
                pl.BlockSpec((rb, F), lambda i, be, bv: (i, 0)),
                pl.BlockSpec((1, F, D), wmap),
                pl.BlockSpec((1, 1, D), wmap),
                pl.BlockSpec((rb, 1), lambda i, be, bv: (i, 0)),
            ],
            out_specs=pl.BlockSpec((rb, D // 2), lambda i, be, bv: (i, 0)),
            scratch_shapes=[pltpu.VMEM((F, D), BF16)],
        ),
        out_shape=jax.ShapeDtypeStruct((n_rows, D // 2), jnp.uint32),
        compiler_params=_params(("arbitrary",), 56),
    )(blk_e, blk_valid, act, wd, bd, row_w.reshape(n_rows, 1))


def _ple_kernel(hb_ref, w_ref, bg_ref, p_ref, wp_ref, h_ref, z_ref, *, alpha):
    k = pl.program_id(2)

    @pl.when(k == 0)
    def _():
        z_ref[...] = jnp.zeros_like(z_ref)

    z_ref[...] += jnp.dot(hb_ref[...], w_ref[...], preferred_element_type=F32)

    @pl.when(k == pl.num_programs(2) - 1)
    def _():
        gate = jax.nn.sigmoid(z_ref[...] + bg_ref[...])
        emb = jnp.dot(p_ref[...].astype(BF16), wp_ref[...], preferred_element_type=F32)
        z_ref[...] = alpha * h_ref[...] + gate * emb


def _ple_residual(hb, w_gate_b, b_gate, p2, w_ple_b, h, alpha):
    T, D = h.shape
    pd = p2.shape[1]
    tm = _pick_tile(T, (1024, 512, 256, 128))
    tn = _pick_tile(D, (1024, 512, 256, 128))
    tk = _pick_tile(D, (2048, 1024, 512, 256, 128))
    return pl.pallas_call(
        functools.partial(_ple_kernel, alpha=alpha),
        name="ple_gate_residual",
        grid=(T // tm, D // tn, D // tk),
        in_specs=[
            pl.BlockSpec((tm, tk), lambda i, j, k: (i, k)),
            pl.BlockSpec((tk, tn), lambda i, j, k: (k, j)),
            pl.BlockSpec((1, tn), lambda i, j, k: (0, j)),
            pl.BlockSpec((tm, pd), lambda i, j, k: (i, 0)),
            pl.BlockSpec((pd, tn), lambda i, j, k: (0, j)),
            pl.BlockSpec((tm, tn), lambda i, j, k: (i, j)),
        ],
        out_specs=pl.BlockSpec((tm, tn), lambda i, j, k: (i, j)),
        out_shape=jax.ShapeDtypeStruct((T, D), F32),
        compiler_params=_params(("parallel", "parallel", "arbitrary"), 48),
    )(hb, w_gate_b, b_gate.reshape(1, D), p2, w_ple_b, h)


def _combine_kernel(pos_ref, y_hbm, z_ref, g_ref, b_ref, o_ref, buf_ref, sem, *, tc, topk, nb):
    i = pl.program_id(0)

    def gather(blk, slot):
        def issue(t, carry):
            for k in range(topk):
                _row_copy(y_hbm, pos_ref[(blk * tc + t) * topk + k], buf_ref.at[slot, k], t, sem.at[slot]).start()
            return carry

        lax.fori_loop(0, tc, issue, 0, unroll=ISSUE_UNROLL // topk)

    def drain(slot):
        for k in range(topk):
            _rows_copy(y_hbm, buf_ref.at[slot, k], sem.at[slot], tc).wait()

    @pl.when(i == 0)
    def _():
        for a in range(GATHER_AHEAD):
            gather(a, a)

    slot = i % NBUF
    drain(slot)
    nxt = jnp.minimum(i + GATHER_AHEAD, nb - 1)
    nslot = (i + GATHER_AHEAD) % NBUF
    for t in range(tc):
        for k in range(topk):
            _row_copy(y_hbm, pos_ref[(nxt * tc + t) * topk + k], buf_ref.at[nslot, k], t, sem.at[nslot]).start()
    lo = hi = None
    for k in range(topk):
        word = buf_ref[slot, k]
        lo_k = pltpu.bitcast(word << 16, F32)
        hi_k = pltpu.bitcast(word & jnp.uint32(0xFFFF0000), F32)
        lo = lo_k if lo is None else lo + lo_k
        hi = hi_k if hi is None else hi + hi_k
    ffn = jnp.concatenate([lo, hi], axis=1)
    o_ref[...] = _layer_norm(z_ref[...] + ffn, g_ref[...], b_ref[...])

    @pl.when(i == nb - 1)
    def _():
        for a in range(GATHER_AHEAD):
            drain((nb - 1 - a + GATHER_AHEAD) % NBUF)


def _combine_ln(pos, y, z, ln_g, ln_b):
    T, D = z.shape
    tc = _pick_tile(T, (128, 64, 32))
    row = lambda i, pos: (i, 0)
    const = lambda i, pos: (0, 0)
    nb = T // tc
    assert nb > GATHER_AHEAD
    return pl.pallas_call(
        functools.partial(_combine_kernel, tc=tc, topk=TOP_K, nb=nb),
        name="moe_combine_ln2",
        grid_spec=pltpu.PrefetchScalarGridSpec(
            num_scalar_prefetch=1,
            grid=(T // tc,),
            in_specs=[
                pl.BlockSpec(memory_space=pl.ANY),
                pl.BlockSpec((tc, D), row),
                pl.BlockSpec((1, D), const),
                pl.BlockSpec((1, D), const),
            ],
            out_specs=pl.BlockSpec((tc, D), row),
            scratch_shapes=[pltpu.VMEM((NBUF, TOP_K, tc, D // 2), jnp.uint32), pltpu.SemaphoreType.DMA((NBUF,))],
        ),
        out_shape=jax.ShapeDtypeStruct((T, D), F32),
        compiler_params=_params(("arbitrary",), 32),
    )(pos, y, z, ln_g.reshape(1, D), ln_b.reshape(1, D))


def _routing_tables(logits, n_experts):
    T = logits.shape[0]
    A = T * TOP_K
    rb = EXPERT_ROW_BLOCK
    i32 = jnp.int32
    top_logit, top_e = lax.top_k(logits[:, :n_experts], TOP_K)
    top_w = jax.nn.softmax(top_logit, axis=-1)
    flat_e = top_e.reshape(-1).astype(i32)
    order = jnp.argsort(flat_e).astype(i32)
    sorted_pos = jnp.argsort(order).astype(i32)
    onehot = (flat_e[:, None] == jnp.arange(n_experts, dtype=i32)[None, :]).astype(i32)
    counts = jnp.sum(onehot, axis=0)
    padded = (counts + rb - 1) // rb * rb
    pad_end = jnp.cumsum(padded)
    pad_start = pad_end - padded
    grp_start = jnp.cumsum(counts) - counts
    pos = (jnp.sum(onehot * (pad_start - grp_start)[None, :], axis=1) + sorted_pos).astype(i32)
    n_blocks = -(-A // rb) + n_experts
    blk_start = jnp.arange(n_blocks, dtype=i32) * rb
    blk_e = jnp.minimum(jnp.sum((pad_end[None, :] <= blk_start[:, None]).astype(i32), axis=1), n_experts - 1)
    blk_valid = (blk_start < pad_end[-1]).astype(i32)
    off = (blk_start - pad_start[blk_e])[:, None] + jnp.arange(rb, dtype=i32)[None, :]
    valid = off < counts[blk_e][:, None]
    src = order[jnp.clip(grp_start[blk_e][:, None] + off, 0, A - 1)]
    row_tok = jnp.where(valid, src // TOP_K, 0).astype(i32).reshape(-1)
    row_w = jnp.where(valid, top_w.reshape(-1)[src], 0.0).reshape(-1)
    return row_tok, row_w, blk_e.astype(i32), blk_valid, pos


def _layer(x2, p2, B, S, w_in, w_pool, pool_scale, cmp_k_pe, cmp_k_w1, cmp_k_w2, cmp_v_pe, cmp_v_w1, cmp_v_w2,
           w_out, ln1_g, ln1_b, w_router, b_router, w_up, b_up, w_down, b_down,
           w_ple_gate, b_ple_gate, w_ple, ln2_g, ln2_b, alpha):
    T, D = x2.shape
    pool_w = D // 2
    nsa_w = D - pool_w
    nh = nsa_w // HEAD_DIM
    in_w = w_in.shape[1]
    kvw = (in_w - D - N_GATES * nh) // N_KV_GROUPS_FIELDS
    G = kvw // HEAD_DIM
    hg = nh // G
    n_experts = w_router.shape[1]

    xb = x2.astype(BF16)
    w_in_b = w_in.astype(BF16)
    proj_a = _matmul_cols(xb, w_in_b, 0, D)
    proj_b = _matmul_cols(xb, w_in_b, D, 2 * kvw)
    proj_c = _matmul_cols(xb, w_in_b, D + 2 * kvw, 4 * kvw)
    w_gates = jnp.pad(w_in_b[:, D + 6 * kvw:], ((0, 0), (0, LANES - N_GATES * nh)))
    gates = _matmul_cols(xb, w_gates, 0, LANES)

    y_pool = _pool_mixer(proj_a, w_pool.astype(BF16), pool_scale, B, S, pool_w)

    cos2, sin2 = _rope_tables(jnp.arange(S))
    nc = S // CMP_STRIDE
    ccos, csin = _rope_tables(jnp.arange(nc) * CMP_STRIDE + CMP_BLOCK - 1)
    kc = _compress(proj_b, 0, cmp_k_pe, cmp_k_w1, cmp_k_w2, ccos, csin, B, S, G, rope=True)
    vc = _compress(proj_b, 1, cmp_v_pe, cmp_v_w1, cmp_v_w2, ccos, csin, B, S, G, rope=False)
    q_rot, k_slc, v_slc, k_win, v_win = _rope_prep(proj_a, proj_c, cos2, sin2, B, S, nsa_w, G)
    o_cmp, sel_bias = _cmp_select(q_rot, kc, vc, B, S, G, hg)
    tq = _pick_tile(S, (256, 128))
    o_slc = _selected_attention(q_rot, k_slc, v_slc, sel_bias, B, S, G, hg, tq)
    gates_g = gates[:, :N_GATES * nh].reshape(T, G, N_GATES * hg).transpose(1, 0, 2)
    gates_g = jnp.pad(gates_g, ((0, 0), (0, 0), (0, LANES - N_GATES * hg)))
    y_nsa = _window_attention_gated(q_rot, k_win, v_win, gates_g, o_cmp, o_slc, B, S, G, hg, tq, WINDOW)

    ne_pad = -(-n_experts // LANES) * LANES
    w_router_p = jnp.pad(w_router, ((0, 0), (0, ne_pad - n_experts))).astype(BF16)
    b_router_p = jnp.pad(b_router, (0, ne_pad - n_experts)).reshape(1, ne_pad)
    h, hb, logits = _mix_ln_router(y_pool, y_nsa, w_out.astype(BF16), x2, ln1_g, ln1_b,
                                   w_router_p, b_router_p, alpha)

    row_tok, row_w, blk_e, blk_valid, pos = _routing_tables(logits, n_experts)
    F = w_down.shape[1]
    w_pairs = _pair_split_cast(w_up)
    b_pairs = b_up.reshape(n_experts, F // LANES, LANES, 2).transpose(0, 1, 3, 2).reshape(n_experts, 1, 2 * F)
    act = _moe_up(row_tok, blk_e, blk_valid, h, w_pairs, b_pairs)
    y = _moe_down(blk_e, blk_valid, act, w_down, b_down.reshape(n_experts, 1, D), row_w)

    z = _ple_residual(hb, w_ple_gate.astype(BF16), b_ple_gate, p2, w_ple.astype(BF16), h, alpha)
    return _combine_ln(pos, y, z, ln2_g, ln2_b)


def kernel(x, p, w_in, w_pool, pool_scale, cmp_k_pe, cmp_k_w1, cmp_k_w2, cmp_v_pe, cmp_v_w1, cmp_v_w2, w_out, ln1_g, ln1_b, w_router, b_router, w_up, b_up, w_down, b_down, w_ple_gate, b_ple_gate, w_ple, ln2_g, ln2_b):
    B, S, D = x.shape
    depth = w_in.shape[0]
    alpha = (2 * depth) ** 0.25
    x2 = x.reshape(B * S, D)
    for i in range(depth):
        x2 = _layer(x2, p[i].reshape(B * S, -1), B, S, w_in[i], w_pool[i], pool_scale[i],
                    cmp_k_pe[i], cmp_k_w1[i], cmp_k_w2[i], cmp_v_pe[i], cmp_v_w1[i], cmp_v_w2[i],
                    w_out[i], ln1_g[i], ln1_b[i], w_router[i], b_router[i], w_up[i], b_up[i],
                    w_down[i], b_down[i], w_ple_gate[i], b_ple_gate[i], w_ple[i], ln2_g[i], ln2_b[i], alpha)
    return x2.reshape(B, S, D)
```

```python
import functools
import math

import jax
import jax.numpy as jnp
import numpy as np
from jax import lax
from jax.experimental import pallas as pl
from jax.experimental.pallas import tpu as pltpu

HEAD_DIM = 128
LANES = 128
N_KV_GROUPS_FIELDS = 6
N_GATES = 3
POOL_WINDOWS = (2, 4, 8, 16)
CMP_BLOCK = 32
CMP_STRIDE = 16
SEL_BLOCK = 64
N_SEL = 16
WINDOW = 512
ROPE_THETA = 10000.0
TOP_K = 4
SWIGLU_LIMIT = 7.0
SWIGLU_ALPHA = 1.702
EXPERT_ROW_BLOCK = 256
LN_EPS = 1e-5
NEG_INF = -1e30
FORCE_SCORE = 1e4
LOG2E = 1.4426950408889634
ISSUE_UNROLL = 8
KEY_TILE_GROUP = 8
GATHER_AHEAD = 2
NBUF = GATHER_AHEAD + 1
MIB = 1024 * 1024

BF16 = jnp.bfloat16
F32 = jnp.float32
NT_DIMS = (((1,), (1,)), ((), ()))


def _params(sem, vmem_mib=None):
    kw = dict(dimension_semantics=sem)
    if vmem_mib is not None:
        kw["vmem_limit_bytes"] = vmem_mib * MIB
    return pltpu.CompilerParams(**kw)


def _pick_tile(n, candidates):
    for c in candidates:
        if n % c == 0:
            return c
    raise ValueError(f"no tile for {n}")


def _mm_kernel(x_ref, w_ref, o_ref):
    @pl.when(pl.program_id(2) == 0)
    def _():
        o_ref[...] = jnp.zeros_like(o_ref)

    o_ref[...] += jnp.dot(x_ref[...], w_ref[...], preferred_element_type=F32)


def _matmul_cols(x, w, col0, ncols):
    M, K = x.shape
    tm = _pick_tile(M, (1024, 512, 256))
    tk = _pick_tile(K, (2048, 1024, 512, 256, 128))
    tn = _pick_tile(math.gcd(col0, ncols) if col0 else ncols, (1024, 512, 256, 128))
    off = col0 // tn
    return pl.pallas_call(
        _mm_kernel,
        name="in_proj",
        grid=(M // tm, ncols // tn, K // tk),
        in_specs=[pl.BlockSpec((tm, tk), lambda i, j, k: (i, k)),
                  pl.BlockSpec((tk, tn), lambda i, j, k: (k, j + off))],
        out_specs=pl.BlockSpec((tm, tn), lambda i, j, k: (i, j)),
        out_shape=jax.ShapeDtypeStruct((M, ncols), F32),
        compiler_params=_params(("parallel", "parallel", "arbitrary"), 40),
    )(x, w)


def _pool_kernel(u_ref, prev_ref, w_ref, sc_ref, o_ref, ext_ref, *, ts, halo):
    g = pl.program_id(0)
    i = pl.program_id(2)
    cur = u_ref[...]
    ext_ref[pl.ds(halo, ts), :] = cur

    @pl.when(i == 0)
    def _():
        ext_ref[pl.ds(0, halo), :] = jnp.zeros((halo, cur.shape[1]), F32)

    @pl.when(i > 0)
    def _():
        ext_ref[pl.ds(0, halo), :] = prev_ref[...]

    def back(d):
        return ext_ref[pl.ds(halo - d, ts), :]

    t = i * ts + lax.broadcasted_iota(jnp.int32, (ts, 1), 0)
    for gi, win in enumerate(POOL_WINDOWS):
        @pl.when(g == gi)
        def _(win=win):
            ssum = cur
            for d in range(1, win):
                ssum = ssum + back(d)
            cnt = jnp.minimum(t + 1, win).astype(F32)
            pooled = ssum / cnt - cur
            mixed = jnp.dot(pooled.astype(BF16), w_ref[0], preferred_element_type=F32)
            o_ref[...] = (mixed * sc_ref[...]).astype(o_ref.dtype)


def _pool_mixer(proj_a, w_pool_b, pool_scale, B, S, pool_w):
    T = B * S
    ng = len(POOL_WINDOWS)
    pg = pool_w // ng
    halo = POOL_WINDOWS[-1]
    ts = _pick_tile(S, (512, 256, 128))
    ns = S // ts
    hb = ts // halo
    kern = functools.partial(_pool_kernel, ts=ts, halo=halo)
    return pl.pallas_call(
        kern,
        name="pool_mixer",
        grid=(ng, B, ns),
        in_specs=[
            pl.BlockSpec((ts, pg), lambda g, b, i: (b * ns + i, g)),
            pl.BlockSpec((halo, pg), lambda g, b, i: (jnp.maximum((b * ns + i) * hb - 1, 0), g)),
            pl.BlockSpec((1, pg, pg), lambda g, b, i: (g, 0, 0)),
            pl.BlockSpec((1, pg), lambda g, b, i: (0, g)),
        ],
        out_specs=pl.BlockSpec((ts, pg), lambda g, b, i: (b * ns + i, g)),
        out_shape=jax.ShapeDtypeStruct((T, pool_w), BF16),
        scratch_shapes=[pltpu.VMEM((ts + halo, pg), F32)],
        compiler_params=_params(("parallel", "parallel", "arbitrary"), 32),
    )(proj_a, proj_a, w_pool_b, pool_scale.reshape(1, pool_w))


def _rope_tables(pos):
    inv = ROPE_THETA ** (-jnp.arange(0, HEAD_DIM, 2, dtype=F32) / HEAD_DIM)
    ang = pos.astype(F32)[:, None] * inv[None, :]
    c, s = jnp.cos(ang), jnp.sin(ang)
    return jnp.concatenate([c, c], axis=-1), jnp.concatenate([-s, s], axis=-1)


def _rope(x, cos2, sin2):
    return x * cos2 + pltpu.roll(x, HEAD_DIM // 2, 1) * sin2


def _compress_kernel(x_ref, pe_ref, w1_ref, w2_ref, cos_ref, sin_ref, o_ref, *, rope, nl):
    nc = x_ref.shape[0] // nl
    xs = [x_ref[pl.ds(l, nc, stride=nl), :] for l in range(nl)]
    lo = jnp.concatenate([(xs[l] + pe_ref[l:l + 1, :]).astype(BF16) for l in range(nl)], axis=1)
    hi = jnp.concatenate([(xs[l] + pe_ref[nl + l:nl + l + 1, :]).astype(BF16) for l in range(nl)], axis=1)
    half = nl * HEAD_DIM
    a = jnp.dot(lo, w1_ref[0:half, :], preferred_element_type=F32)
    b = jnp.dot(hi, w1_ref[half:2 * half, :], preferred_element_type=F32)
    hid = jax.nn.gelu(a + pltpu.roll(b, nc - 1, 0))
    out = jnp.dot(hid.astype(BF16), w2_ref[...], preferred_element_type=F32)
    if rope:
        out = _rope(out, cos_ref[...], sin_ref[...])
    o_ref[0, 0] = out.astype(o_ref.dtype)


def _compress(proj_b, field, pe, w1, w2, cos2, sin2, B, S, G, rope):
    nl = CMP_STRIDE
    nc = S // nl
    kern = functools.partial(_compress_kernel, rope=rope, nl=nl)
    const = lambda b, g: (0, 0)
    return pl.pallas_call(
        kern,
        name="compress_k" if rope else "compress_v",
        grid=(B, G),
        in_specs=[
            pl.BlockSpec((S, HEAD_DIM), lambda b, g: (b, field * G + g)),
            pl.BlockSpec((CMP_BLOCK, HEAD_DIM), const),
            pl.BlockSpec((CMP_BLOCK * HEAD_DIM, HEAD_DIM), const),
            pl.BlockSpec((HEAD_DIM, HEAD_DIM), const),
            pl.BlockSpec((nc, HEAD_DIM), const),
            pl.BlockSpec((nc, HEAD_DIM), const),
        ],
        out_specs=pl.BlockSpec((1, 1, nc, HEAD_DIM), lambda b, g: (b, g, 0, 0)),
        out_shape=jax.ShapeDtypeStruct((B, G, nc, HEAD_DIM), BF16),
        compiler_params=_params(("parallel", "parallel"), 32),
    )(proj_b, pe, w1.reshape(CMP_BLOCK * HEAD_DIM, HEAD_DIM).astype(BF16), w2.astype(BF16), cos2, sin2)


def _rope_prep_kernel(q_ref, kv_ref, cos_ref, sin_ref, qo_ref, ks_ref, vs_ref, kw_ref, vw_ref, *, nh, G, ts):
    i = pl.program_id(1)
    c = cos_ref[...]
    s = sin_ref[...]
    kvw = G * HEAD_DIM
    for h in range(nh):
        qo_ref[0, h] = _rope(q_ref[:, h * HEAD_DIM:(h + 1) * HEAD_DIM], c, s).astype(BF16)
    blk = (i * ts + lax.broadcasted_iota(jnp.int32, (ts, LANES), 0)) // SEL_BLOCK
    onehot = (blk == lax.broadcasted_iota(jnp.int32, (ts, LANES), 1)).astype(BF16)
    for g in range(G):
        lo = g * HEAD_DIM
        ks_ref[0, g, :, 0:HEAD_DIM] = _rope(kv_ref[:, lo:lo + HEAD_DIM], c, s).astype(BF16)
        ks_ref[0, g, :, HEAD_DIM:2 * HEAD_DIM] = onehot
        vs_ref[0, g] = kv_ref[:, kvw + lo:kvw + lo + HEAD_DIM].astype(BF16)
        kw_ref[0, g] = _rope(kv_ref[:, 2 * kvw + lo:2 * kvw + lo + HEAD_DIM], c, s).astype(BF16)
        vw_ref[0, g] = kv_ref[:, 3 * kvw + lo:3 * kvw + lo + HEAD_DIM].astype(BF16)


def _rope_prep(proj_a, proj_c, cos2, sin2, B, S, nsa_w, G):
    nh = nsa_w // HEAD_DIM
    ts = _pick_tile(S, (512, 256, 128))
    ns = S // ts
    kern = functools.partial(_rope_prep_kernel, nh=nh, G=G, ts=ts)
    hm = lambda b, i: (b, 0, i, 0)
    return pl.pallas_call(
        kern,
        name="rope_prep",
        grid=(B, ns),
        in_specs=[
            pl.BlockSpec((ts, nsa_w), lambda b, i: (b * ns + i, 1)),
            pl.BlockSpec((ts, 4 * G * HEAD_DIM), lambda b, i: (b * ns + i, 0)),
            pl.BlockSpec((ts, HEAD_DIM), lambda b, i: (i, 0)),
            pl.BlockSpec((ts, HEAD_DIM), lambda b, i: (i, 0)),
        ],
        out_specs=[
            pl.BlockSpec((1, nh, ts, HEAD_DIM), hm),
            pl.BlockSpec((1, G, ts, 2 * HEAD_DIM), hm),
            pl.BlockSpec((1, G, ts, HEAD_DIM), hm),
            pl.BlockSpec((1, G, ts, HEAD_DIM), hm),
            pl.BlockSpec((1, G, ts, HEAD_DIM), hm),
        ],
        out_shape=[
            jax.ShapeDtypeStruct((B, nh, S, HEAD_DIM), BF16),
            jax.ShapeDtypeStruct((B, G, S, 2 * HEAD_DIM), BF16),
            jax.ShapeDtypeStruct((B, G, S, HEAD_DIM), BF16),
            jax.ShapeDtypeStruct((B, G, S, HEAD_DIM), BF16),
            jax.ShapeDtypeStruct((B, G, S, HEAD_DIM), BF16),
        ],
        compiler_params=_params(("parallel", "parallel"), 40),
    )(proj_a, proj_c, cos2, sin2)


def _cmp_select_kernel(q_ref, kc_ref, vc_ref, ovl_ref, o_ref, bias_ref, *, hg, tq, n_sel, n_keep, scale):
    i = pl.program_id(2)
    nc = kc_ref.shape[2]
    kc = kc_ref[0, 0]
    vc = vc_ref[0, 0]
    t = i * tq + lax.broadcasted_iota(jnp.int32, (tq, nc), 0)
    n = lax.broadcasted_iota(jnp.int32, (tq, nc), 1)
    valid = (n * CMP_STRIDE + (CMP_BLOCK - 1) <= t) & (n < nc - 1)
    ovl = ovl_ref[...]
    imp = jnp.zeros((n_sel, tq), F32)
    for h in range(hg):
        s = lax.dot_general(q_ref[0, h], kc, NT_DIMS, preferred_element_type=F32) * scale
        s = jnp.where(valid, s, NEG_INF)
        m = jnp.max(s, axis=-1, keepdims=True)
        e = jnp.where(valid, jnp.exp(s - m), 0.0)
        den = jnp.sum(e, axis=-1, keepdims=True)
        pc = (e / jnp.where(den > 0.0, den, 1.0)).astype(BF16)
        o_ref[0, h] = jnp.dot(pc, vc, preferred_element_type=F32)
        imp = imp + lax.dot_general(ovl, pc, NT_DIMS, preferred_element_type=F32)
    j = lax.broadcasted_iota(jnp.int32, (n_sel, tq), 0)
    tb = (i * tq + lax.broadcasted_iota(jnp.int32, (n_sel, tq), 1)) // SEL_BLOCK
    forced = (j == 0) | (j == tb) | (j == tb - 1)
    score = jnp.where(forced, FORCE_SCORE, jnp.where(j <= tb, imp, -1.0))
    rank = jnp.zeros((n_sel, tq), jnp.int32)
    for jp in range(n_sel):
        row = score[jp:jp + 1, :]
        ahead = (row > score) | ((row == score) & (jp < j))
        rank = rank + ahead.astype(jnp.int32)
    bias_t = jnp.where(rank < n_keep, 0.0, NEG_INF)
    if n_sel < LANES:
        bias_t = jnp.concatenate([bias_t, jnp.zeros((LANES - n_sel, tq), F32)], axis=0)
    bias_ref[0, 0] = bias_t.T.astype(BF16)


def _overlap_t(n_cmp_pad, n_sel):
    n_cmp = n_cmp_pad - 1
    cs = np.arange(n_cmp) * CMP_STRIDE
    ce = cs + CMP_BLOCK
    ss = np.arange(n_sel) * SEL_BLOCK
    se = ss + SEL_BLOCK
    ov = np.clip(np.minimum(ce[:, None], se[None, :]) - np.maximum(cs[:, None], ss[None, :]), 0, None)
    out = np.zeros((n_sel, n_cmp_pad), np.float32)
    out[:, :n_cmp] = (ov / CMP_BLOCK).T
    return jnp.asarray(out, dtype=BF16)


def _cmp_select(q_rot, kc, vc, B, S, G, hg):
    nc = kc.shape[2]
    n_sel = S // SEL_BLOCK
    assert n_sel <= LANES and n_sel % 8 == 0
    tq = _pick_tile(S, (512, 256, 128))
    nq = S // tq
    nh = G * hg
    kern = functools.partial(_cmp_select_kernel, hg=hg, tq=tq, n_sel=n_sel,
                             n_keep=min(N_SEL, n_sel), scale=HEAD_DIM ** -0.5)
    return pl.pallas_call(
        kern,
        name="cmp_attn_select",
        grid=(B, G, nq),
        in_specs=[
            pl.BlockSpec((1, hg, tq, HEAD_DIM), lambda b, g, i: (b, g, i, 0)),
            pl.BlockSpec((1, 1, nc, HEAD_DIM), lambda b, g, i: (b, g, 0, 0)),
            pl.BlockSpec((1, 1, nc, HEAD_DIM), lambda b, g, i: (b, g, 0, 0)),
            pl.BlockSpec((n_sel, nc), lambda b, g, i: (0, 0)),
        ],
        out_specs=[
            pl.BlockSpec((1, hg, tq, HEAD_DIM), lambda b, g, i: (b, g, i, 0)),
            pl.BlockSpec((1, 1, tq, LANES), lambda b, g, i: (b, g, i, 0)),
        ],
        out_shape=[
            jax.ShapeDtypeStruct((B, nh, S, HEAD_DIM), F32),
            jax.ShapeDtypeStruct((B, G, S, LANES), BF16),
        ],
        compiler_params=_params(("parallel", "parallel", "parallel"), 40),
    )(q_rot, kc, vc, _overlap_t(nc, n_sel))


def _lane_fold(x, op):
    r = x[:, :LANES]
    for a in range(1, x.shape[1] // LANES):
        r = op(r, x[:, a * LANES:(a + 1) * LANES])
    return r


def _grouped_loop(n, tile_fn, combine, commit, group):
    def run(start, count):
        total = tile_fn(start)
        for u in range(1, count):
            total = combine(total, tile_fn(start + u))
        commit(total)

    def body(g, carry):
        run(g * group, group)
        return carry

    lax.fori_loop(0, n // group, body, 0)
    rem = n % group
    step = group // 2
    while step >= 1:
        @pl.when((rem & step) != 0)
        def _(step=step):
            run(n - rem + (rem & ~(2 * step - 1)), step)

        step //= 2


def _slc_kernel(q_ref, bias_ref, k_ref, v_ref, o_ref, s_ref, m_ref, l_ref, acc_ref, *, hg, tq, scale):
    i = pl.program_id(2)
    rows = hg * tq
    c = scale * LOG2E
    q = q_ref[0].reshape(rows, HEAD_DIM)
    b = bias_ref[0, 0]
    qa = jnp.concatenate([q, jnp.concatenate([b] * hg, axis=0)], axis=1)

    def k_tile(j):
        return k_ref[0, 0, pl.ds(pl.multiple_of(j * tq, tq), tq), :]

    def v_tile(j):
        return v_ref[0, 0, pl.ds(pl.multiple_of(j * tq, tq), tq), :]

    m_ref[...] = jnp.full_like(m_ref, NEG_INF)

    def scores(j):
        s = lax.dot_general(qa, k_tile(j), NT_DIMS, preferred_element_type=F32) * c
        s_ref[j] = s
        return _lane_fold(s, jnp.maximum)

    def commit_max(m):
        m_ref[...] = jnp.maximum(m_ref[...], m)

    _grouped_loop(i, scores, jnp.maximum, commit_max, KEY_TILE_GROUP)
    sd = lax.dot_general(qa, k_tile(i), NT_DIMS, preferred_element_type=F32) * c
    tpos = lax.broadcasted_iota(jnp.int32, (rows, tq), 0) & (tq - 1)
    kpos = lax.broadcasted_iota(jnp.int32, (rows, tq), 1)
    sd = jnp.where(kpos <= tpos, sd, NEG_INF)
    mrow = jnp.max(jnp.maximum(m_ref[...], _lane_fold(sd, jnp.maximum)), axis=-1, keepdims=True)
    m_ref[...] = jnp.broadcast_to(mrow, (rows, LANES))
    l_ref[...] = jnp.zeros_like(l_ref)
    acc_ref[...] = jnp.zeros_like(acc_ref)

    def probs(s):
        mfull = jnp.concatenate([m_ref[...]] * (tq // LANES), axis=1)
        return jnp.exp2(s - mfull)

    def weighted(j):
        p = probs(s_ref[j])
        return _lane_fold(p, jnp.add), jnp.dot(p.astype(BF16), v_tile(j), preferred_element_type=F32)

    def commit_sums(t):
        l_ref[...] += t[0]
        acc_ref[...] += t[1]

    _grouped_loop(i, weighted, lambda a, b: (a[0] + b[0], a[1] + b[1]), commit_sums, KEY_TILE_GROUP)
    p = probs(sd)
    den = jnp.sum(l_ref[...] + _lane_fold(p, jnp.add), axis=-1, keepdims=True)
    acc = acc_ref[...] + jnp.dot(p.astype(BF16), v_tile(i), preferred_element_type=F32)
    o_ref[0] = (acc / den).reshape(hg, tq, HEAD_DIM)


def _selected_attention(q_rot, k_aug, v, bias, B, S, G, hg, tq):
    assert tq & (tq - 1) == 0 and tq % LANES == 0
    nq = S // tq
    rows = hg * tq
    kern = functools.partial(_slc_kernel, hg=hg, tq=tq, scale=HEAD_DIM ** -0.5)
    return pl.pallas_call(
        kern,
        name="selected_attn",
        grid=(B, G, nq),
        in_specs=[
            pl.BlockSpec((1, hg, tq, HEAD_DIM), lambda b, g, i: (b, g, i, 0)),
            pl.BlockSpec((1, 1, tq, LANES), lambda b, g, i: (b, g, i, 0)),
            pl.BlockSpec((1, 1, S, 2 * HEAD_DIM), lambda b, g, i: (b, g, 0, 0)),
            pl.BlockSpec((1, 1, S, HEAD_DIM), lambda b, g, i: (b, g, 0, 0)),
        ],
        out_specs=pl.BlockSpec((1, hg, tq, HEAD_DIM), lambda b, g, i: (b, g, i, 0)),
        out_shape=jax.ShapeDtypeStruct((B, G * hg, S, HEAD_DIM), F32),
        scratch_shapes=[pltpu.VMEM((nq, rows, tq), F32), pltpu.VMEM((rows, LANES), F32),
                        pltpu.VMEM((rows, LANES), F32), pltpu.VMEM((rows, HEAD_DIM), F32)],
        compiler_params=_params(("parallel", "parallel", "arbitrary"), 48),
    )(q_rot, bias, k_aug, v)


def _win_kernel(q_ref, g_ref, oc_ref, os_ref, *refs, hg, tq, nprev, window, scale):
    nt = nprev + 1
    k_refs, v_refs, y_ref = refs[:nt], refs[nt:2 * nt], refs[2 * nt]
    i = pl.program_id(2)
    gate = jax.nn.sigmoid(g_ref[0])
    c = scale * LOG2E
    nk = nt * tq
    kcat = jnp.concatenate([r[0, 0] for r in k_refs], axis=0)
    vcat = jnp.concatenate([r[0, 0] for r in v_refs], axis=0)
    tpos = i * tq + lax.broadcasted_iota(jnp.int32, (tq, nk), 0)
    kpos = (i - nprev) * tq + lax.broadcasted_iota(jnp.int32, (tq, nk), 1)
    mask = (kpos <= tpos) & (kpos > tpos - window) & (kpos >= 0)
    for h in range(hg):
        s = lax.dot_general(q_ref[0, h], kcat, NT_DIMS, preferred_element_type=F32)
        s = jnp.where(mask, s, NEG_INF)
        m = jnp.max(s, axis=-1, keepdims=True)
        p = jnp.exp2((s - m) * c)
        den = jnp.sum(p, axis=-1, keepdims=True)
        o_win = jnp.dot(p.astype(BF16), vcat, preferred_element_type=F32) / den
        gc = N_GATES * h
        y = (gate[:, gc:gc + 1] * oc_ref[0, h] + gate[:, gc + 1:gc + 2] * os_ref[0, h]
             + gate[:, gc + 2:gc + 3] * o_win)
        y_ref[:, h * HEAD_DIM:(h + 1) * HEAD_DIM] = y.astype(y_ref.dtype)


def _window_attention_gated(q_rot, k, v, gates_g, o_cmp, o_slc, B, S, G, hg, tq, window):
    assert window % tq == 0
    nprev = window // tq
    nq = S // tq
    kern = functools.partial(_win_kernel, hg=hg, tq=tq, nprev=nprev, window=window, scale=HEAD_DIM ** -0.5)

    def kv_spec(a):
        return pl.BlockSpec((1, 1, tq, HEAD_DIM), lambda b, g, i: (b, g, jnp.maximum(i - nprev + a, 0), 0))

    kv_specs = [kv_spec(a) for a in range(nprev + 1)]
    heads = pl.BlockSpec((1, hg, tq, HEAD_DIM), lambda b, g, i: (b, g, i, 0))
    return pl.pallas_call(
        kern,
        name="window_attn_gated_sum",
        grid=(B, G, nq),
        in_specs=[heads, pl.BlockSpec((1, tq, LANES), lambda b, g, i: (g, b * nq + i, 0)), heads, heads]
        + kv_specs + kv_specs,
        out_specs=pl.BlockSpec((tq, hg * HEAD_DIM), lambda b, g, i: (b * nq + i, g)),
        out_shape=jax.ShapeDtypeStruct((B * S, G * hg * HEAD_DIM), BF16),
        compiler_params=_params(("parallel", "parallel", "arbitrary"), 40),
    )(q_rot, gates_g, o_cmp, o_slc, *([k] * (nprev + 1)), *([v] * (nprev + 1)))


def _layer_norm(z, g, b):
    mu = jnp.mean(z, axis=-1, keepdims=True)
    zc = z - mu
    var = jnp.mean(zc * zc, axis=-1, keepdims=True)
    return zc * lax.rsqrt(var + LN_EPS) * g + b


def _mix_kernel(yp_ref, yn_ref, w_ref, x_ref, z_ref, *, nkp, alpha):
    k = pl.program_id(2)

    @pl.when(k == 0)
    def _():
        z_ref[...] = alpha * x_ref[...]

    @pl.when(k < nkp)
    def _():
        z_ref[...] += jnp.dot(yp_ref[...], w_ref[...], preferred_element_type=F32)

    @pl.when(k >= nkp)
    def _():
        z_ref[...] += jnp.dot(yn_ref[...], w_ref[...], preferred_element_type=F32)


def _ln_router_kernel(z_ref, g_ref, b_ref, wr_ref, br_ref, h_ref, hb_ref, lg_ref):
    h = _layer_norm(z_ref[...], g_ref[...], b_ref[...])
    hb = h.astype(BF16)
    h_ref[...] = h
    hb_ref[...] = hb
    lg_ref[...] = jnp.dot(hb, wr_ref[...], preferred_element_type=F32) + br_ref[...]


def _mix_ln_router(y_pool, y_nsa, w_out_b, x2, ln_g, ln_b, w_router_p, b_router_p, alpha):
    T, D = x2.shape
    kp = y_pool.shape[1]
    tm = _pick_tile(T, (1024, 512, 256, 128))
    tn = _pick_tile(D, (1024, 512, 256, 128))
    tk = _pick_tile(math.gcd(kp, y_nsa.shape[1]), (2048, 1024, 512, 256, 128))
    nkp = kp // tk
    nk = nkp + y_nsa.shape[1] // tk
    z = pl.pallas_call(
        functools.partial(_mix_kernel, nkp=nkp, alpha=alpha),
        name="out_proj_residual",
        grid=(T // tm, D // tn, nk),
        in_specs=[
            pl.BlockSpec((tm, tk), lambda i, j, k: (i, jnp.minimum(k, nkp - 1))),
            pl.BlockSpec((tm, tk), lambda i, j, k: (i, jnp.maximum(k - nkp, 0))),
            pl.BlockSpec((tk, tn), lambda i, j, k: (k, j)),
            pl.BlockSpec((tm, tn), lambda i, j, k: (i, j)),
        ],
        out_specs=pl.BlockSpec((tm, tn), lambda i, j, k: (i, j)),
        out_shape=jax.ShapeDtypeStruct((T, D), F32),
        compiler_params=_params(("parallel", "parallel", "arbitrary"), 48),
    )(y_pool, y_nsa, w_out_b, x2)

    ne = w_router_p.shape[1]
    tr = _pick_tile(T, (256, 128))
    row = lambda i: (i, 0)
    const = lambda i: (0, 0)
    return pl.pallas_call(
        _ln_router_kernel,
        name="ln1_router",
        grid=(T // tr,),
        in_specs=[
            pl.BlockSpec((tr, D), row),
            pl.BlockSpec((1, D), const),
            pl.BlockSpec((1, D), const),
            pl.BlockSpec((D, ne), const),
            pl.BlockSpec((1, ne), const),
        ],
        out_specs=[pl.BlockSpec((tr, D), row), pl.BlockSpec((tr, D), row), pl.BlockSpec((tr, ne), row)],
        out_shape=[jax.ShapeDtypeStruct((T, D), F32), jax.ShapeDtypeStruct((T, D), BF16),
                   jax.ShapeDtypeStruct((T, ne), F32)],
        compiler_params=_params(("parallel",), 48),
    )(z, ln_g.reshape(1, D), ln_b.reshape(1, D), w_router_p, b_router_p)


def _row_copy(src_hbm, row, dst_ref, slot, sem):
    return pltpu.make_async_copy(src_hbm.at[pl.ds(row, 1)], dst_ref.at[pl.ds(slot, 1)], sem)


def _rows_copy(src_hbm, dst_ref, sem, n):
    return pltpu.make_async_copy(src_hbm.at[pl.ds(0, n)], dst_ref, sem)


PAIR = 2 * LANES


def _pair_split_kernel(w_ref, perm_ref, o_ref):
    perm = perm_ref[...]
    for c in range(w_ref.shape[2] // PAIR):
        sl = slice(c * PAIR, (c + 1) * PAIR)
        o_ref[0, :, sl] = jnp.dot(w_ref[0, :, sl].astype(BF16), perm, preferred_element_type=F32).astype(BF16)


def _pair_split_cast(w_up):
    E, D, F2 = w_up.shape
    assert F2 % PAIR == 0
    td = _pick_tile(D, (512, 256, 128))
    perm = np.zeros((PAIR, PAIR), np.float32)
    perm[2 * np.arange(LANES), np.arange(LANES)] = 1.0
    perm[2 * np.arange(LANES) + 1, LANES + np.arange(LANES)] = 1.0
    return pl.pallas_call(
        _pair_split_kernel,
        name="w_up_pair_split",
        grid=(E, D // td),
        in_specs=[pl.BlockSpec((1, td, F2), lambda e, i: (e, i, 0)),
                  pl.BlockSpec((PAIR, PAIR), lambda e, i: (0, 0))],
        out_specs=pl.BlockSpec((1, td, F2), lambda e, i: (e, i, 0)),
        out_shape=jax.ShapeDtypeStruct((E, D, F2), BF16),
        compiler_params=_params(("parallel", "parallel"), 32),
    )(w_up, jnp.asarray(perm, dtype=BF16))


def _moe_up_kernel(tok_ref, be_ref, bv_ref, h_hbm, w_ref, b_ref, act_ref, xbuf_ref, sem, *, rb, nb):
    i = pl.program_id(0)

    def gather(blk, slot):
        def issue(r, carry):
            _row_copy(h_hbm, tok_ref[blk * rb + r], xbuf_ref.at[slot], r, sem.at[slot]).start()
            return carry

        lax.fori_loop(0, rb, issue, 0, unroll=ISSUE_UNROLL)

    def drain(slot):
        _rows_copy(h_hbm, xbuf_ref.at[slot], sem.at[slot], rb).wait()

    @pl.when((i == 0) & (bv_ref[0] != 0))
    def _():
        for a in range(GATHER_AHEAD):
            gather(a, a)

    slot = i % NBUF
    valid = bv_ref[i] != 0
    started = bv_ref[jnp.maximum(i - GATHER_AHEAD, 0)] != 0

    @pl.when(jnp.logical_not(valid))
    def _():
        act_ref[...] = jnp.zeros_like(act_ref)

    @pl.when(jnp.logical_not(valid) & started)
    def _():
        drain(slot)

    @pl.when(valid)
    def _():
        drain(slot)
        xb = xbuf_ref[slot].astype(BF16)
        nxt = jnp.minimum(i + GATHER_AHEAD, nb - 1)
        nslot = (i + GATHER_AHEAD) % NBUF
        for r in range(rb):
            _row_copy(h_hbm, tok_ref[nxt * rb + r], xbuf_ref.at[nslot], r, sem.at[nslot]).start()
        hcat = jnp.dot(xb, w_ref[0], preferred_element_type=F32) + b_ref[0]
        for c in range(hcat.shape[1] // PAIR):
            gate = jnp.minimum(hcat[:, c * PAIR:c * PAIR + LANES], SWIGLU_LIMIT)
            up = jnp.clip(hcat[:, c * PAIR + LANES:(c + 1) * PAIR], -SWIGLU_LIMIT, SWIGLU_LIMIT)
            act = gate * jax.nn.sigmoid(SWIGLU_ALPHA * gate) * (up + 1.0)
            act_ref[:, c * LANES:(c + 1) * LANES] = act.astype(act_ref.dtype)

    for a in range(GATHER_AHEAD):
        j = nb - 1 - a

        @pl.when((i == nb - 1) & (bv_ref[j] != 0))
        def _():
            drain((j + GATHER_AHEAD) % NBUF)


def _moe_up(row_tok, blk_e, blk_valid, h, w_pairs, b_pairs):
    n_rows = row_tok.shape[0]
    rb = EXPERT_ROW_BLOCK
    nb = n_rows // rb
    assert nb > GATHER_AHEAD
    E, D, F2 = w_pairs.shape
    wmap = lambda i, tok, be, bv: (be[i], 0, 0)
    return pl.pallas_call(
        functools.partial(_moe_up_kernel, rb=rb, nb=nb),
        name="moe_gather_up_swiglu",
        grid_spec=pltpu.PrefetchScalarGridSpec(
            num_scalar_prefetch=3,
            grid=(n_rows // rb,),
            in_specs=[
                pl.BlockSpec(memory_space=pl.ANY),
                pl.BlockSpec((1, D, F2), wmap),
                pl.BlockSpec((1, 1, F2), wmap),
            ],
            out_specs=pl.BlockSpec((rb, F2 // 2), lambda i, tok, be, bv: (i, 0)),
            scratch_shapes=[pltpu.VMEM((NBUF, rb, D), F32), pltpu.SemaphoreType.DMA((NBUF,))],
        ),
        out_shape=jax.ShapeDtypeStruct((n_rows, F2 // 2), BF16),
        compiler_params=_params(("arbitrary",), 56),
    )(row_tok, blk_e, blk_valid, h, w_pairs, b_pairs)


def _moe_down_kernel(be_ref, bv_ref, act_ref, wd_ref, bd_ref, rw_ref, y_ref, wb_ref):
    i = pl.program_id(0)

    @pl.when(bv_ref[i] == 0)
    def _():
        y_ref[...] = jnp.zeros_like(y_ref)

    @pl.when((bv_ref[i] != 0) & ((i == 0) | (be_ref[i] != be_ref[jnp.maximum(i - 1, 0)])))
    def _():
        wb_ref[...] = wd_ref[0].astype(BF16)

    @pl.when(bv_ref[i] != 0)
    def _():
        y = (jnp.dot(act_ref[...], wb_ref[...], preferred_element_type=F32) + bd_ref[0]) * rw_ref[...]
        bits = pltpu.bitcast(y.astype(BF16).astype(F32), jnp.uint32)
        half = bits.shape[1] // 2
        y_ref[...] = (bits[:, :half] >> 16) | (bits[:, half:] & jnp.uint32(0xFFFF0000))


def _moe_down(blk_e, blk_valid, act, wd, bd, row_w):
    n_rows, F = act.shape
    rb = EXPERT_ROW_BLOCK
    E, _, D = wd.shape
    wmap = lambda i, be, bv: (be[i], 0, 0)
    return pl.pallas_call(
        _moe_down_kernel,
        name="moe_down",
        grid_spec=pltpu.PrefetchScalarGridSpec(
            num_scalar_prefetch=2,
            grid=(n_rows // rb,),
            in_specs=[
                pl.BlockSpec((rb, F), lambda i, be, bv: (i, 0)),
                pl.BlockSpec((1, F, D), wmap),
                pl.BlockSpec((1, 1, D), wmap),
                pl.BlockSpec((rb, 1), lambda i, be, bv: (i, 0)),
            ],
            out_specs=pl.BlockSpec((rb, D // 2), lambda i, be, bv: (i, 0)),
            scratch_shapes=[pltpu.VMEM((F, D), BF16)],
        ),
        out_shape=jax.ShapeDtypeStruct((n_rows, D // 2), jnp.uint32),
        compiler_params=_params(("arbitrary",), 56),
    )(blk_e, blk_valid, act, wd, bd, row_w.reshape(n_rows, 1))


def _ple_kernel(hb_ref, w_ref, bg_ref, p_ref, wp_ref, h_ref, z_ref, *, alpha):
    k = pl.program_id(2)

    @pl.when(k == 0)
    def _():
        z_ref[...] = jnp.zeros_like(z_ref)

    z_ref[...] += jnp.dot(hb_ref[...], w_ref[...], preferred_element_type=F32)

    @pl.when(k == pl.num_programs(2) - 1)
    def _():
        gate = jax.nn.sigmoid(z_ref[...] + bg_ref[...])
        emb = jnp.dot(p_ref[...].astype(BF16), wp_ref[...], preferred_element_type=F32)
        z_ref[...] = alpha * h_ref[...] + gate * emb


def _ple_residual(hb, w_gate_b, b_gate, p2, w_ple_b, h, alpha):
    T, D = h.shape
    pd = p2.shape[1]
    tm = _pick_tile(T, (1024, 512, 256, 128))
    tn = _pick_tile(D, (1024, 512, 256, 128))
    tk = _pick_tile(D, (2048, 1024, 512, 256, 128))
    return pl.pallas_call(
        functools.partial(_ple_kernel, alpha=alpha),
        name="ple_gate_residual",
        grid=(T // tm, D // tn, D // tk),
        in_specs=[
            pl.BlockSpec((tm, tk), lambda i, j, k: (i, k)),
            pl.BlockSpec((tk, tn), lambda i, j, k: (k, j)),
            pl.BlockSpec((1, tn), lambda i, j, k: (0, j)),
            pl.BlockSpec((tm, pd), lambda i, j, k: (i, 0)),
            pl.BlockSpec((pd, tn), lambda i, j, k: (0, j)),
            pl.BlockSpec((tm, tn), lambda i, j, k: (i, j)),
        ],
        out_specs=pl.BlockSpec((tm, tn), lambda i, j, k: (i, j)),
        out_shape=jax.ShapeDtypeStruct((T, D), F32),
        compiler_params=_params(("parallel", "parallel", "arbitrary"), 48),
    )(hb, w_gate_b, b_gate.reshape(1, D), p2, w_ple_b, h)


def _combine_kernel(pos_ref, y_hbm, z_ref, g_ref, b_ref, o_ref, buf_ref, sem, *, tc, topk, nb):
    i = pl.program_id(0)

    def gather(blk, slot):
        def issue(t, carry):
            for k in range(topk):
                _row_copy(y_hbm, pos_ref[(blk * tc + t) * topk + k], buf_ref.at[slot, k], t, sem.at[slot]).start()
            return carry

        lax.fori_loop(0, tc, issue, 0, unroll=ISSUE_UNROLL // topk)

    def drain(slot):
        for k in range(topk):
            _rows_copy(y_hbm, buf_ref.at[slot, k], sem.at[slot], tc).wait()

    @pl.when(i == 0)
    def _():
        for a in range(GATHER_AHEAD):
            gather(a, a)

    slot = i % NBUF
    drain(slot)
    nxt = jnp.minimum(i + GATHER_AHEAD, nb - 1)
    nslot = (i + GATHER_AHEAD) % NBUF
    for t in range(tc):
        for k in range(topk):
            _row_copy(y_hbm, pos_ref[(nxt * tc + t) * topk + k], buf_ref.at[nslot, k], t, sem.at[nslot]).start()
    lo = hi = None
    for k in range(topk):
        word = buf_ref[slot, k]
        lo_k = pltpu.bitcast(word << 16, F32)
        hi_k = pltpu.bitcast(word & jnp.uint32(0xFFFF0000), F32)
        lo = lo_k if lo is None else lo + lo_k
        hi = hi_k if hi is None else hi + hi_k
    ffn = jnp.concatenate([lo, hi], axis=1)
    o_ref[...] = _layer_norm(z_ref[...] + ffn, g_ref[...], b_ref[...])

    @pl.when(i == nb - 1)
    def _():
        for a in range(GATHER_AHEAD):
            drain((nb - 1 - a + GATHER_AHEAD) % NBUF)


def _combine_ln(pos, y, z, ln_g, ln_b):
    T, D = z.shape
    tc = _pick_tile(T, (128, 64, 32))
    row = lambda i, pos: (i, 0)
    const = lambda i, pos: (0, 0)
    nb = T // tc
    assert nb > GATHER_AHEAD
    return pl.pallas_call(
        functools.partial(_combine_kernel, tc=tc, topk=TOP_K, nb=nb),
        name="moe_combine_ln2",
        grid_spec=pltpu.PrefetchScalarGridSpec(
            num_scalar_prefetch=1,
            grid=(T // tc,),
            in_specs=[
                pl.BlockSpec(memory_space=pl.ANY),
                pl.BlockSpec((tc, D), row),
                pl.BlockSpec((1, D), const),
                pl.BlockSpec((1, D), const),
            ],
            out_specs=pl.BlockSpec((tc, D), row),
            scratch_shapes=[pltpu.VMEM((NBUF, TOP_K, tc, D // 2), jnp.uint32), pltpu.SemaphoreType.DMA((NBUF,))],
        ),
        out_shape=jax.ShapeDtypeStruct((T, D), F32),
        compiler_params=_params(("arbitrary",), 32),
    )(pos, y, z, ln_g.reshape(1, D), ln_b.reshape(1, D))


def _routing_tables(logits, n_experts):
    T = logits.shape[0]
    A = T * TOP_K
    rb = EXPERT_ROW_BLOCK
    i32 = jnp.int32
    top_logit, top_e = lax.top_k(logits[:, :n_experts], TOP_K)
    top_w = jax.nn.softmax(top_logit, axis=-1)
    flat_e = top_e.reshape(-1).astype(i32)
    order = jnp.argsort(flat_e).astype(i32)
    sorted_pos = jnp.argsort(order).astype(i32)
    onehot = (flat_e[:, None] == jnp.arange(n_experts, dtype=i32)[None, :]).astype(i32)
    counts = jnp.sum(onehot, axis=0)
    padded = (counts + rb - 1) // rb * rb
    pad_end = jnp.cumsum(padded)
    pad_start = pad_end - padded
    grp_start = jnp.cumsum(counts) - counts
    pos = (jnp.sum(onehot * (pad_start - grp_start)[None, :], axis=1) + sorted_pos).astype(i32)
    n_blocks = -(-A // rb) + n_experts
    blk_start = jnp.arange(n_blocks, dtype=i32) * rb
    blk_e = jnp.minimum(jnp.sum((pad_end[None, :] <= blk_start[:, None]).astype(i32), axis=1), n_experts - 1)
    blk_valid = (blk_start < pad_end[-1]).astype(i32)
    off = (blk_start - pad_start[blk_e])[:, None] + jnp.arange(rb, dtype=i32)[None, :]
    valid = off < counts[blk_e][:, None]
    src = order[jnp.clip(grp_start[blk_e][:, None] + off, 0, A - 1)]
    row_tok = jnp.where(valid, src // TOP_K, 0).astype(i32).reshape(-1)
    row_w = jnp.where(valid, top_w.reshape(-1)[src], 0.0).reshape(-1)
    return row_tok, row_w, blk_e.astype(i32), blk_valid, pos


def _layer(x2, p2, B, S, w_in, w_pool, pool_scale, cmp_k_pe, cmp_k_w1, cmp_k_w2, cmp_v_pe, cmp_v_w1, cmp_v_w2,
           w_out, ln1_g, ln1_b, w_router, b_router, w_up, b_up, w_down, b_down,
           w_ple_gate, b_ple_gate, w_ple, ln2_g, ln2_b, alpha):
    T, D = x2.shape
    pool_w = D // 2
    nsa_w = D - pool_w
    nh = nsa_w // HEAD_DIM
    in_w = w_in.shape[1]
    kvw = (in_w - D - N_GATES * nh) // N_KV_GROUPS_FIELDS
    G = kvw // HEAD_DIM
    hg = nh // G
    n_experts = w_router.shape[1]

    xb = x2.astype(BF16)
    w_in_b = w_in.astype(BF16)
    proj_a = _matmul_cols(xb, w_in_b, 0, D)
    proj_b = _matmul_cols(xb, w_in_b, D, 2 * kvw)
    proj_c = _matmul_cols(xb, w_in_b, D + 2 * kvw, 4 * kvw)
    w_gates = jnp.pad(w_in_b[:, D + 6 * kvw:], ((0, 0), (0, LANES - N_GATES * nh)))
    gates = _matmul_cols(xb, w_gates, 0, LANES)

    y_pool = _pool_mixer(proj_a, w_pool.astype(BF16), pool_scale, B, S, pool_w)

    cos2, sin2 = _rope_tables(jnp.arange(S))
    nc = S // CMP_STRIDE
    ccos, csin = _rope_tables(jnp.arange(nc) * CMP_STRIDE + CMP_BLOCK - 1)
    kc = _compress(proj_b, 0, cmp_k_pe, cmp_k_w1, cmp_k_w2, ccos, csin, B, S, G, rope=True)
    vc = _compress(proj_b, 1, cmp_v_pe, cmp_v_w1, cmp_v_w2, ccos, csin, B, S, G, rope=False)
    q_rot, k_slc, v_slc, k_win, v_win = _rope_prep(proj_a, proj_c, cos2, sin2, B, S, nsa_w, G)
    o_cmp, sel_bias = _cmp_select(q_rot, kc, vc, B, S, G, hg)
    tq = _pick_tile(S, (256, 128))
    o_slc = _selected_attention(q_rot, k_slc, v_slc, sel_bias, B, S, G, hg, tq)
    gates_g = gates[:, :N_GATES * nh].reshape(T, G, N_GATES * hg).transpose(1, 0, 2)
    gates_g = jnp.pad(gates_g, ((0, 0), (0, 0), (0, LANES - N_GATES * hg)))
    y_nsa = _window_attention_gated(q_rot, k_win, v_win, gates_g, o_cmp, o_slc, B, S, G, hg, tq, WINDOW)

    ne_pad = -(-n_experts // LANES) * LANES
    w_router_p = jnp.pad(w_router, ((0, 0), (0, ne_pad - n_experts))).astype(BF16)
    b_router_p = jnp.pad(b_router, (0, ne_pad - n_experts)).reshape(1, ne_pad)
    h, hb, logits = _mix_ln_router(y_pool, y_nsa, w_out.astype(BF16), x2, ln1_g, ln1_b,
                                   w_router_p, b_router_p, alpha)

    row_tok, row_w, blk_e, blk_valid, pos = _routing_tables(logits, n_experts)
    F = w_down.shape[1]
    w_pairs = _pair_split_cast(w_up)
    b_pairs = b_up.reshape(n_experts, F // LANES, LANES, 2).transpose(0, 1, 3, 2).reshape(n_experts, 1, 2 * F)
    act = _moe_up(row_tok, blk_e, blk_valid, h, w_pairs, b_pairs)
    y = _moe_down(blk_e, blk_valid, act, w_down, b_down.reshape(n_experts, 1, D), row_w)

    z = _ple_residual(hb, w_ple_gate.astype(BF16), b_ple_gate, p2, w_ple.astype(BF16), h, alpha)
    return _combine_ln(pos, y, z, ln2_g, ln2_b)


def kernel(x, p, w_in, w_pool, pool_scale, cmp_k_pe, cmp_k_w1, cmp_k_w2, cmp_v_pe, cmp_v_w1, cmp_v_w2, w_out, ln1_g, ln1_b, w_router, b_router, w_up, b_up, w_down, b_down, w_ple_gate, b_ple_gate, w_ple, ln2_g, ln2_b):
    B, S, D = x.shape
    depth = w_in.shape[0]
    alpha = (2 * depth) ** 0.25
    x2 = x.reshape(B * S, D)
    for i in range(depth):
        x2 = _layer(x2, p[i].reshape(B * S, -1), B, S, w_in[i], w_pool[i], pool_scale[i],
                    cmp_k_pe[i], cmp_k_w1[i], cmp_k_w2[i], cmp_v_pe[i], cmp_v_w1[i], cmp_v_w2[i],
                    w_out[i], ln1_g[i], ln1_b[i], w_router[i], b_router[i], w_up[i], b_up[i],
                    w_down[i], b_down[i], w_ple_gate[i], b_ple_gate[i], w_ple[i], ln2_g[i], ln2_b[i], alpha)
    return x2.reshape(B, S, D)
```
